```python
import math
import jax
import jax.numpy as jnp
from jax import lax
import numpy as np

D_MODEL = 1024
BATCH = 2
SEQ = 8192
DEPTH = 1
DEC_BATCH = 32
DEC_SEQ = 4
PAST_LEN = 16384
PAGE_SIZE = 128

N_MEM = 256
H_SB = 8
DH_SB = 64
H_DIFF = 4
DQ_DIFF = 32
DV_DIFF = 2 * DQ_DIFF
H_MEM = 4
DH_MEM = 64
D_SB = H_SB * DH_SB
D_DIFF = H_DIFF * DV_DIFF
D_MEM = H_MEM * DH_MEM
D_QK_DIFF = H_DIFF * 2 * DQ_DIFF
D_IN = 3 * D_SB + 2 * D_QK_DIFF + D_DIFF + D_MEM
N_BUCKETS = 32
MAX_EXACT = 16
MAX_DISTANCE = 128
Q_BLOCK = 128
N_GROUPS = 4
EXPERTS_PER_GROUP = 8
N_EXPERTS = N_GROUPS * EXPERTS_PER_GROUP
TOP_K = 2
D_EXPERT = 512
MOE_BLOCK = 128
EPS = 1e-6
NEG_INF = -1e30

kernel_name = 'hybrid_sb_diff_mem_hmoe_step'


def rmsnorm(x, g):
    xf = x.astype(jnp.float32)
    y = xf * lax.rsqrt(jnp.mean(xf * xf, axis=-1, keepdims=True) + EPS)
    return (y * g.astype(jnp.float32)).astype(x.dtype)


def split_projection(u, w_in):
    b, t, _ = u.shape
    p = u @ w_in
    offs = [D_SB, 2 * D_SB, 3 * D_SB, 3 * D_SB + D_QK_DIFF, 3 * D_SB + 2 * D_QK_DIFF,
            3 * D_SB + 2 * D_QK_DIFF + D_DIFF]
    q_sb, k_sb, v_sb, q_df, k_df, v_df, q_mem = jnp.split(p, offs, axis=-1)
    return (q_sb.reshape(b, t, H_SB, DH_SB), k_sb.reshape(b, t, H_SB, DH_SB),
            v_sb.reshape(b, t, H_SB, DH_SB),
            q_df.reshape(b, t, H_DIFF, 2 * DQ_DIFF), k_df.reshape(b, t, H_DIFF, 2 * DQ_DIFF),
            v_df.reshape(b, t, H_DIFF, DV_DIFF), q_mem.reshape(b, t, H_MEM, DH_MEM))


def t5_bucket(delta):
    n = jnp.maximum(delta, 0)
    nf = jnp.maximum(n, 1).astype(jnp.float32)
    large = MAX_EXACT + (jnp.log(nf / MAX_EXACT) / math.log(MAX_DISTANCE / MAX_EXACT)
                         * (N_BUCKETS - MAX_EXACT)).astype(jnp.int32)
    large = jnp.minimum(large, N_BUCKETS - 1)
    return jnp.where(n < MAX_EXACT, n, large)


def rel_bias_logits(table, qpos, kpos):
    bucket = t5_bucket(qpos[:, None] - kpos[None, :])
    return jnp.transpose(table[bucket], (2, 0, 1)).astype(jnp.float32)


def stick_breaking(q, k, v, qpos, kpos):
    z = jnp.einsum('bqhd,bkhd->bhqk', q, k, preferred_element_type=jnp.float32) * (DH_SB ** -0.5)
    mask = kpos[None, :] < qpos[:, None]
    log_keep = jnp.where(mask, jax.nn.log_sigmoid(-z), 0.0)
    log_w = jax.nn.log_sigmoid(z) + lax.cumsum(log_keep, axis=3, reverse=True) - log_keep
    a = jnp.where(mask, jnp.exp(log_w), 0.0)
    return jnp.einsum('bhqk,bkhd->bqhd', a.astype(v.dtype), v)


def diff_attention(q, k, v, qpos, kpos, table, lam):
    b, tq = q.shape[:2]
    tk = k.shape[1]
    q = q.reshape(b, tq, H_DIFF, 2, DQ_DIFF)
    k = k.reshape(b, tk, H_DIFF, 2, DQ_DIFF)
    z = jnp.einsum('bqhcd,bkhcd->bchqk', q, k, preferred_element_type=jnp.float32) * (DQ_DIFF ** -0.5)
    z = z + rel_bias_logits(table, qpos, kpos)[None, None]
    mask = kpos[None, :] <= qpos[:, None]
    p = jax.nn.softmax(jnp.where(mask, z, NEG_INF), axis=-1)
    a = p[:, 0] - lam * p[:, 1]
    return jnp.einsum('bhqk,bkhd->bqhd', a.astype(v.dtype), v)


def diff_lambda(lq1, lk1, lq2, lk2, lam_init):
    f = jnp.float32
    return (jnp.exp(jnp.sum(lq1.astype(f) * lk1.astype(f)))
            - jnp.exp(jnp.sum(lq2.astype(f) * lk2.astype(f))) + lam_init)


def query_blocks(fn, q, qpos):
    b, t = q.shape[:2]
    nb = t // Q_BLOCK
    qb = jnp.moveaxis(q.reshape((b, nb, Q_BLOCK) + q.shape[2:]), 1, 0)
    pb = qpos.reshape(nb, Q_BLOCK)
    out = lax.map(lambda args: fn(args[0], args[1]), (qb, pb))
    return jnp.moveaxis(out, 0, 1).reshape((b, t) + out.shape[3:])


def memory_kv(mem, g, w_mem_kv):
    b, m, _ = mem.shape
    kv = rmsnorm(mem, g) @ w_mem_kv
    mk, mv = jnp.split(kv, 2, axis=-1)
    return mk.reshape(b, m, H_MEM, DH_MEM), mv.reshape(b, m, H_MEM, DH_MEM)


def memory_attention(q, mk, mv):
    z = jnp.einsum('bqhd,bmhd->bhqm', q, mk, preferred_element_type=jnp.float32) * (DH_MEM ** -0.5)
    p = jax.nn.softmax(z, axis=-1)
    return jnp.einsum('bhqm,bmhd->bqhd', p.astype(mv.dtype), mv)


def gather_pages(pool, l, page_table):
    g = pool[l, page_table]
    b, npg, ps = g.shape[:3]
    return g.reshape((b, npg * ps) + g.shape[3:])


def hierarchical_moe(x, w_rg, b_rg, w_re, b_re, w_eg, w_eu, w_ed):
    b, t, d = x.shape
    n = b * t
    xf = x.reshape(n, d)
    gl = (xf @ w_rg).astype(jnp.float32) + b_rg
    gp = jax.nn.softmax(gl, axis=-1)
    grp = jnp.argmax(gl, axis=-1).astype(jnp.int32)
    p_grp = jnp.take_along_axis(gp, grp[:, None], axis=-1)
    el = ((xf @ w_re).astype(jnp.float32) + b_re).reshape(n, N_GROUPS, EXPERTS_PER_GROUP)
    el = jnp.take_along_axis(el, grp[:, None, None], axis=1)[:, 0]
    top_val, top_idx = lax.top_k(el, TOP_K)
    wts = jax.nn.softmax(top_val, axis=-1) * p_grp
    eid = (grp[:, None] * EXPERTS_PER_GROUP + top_idx).reshape(-1)
    tok = jnp.repeat(jnp.arange(n, dtype=jnp.int32), TOP_K)
    order = jnp.argsort(eid)
    s_eid, s_tok, s_w = eid[order], tok[order], wts.reshape(-1)[order]
    counts = jnp.bincount(eid, length=N_EXPERTS)
    padded = (counts + MOE_BLOCK - 1) // MOE_BLOCK * MOE_BLOCK
    raw_start = jnp.cumsum(counts) - counts
    pad_end = jnp.cumsum(padded)
    pad_start = pad_end - padded
    dest = pad_start[s_eid] + (jnp.arange(n * TOP_K, dtype=jnp.int32) - raw_start[s_eid])
    n_blocks = -(-(n * TOP_K + N_EXPERTS * (MOE_BLOCK - 1)) // MOE_BLOCK)
    n_rows = n_blocks * MOE_BLOCK
    row_tok = jnp.full((n_rows,), n, jnp.int32).at[dest].set(s_tok)
    row_w = jnp.zeros((n_rows,), jnp.float32).at[dest].set(s_w)
    blk_exp = jnp.minimum(jnp.searchsorted(pad_end, jnp.arange(n_blocks, dtype=jnp.int32) * MOE_BLOCK,
                                           side='right'), N_EXPERTS - 1)
    x_pad = jnp.concatenate([xf, jnp.zeros((1, d), xf.dtype)], axis=0)
    xb = x_pad[row_tok].reshape(n_blocks, MOE_BLOCK, d)

    def expert_block(args):
        xe, e = args
        h = jax.nn.silu(xe @ w_eg[e]) * (xe @ w_eu[e])
        return h @ w_ed[e]

    yb = lax.map(expert_block, (xb, blk_exp)).reshape(n_rows, d)
    y = jax.ops.segment_sum(yb * row_w[:, None].astype(yb.dtype), row_tok, num_segments=n + 1)[:n]
    return y.reshape(b, t, d)


def finish_layer(x, u, o_sb, o_df, o_mem, lam_init, diff_norm_g, w_gate, b_gate, w_br_sb, w_br_diff,
                 w_br_mem, w_out, norm_ffn_g, w_rg, b_rg, w_re, b_re, w_eg, w_eu, w_ed):
    b, t, _ = x.shape
    o_df = rmsnorm(o_df, diff_norm_g) * (1.0 - lam_init)
    gates = jax.nn.sigmoid(u @ w_gate + b_gate)
    g_sb, g_df, g_mem = jnp.split(gates, 3, axis=-1)
    h = (g_sb * (o_sb.reshape(b, t, D_SB) @ w_br_sb)
         + g_df * (o_df.reshape(b, t, D_DIFF) @ w_br_diff)
         + g_mem * (o_mem.reshape(b, t, D_MEM) @ w_br_mem))
    x = x + h @ w_out
    return x + hierarchical_moe(rmsnorm(x, norm_ffn_g), w_rg, b_rg, w_re, b_re, w_eg, w_eu, w_ed)


def setup_inputs(seed: int = 0) -> dict:
    key = jax.random.key(seed)
    ks = jax.random.split(key, 36)
    n_pages = PAST_LEN // PAGE_SIZE
    n_used = DEC_BATCH * n_pages
    n_pool = n_used + n_used // 4
    s = D_MODEL ** -0.5

    def nrm(k, shape, scale):
        return scale * jax.random.normal(k, shape, jnp.float32)

    def gain(k, shape):
        return 1.0 + 0.05 * jax.random.normal(k, shape, jnp.float32)

    return {
        'x_prompt': nrm(ks[0], (BATCH, SEQ, D_MODEL), 1.0),
        'x_sample': nrm(ks[1], (DEC_BATCH, DEC_SEQ, D_MODEL), 1.0),
        'cache_sb_k': nrm(ks[2], (DEPTH, n_pool, PAGE_SIZE, H_SB, DH_SB), 1.0),
        'cache_sb_v': nrm(ks[3], (DEPTH, n_pool, PAGE_SIZE, H_SB, DH_SB), 1.0),
        'cache_diff_k': nrm(ks[4], (DEPTH, n_pool, PAGE_SIZE, H_DIFF, 2 * DQ_DIFF), 1.0),
        'cache_diff_v': nrm(ks[5], (DEPTH, n_pool, PAGE_SIZE, H_DIFF, DV_DIFF), 1.0),
        'cache_mem_k': nrm(ks[6], (DEPTH, DEC_BATCH, N_MEM, H_MEM, DH_MEM), 1.0),
        'cache_mem_v': nrm(ks[7], (DEPTH, DEC_BATCH, N_MEM, H_MEM, DH_MEM), 1.0),
        'page_table': jax.random.permutation(ks[8], n_pool)[:n_used].reshape(DEC_BATCH, n_pages).astype(jnp.int32),
        'mem_prompt': nrm(ks[9], (BATCH, N_MEM, D_MODEL), 1.0),
        'norm_mix_g': gain(ks[10], (DEPTH, D_MODEL)),
        'w_in': nrm(ks[11], (DEPTH, D_MODEL, D_IN), s),
        'diff_lam_q1': nrm(ks[12], (DEPTH, DQ_DIFF), 0.1),
        'diff_lam_k1': nrm(ks[13], (DEPTH, DQ_DIFF), 0.1),
        'diff_lam_q2': nrm(ks[14], (DEPTH, DQ_DIFF), 0.1),
        'diff_lam_k2': nrm(ks[15], (DEPTH, DQ_DIFF), 0.1),
        'diff_norm_g': gain(ks[16], (DEPTH, DV_DIFF)),
        'mem_norm_g': gain(ks[17], (DEPTH, D_MODEL)),
        'w_mem_kv': nrm(ks[18], (DEPTH, D_MODEL, 2 * D_MEM), s),
        'w_gate': nrm(ks[19], (DEPTH, D_MODEL, 3 * D_MODEL), s),
        'b_gate': nrm(ks[20], (DEPTH, 3 * D_MODEL), 0.02),
        'w_br_sb': nrm(ks[21], (DEPTH, D_SB, D_MODEL), D_SB ** -0.5),
        'w_br_diff': nrm(ks[22], (DEPTH, D_DIFF, D_MODEL), D_DIFF ** -0.5),
        'w_br_mem': nrm(ks[23], (DEPTH, D_MEM, D_MODEL), D_MEM ** -0.5),
        'w_out': nrm(ks[24], (DEPTH, D_MODEL, D_MODEL), s),
        'norm_ffn_g': gain(ks[25], (DEPTH, D_MODEL)),
        'w_router_group': nrm(ks[26], (DEPTH, D_MODEL, N_GROUPS), s),
        'b_router_group': nrm(ks[27], (DEPTH, N_GROUPS), 0.01),
        'w_router_expert': nrm(ks[28], (DEPTH, D_MODEL, N_EXPERTS), s),
        'b_router_expert': nrm(ks[29], (DEPTH, N_EXPERTS), 0.01),
        'w_exp_gate': nrm(ks[30], (DEPTH, N_EXPERTS, D_MODEL, D_EXPERT), s),
        'w_exp_up': nrm(ks[31], (DEPTH, N_EXPERTS, D_MODEL, D_EXPERT), s),
        'w_exp_down': nrm(ks[32], (DEPTH, N_EXPERTS, D_EXPERT, D_MODEL), D_EXPERT ** -0.5),
        'rel_bias': nrm(ks[33], (N_BUCKETS, H_DIFF), 0.5),
        'norm_final_g': gain(ks[34], (D_MODEL,)),
    }


def reference(x_prompt, x_sample, cache_sb_k, cache_sb_v, cache_diff_k, cache_diff_v, cache_mem_k,
              cache_mem_v, page_table, mem_prompt, norm_mix_g, w_in, diff_lam_q1, diff_lam_k1,
              diff_lam_q2, diff_lam_k2, diff_norm_g, mem_norm_g, w_mem_kv, w_gate, b_gate, w_br_sb,
              w_br_diff, w_br_mem, w_out, norm_ffn_g, w_router_group, b_router_group, w_router_expert,
              b_router_expert, w_exp_gate, w_exp_up, w_exp_down, rel_bias, norm_final_g):
    xp, xs = x_prompt, x_sample
    tp = xp.shape[1]
    ts = xs.shape[1]
    past_len = page_table.shape[1] * PAGE_SIZE
    pos_p = jnp.arange(tp, dtype=jnp.int32)
    qpos_s = past_len + jnp.arange(ts, dtype=jnp.int32)
    kpos_s = jnp.arange(past_len + ts, dtype=jnp.int32)
    sbk_p, sbv_p, dfk_p, dfv_p, mk_p, mv_p = [], [], [], [], [], []
    sbk_s, sbv_s, dfk_s, dfv_s = [], [], [], []
    for l in range(DEPTH):
        lam_init = 0.8 - 0.6 * math.exp(-0.3 * l)
        lam = diff_lambda(diff_lam_q1[l], diff_lam_k1[l], diff_lam_q2[l], diff_lam_k2[l], lam_init)
        ffn_args = (lam_init, diff_norm_g[l], w_gate[l], b_gate[l], w_br_sb[l], w_br_diff[l], w_br_mem[l],
                    w_out[l], norm_ffn_g[l], w_router_group[l], b_router_group[l], w_router_expert[l],
                    b_router_expert[l], w_exp_gate[l], w_exp_up[l], w_exp_down[l])
        u = rmsnorm(xp, norm_mix_g[l])
        q_sb, k_sb, v_sb, q_df, k_df, v_df, q_mem = split_projection(u, w_in[l])
        o_sb = query_blocks(lambda qb, pb: stick_breaking(qb, k_sb, v_sb, pb, pos_p), q_sb, pos_p)
        o_df = query_blocks(lambda qb, pb: diff_attention(qb, k_df, v_df, pb, pos_p, rel_bias, lam), q_df, pos_p)
        mk, mv = memory_kv(mem_prompt, mem_norm_g[l], w_mem_kv[l])
        o_mem = memory_attention(q_mem, mk, mv)
        xp = finish_layer(xp, u, o_sb, o_df, o_mem, *ffn_args)
        sbk_p.append(k_sb)
        sbv_p.append(v_sb)
        dfk_p.append(k_df)
        dfv_p.append(v_df)
        mk_p.append(mk)
        mv_p.append(mv)
        u = rmsnorm(xs, norm_mix_g[l])
        q_sb, k_sb, v_sb, q_df, k_df, v_df, q_mem = split_projection(u, w_in[l])
        k_sb_all = jnp.concatenate([gather_pages(cache_sb_k, l, page_table), k_sb], axis=1)
        v_sb_all = jnp.concatenate([gather_pages(cache_sb_v, l, page_table), v_sb], axis=1)
        o_sb = stick_breaking(q_sb, k_sb_all, v_sb_all, qpos_s, kpos_s)
        k_df_all = jnp.concatenate([gather_pages(cache_diff_k, l, page_table), k_df], axis=1)
        v_df_all = jnp.concatenate([gather_pages(cache_diff_v, l, page_table), v_df], axis=1)
        o_df = diff_attention(q_df, k_df_all, v_df_all, qpos_s, kpos_s, rel_bias, lam)
        o_mem = memory_attention(q_mem, cache_mem_k[l], cache_mem_v[l])
        xs = finish_layer(xs, u, o_sb, o_df, o_mem, *ffn_args)
        sbk_s.append(k_sb)
        sbv_s.append(v_sb)
        dfk_s.append(k_df)
        dfv_s.append(v_df)
    y_prompt = rmsnorm(xp, norm_final_g)
    y_sample = rmsnorm(xs, norm_final_g)
    return (y_prompt, y_sample, jnp.stack(sbk_p), jnp.stack(sbv_p), jnp.stack(dfk_p), jnp.stack(dfv_p),
            jnp.stack(mk_p), jnp.stack(mv_p), jnp.stack(sbk_s), jnp.stack(sbv_s), jnp.stack(dfk_s),
            jnp.stack(dfv_s))
```

```python
import functools
import math

import jax
import jax.numpy as jnp
from jax import lax
from jax.experimental import pallas as pl
from jax.experimental.pallas import tpu as pltpu

F32 = jnp.float32
BF16 = jnp.bfloat16
I32 = jnp.int32

D_MODEL = 1024
PAGE_SIZE = 128
H_SB = 8
DH_SB = 64
H_DIFF = 4
DQ_DIFF = 32
DV_DIFF = 64
H_MEM = 4
DH_MEM = 64
D_SB = H_SB * DH_SB
D_DIFF = H_DIFF * DV_DIFF
D_MEM = H_MEM * DH_MEM
D_QK_DIFF = H_DIFF * 2 * DQ_DIFF
D_IN = 3 * D_SB + 2 * D_QK_DIFF + D_DIFF + D_MEM
N_BUCKETS = 32
MAX_EXACT = 16
MAX_DISTANCE = 128
N_GROUPS = 4
EXPERTS_PER_GROUP = 8
N_EXPERTS = N_GROUPS * EXPERTS_PER_GROUP
D_EXPERT = 512
EPS = 1e-6
NEG_INF = -1e30

LANES = 128
SUBLANES = 8
VMEM_LIMIT = 48 * 1024 * 1024

OFF_Q_SB = 0
OFF_K_SB = D_SB
OFF_V_SB = 2 * D_SB
OFF_Q_DF = 3 * D_SB
OFF_K_DF = OFF_Q_DF + D_QK_DIFF
OFF_V_DF = OFF_K_DF + D_QK_DIFF
OFF_Q_MEM = OFF_V_DF + D_DIFF

ATT_BLOCK = 256
MOE_ROWS = 256
MOE_ROWS_LOG2 = 8
TOK_TILE = 256
ROW_TILE = 128
PAGES_PER_STEP = 4
DEC_COLS = 64

R_EID1, R_EID2, R_W1, R_W2 = 0, 1, 2, 3


def _params(sem, vmem=VMEM_LIMIT):
    return pltpu.CompilerParams(dimension_semantics=sem, vmem_limit_bytes=vmem)


def _rmsnorm(x, g):
    ms = jnp.mean(x * x, axis=-1, keepdims=True)
    return (x * lax.rsqrt(ms + EPS)) * g


def _dot(a, b):
    return jnp.dot(a, b, preferred_element_type=F32)


def _dot_nt(a, b):
    return lax.dot_general(a, b, (((1,), (1,)), ((), ())), preferred_element_type=F32)


def _split_bf16(x):
    hi = x.astype(BF16)
    lo = (x - hi.astype(F32)).astype(BF16)
    return hi, lo


def _softplus(z):
    return jnp.maximum(z, 0.0) + jnp.log(1.0 + jnp.exp(-jnp.abs(z)))


def _row_to_col(v, n):
    r = lax.broadcasted_iota(I32, (n, LANES), 0)
    c = lax.broadcasted_iota(I32, (n, LANES), 1)
    return jnp.sum(jnp.where(r == c, v, 0.0), axis=1, keepdims=True)


def _norm_proj_kernel(x_ref, g_ref, w_ref, s_ref, *out_refs, f32_cols):
    u = _rmsnorm(x_ref[...], g_ref[...]).astype(BF16)
    p = _dot(u, w_ref[...])
    for ref, (lo, width) in zip(out_refs[:-1], f32_cols):
        ref[...] = p[:, lo:lo + width]
    out_refs[-1][...] = (p * s_ref[...]).astype(BF16)


def _norm_proj(x, g, w_bf16, col_scale, f32_cols, name):
    n, d = x.shape
    n_out = w_bf16.shape[1]
    tm = min(TOK_TILE, n)
    out_shape = [jax.ShapeDtypeStruct((n, width), F32) for _, width in f32_cols]
    out_shape.append(jax.ShapeDtypeStruct((n, n_out), BF16))
    out_specs = [pl.BlockSpec((tm, width), lambda i: (i, 0)) for _, width in f32_cols]
    out_specs.append(pl.BlockSpec((tm, n_out), lambda i: (i, 0)))
    return pl.pallas_call(
        functools.partial(_norm_proj_kernel, f32_cols=tuple(f32_cols)),
        grid=(n // tm,),
        in_specs=[
            pl.BlockSpec((tm, d), lambda i: (i, 0)),
            pl.BlockSpec((1, d), lambda i: (0, 0)),
            pl.BlockSpec((d, n_out), lambda i: (0, 0)),
            pl.BlockSpec((1, n_out), lambda i: (0, 0)),
        ],
        out_specs=out_specs,
        out_shape=out_shape,
        compiler_params=_params(("parallel",)),
        name=name,
    )(x, g.reshape(1, d), w_bf16, col_scale)


def _t5_bucket(delta):
    n = jnp.maximum(delta, 0)
    nf = jnp.maximum(n, 1).astype(F32)
    large = MAX_EXACT + (jnp.log(nf / MAX_EXACT) / math.log(MAX_DISTANCE / MAX_EXACT)
                         * (N_BUCKETS - MAX_EXACT)).astype(I32)
    large = jnp.minimum(large, N_BUCKETS - 1)
    return jnp.where(n < MAX_EXACT, n, large)


def _bias_of_bucket(bucket, rel_ref, head):
    out = jnp.zeros(bucket.shape, F32)
    for b in range(N_BUCKETS):
        out = jnp.where(bucket == b, rel_ref[b, head], out)
    return out


def _bias_by_column(bucket, col_head, rel_ref):
    acc = jnp.zeros(bucket.shape, F32)
    for h in range(H_DIFF):
        acc = jnp.where(col_head == h, _bias_of_bucket(bucket, rel_ref, h), acc)
    return acc


def _prep_kernel(rel_ref, lam_ref, tz_ref, dec_ref, lam_out_ref, *, blk, lam_init):
    r = lax.broadcasted_iota(I32, (blk, blk), 0)
    c = lax.broadcasted_iota(I32, (blk, blk), 1)
    for off in range(2):
        bucket = _t5_bucket(r - c + off * blk)
        for h in range(H_DIFF):
            tz_ref[h, off] = _bias_of_bucket(bucket, rel_ref, h)
    col = lax.broadcasted_iota(I32, (PAGE_SIZE, LANES), 1)
    row = lax.broadcasted_iota(I32, (PAGE_SIZE, LANES), 0)
    qi = jnp.right_shift(col, 3) & 3
    head = col & (SUBLANES - 1)
    dec_ref[0] = _bias_by_column(_t5_bucket(PAGE_SIZE + qi - row), head, rel_ref)
    dec_ref[1] = _bias_by_column(_t5_bucket(qi - row), head, rel_ref)
    dec_ref[2] = _bias_by_column(jnp.full((PAGE_SIZE, LANES), N_BUCKETS - 1, I32), head, rel_ref)
    lq1, lk1, lq2, lk2 = lam_ref[0:1, :], lam_ref[1:2, :], lam_ref[2:3, :], lam_ref[3:4, :]
    lam = (jnp.exp(jnp.sum(lq1 * lk1, axis=-1, keepdims=True))
           - jnp.exp(jnp.sum(lq2 * lk2, axis=-1, keepdims=True)) + lam_init)
    lam_out_ref[...] = jnp.broadcast_to(lam, (SUBLANES, LANES))


def _prep(rel_bias, lam_vecs, blk, lam_init):
    return pl.pallas_call(
        functools.partial(_prep_kernel, blk=blk, lam_init=lam_init),
        in_specs=[
            pl.BlockSpec(memory_space=pltpu.SMEM),
            pl.BlockSpec(memory_space=pltpu.VMEM),
        ],
        out_specs=[pl.BlockSpec(memory_space=pltpu.VMEM)] * 3,
        out_shape=[
            jax.ShapeDtypeStruct((H_DIFF, 2, blk, blk), F32),
            jax.ShapeDtypeStruct((3, PAGE_SIZE, LANES), F32),
            jax.ShapeDtypeStruct((SUBLANES, LANES), F32),
        ],
        name="prep_bias_lambda",
    )(rel_bias, lam_vecs)


def _sb_prompt_kernel(q_ref, k_ref, v_ref, o_ref, acc_ref, c_ref, *, blk):
    i = pl.program_id(2)
    r = lax.broadcasted_iota(I32, (blk, blk), 0)
    c = lax.broadcasted_iota(I32, (blk, blk), 1)
    tri = (r >= c).astype(BF16)
    strict = c < r

    def block(j, masked):
        start = pl.multiple_of(j * blk, blk)
        for hh in range(2):
            lo = hh * DH_SB
            q = q_ref[0, :, lo:lo + DH_SB]
            k = k_ref[0, pl.ds(start, blk), lo:lo + DH_SB]
            v = v_ref[0, pl.ds(start, blk), lo:lo + DH_SB]
            z = _dot_nt(q, k)
            log_keep = -_softplus(z)
            if masked:
                log_keep = jnp.where(strict, log_keep, 0.0)
            hi, lo_part = _split_bf16(log_keep)
            suffix = _dot(hi, tri) + _dot(lo_part, tri)
            log_w = z + suffix + c_ref[hh]
            if masked:
                log_w = jnp.where(strict, log_w, NEG_INF)
            a = jnp.exp(log_w)
            acc_ref[hh] += _dot(a.astype(BF16), v)
            c_ref[hh] += suffix[:, 0:1]

    acc_ref[...] = jnp.zeros_like(acc_ref)
    c_ref[...] = jnp.zeros_like(c_ref)
    block(i, True)

    def body(kk, carry):
        block(i - 1 - kk, False)
        return carry

    lax.fori_loop(0, i, body, 0)
    o_ref[0] = jnp.concatenate([acc_ref[0], acc_ref[1]], axis=1).astype(BF16)


def _sb_prompt(pb, blk):
    b, t, _ = pb.shape
    qb, kb, vb = OFF_Q_SB // LANES, OFF_K_SB // LANES, OFF_V_SB // LANES
    return pl.pallas_call(
        functools.partial(_sb_prompt_kernel, blk=blk),
        grid=(b, H_SB // 2, t // blk),
        in_specs=[
            pl.BlockSpec((1, blk, LANES), lambda bi, hp, i: (bi, i, qb + hp)),
            pl.BlockSpec((1, t, LANES), lambda bi, hp, i: (bi, 0, kb + hp)),
            pl.BlockSpec((1, t, LANES), lambda bi, hp, i: (bi, 0, vb + hp)),
        ],
        out_specs=pl.BlockSpec((1, blk, LANES), lambda bi, hp, i: (bi, i, hp)),
        out_shape=jax.ShapeDtypeStruct((b, t, D_SB), BF16),
        scratch_shapes=[pltpu.VMEM((2, blk, DH_SB), F32), pltpu.VMEM((2, blk, 1), F32)],
        compiler_params=_params(("parallel", "parallel", "arbitrary")),
        name="sb_prompt",
    )(pb, pb, pb)


def _diff_prompt_kernel(lam_ref, rel_ref, q_ref, k_ref, v_ref, tz_ref, g_ref, o_ref,
                        q2_ref, m_ref, l_ref, acc_ref, *, blk, lam_init):
    hp = pl.program_id(1)
    i = pl.program_id(2)
    r = lax.broadcasted_iota(I32, (2 * blk, blk), 0)
    c = lax.broadcasted_iota(I32, (2 * blk, blk), 1)
    causal = c <= jnp.where(r >= blk, r - blk, r)
    lane = lax.broadcasted_iota(I32, (blk, DV_DIFF), 1)
    scale = DQ_DIFF ** -0.5
    for hh in range(2):
        q = q_ref[0, :, hh * DV_DIFF:(hh + 1) * DV_DIFF]
        zero = jnp.zeros_like(q)
        q2_ref[hh, 0:blk, :] = jnp.where(lane < DQ_DIFF, q, zero)
        q2_ref[hh, blk:2 * blk, :] = jnp.where(lane >= DQ_DIFF, q, zero)
    m_ref[...] = jnp.full_like(m_ref, NEG_INF)
    l_ref[...] = jnp.zeros_like(l_ref)
    acc_ref[...] = jnp.zeros_like(acc_ref)

    def block(j, kind):
        start = pl.multiple_of(j * blk, blk)
        for hh in range(2):
            lo = hh * DV_DIFF
            k = k_ref[0, pl.ds(start, blk), lo:lo + DV_DIFF]
            v = v_ref[0, pl.ds(start, blk), lo:lo + DV_DIFF]
            z = _dot_nt(q2_ref[hh], k) * scale
            if kind == 2:
                z = z + rel_ref[N_BUCKETS - 1, hp * 2 + hh]
            else:
                bias = tz_ref[hh, kind]
                z = z + jnp.concatenate([bias, bias], axis=0)
            if kind == 0:
                z = jnp.where(causal, z, NEG_INF)
            m_prev = m_ref[hh]
            m_new = jnp.maximum(m_prev, jnp.max(z, axis=1, keepdims=True))
            alpha = jnp.exp(m_prev - m_new)
            p = jnp.exp(z - m_new)
            l_ref[hh] = alpha * l_ref[hh] + jnp.sum(p, axis=1, keepdims=True)
            acc_ref[hh] = alpha * acc_ref[hh] + _dot(p.astype(BF16), v)
            m_ref[hh] = m_new

    block(i, 0)

    @pl.when(i >= 1)
    def _():
        block(i - 1, 1)

    def body(kk, carry):
        block(i - 2 - kk, 2)
        return carry

    lax.fori_loop(0, jnp.maximum(i - 1, 0), body, 0)

    lam = lam_ref[0, 0]
    outs = []
    for hh in range(2):
        o = acc_ref[hh] / l_ref[hh]
        o = o[0:blk] - lam * o[blk:2 * blk]
        outs.append(_rmsnorm(o, g_ref[...]) * (1.0 - lam_init))
    o_ref[0] = jnp.concatenate(outs, axis=1).astype(BF16)


def _diff_prompt(pb, tz, lam, rel_bias, g_diff, blk, lam_init):
    b, t, _ = pb.shape
    qb, kb, vb = OFF_Q_DF // LANES, OFF_K_DF // LANES, OFF_V_DF // LANES
    return pl.pallas_call(
        functools.partial(_diff_prompt_kernel, blk=blk, lam_init=lam_init),
        grid=(b, H_DIFF // 2, t // blk),
        in_specs=[
            pl.BlockSpec(memory_space=pltpu.SMEM),
            pl.BlockSpec(memory_space=pltpu.SMEM),
            pl.BlockSpec((1, blk, LANES), lambda bi, hp, i: (bi, i, qb + hp)),
            pl.BlockSpec((1, t, LANES), lambda bi, hp, i: (bi, 0, kb + hp)),
            pl.BlockSpec((1, t, LANES), lambda bi, hp, i: (bi, 0, vb + hp)),
            pl.BlockSpec((2, 2, blk, blk), lambda bi, hp, i: (hp, 0, 0, 0)),
            pl.BlockSpec((1, DV_DIFF), lambda bi, hp, i: (0, 0)),
        ],
        out_specs=pl.BlockSpec((1, blk, LANES), lambda bi, hp, i: (bi, i, hp)),
        out_shape=jax.ShapeDtypeStruct((b, t, D_DIFF), BF16),
        scratch_shapes=[
            pltpu.VMEM((2, 2 * blk, DV_DIFF), BF16),
            pltpu.VMEM((2, 2 * blk, 1), F32),
            pltpu.VMEM((2, 2 * blk, 1), F32),
            pltpu.VMEM((2, 2 * blk, DV_DIFF), F32),
        ],
        compiler_params=_params(("parallel", "parallel", "arbitrary")),
        name="diff_prompt",
    )(lam, rel_bias, pb, pb, pb, tz, g_diff.reshape(1, DV_DIFF))


def _mem_attn_kernel(q_ref, k_ref, v_ref, o_ref, *, tq):
    rows = max(tq, SUBLANES)
    q_all = q_ref[0].astype(F32) * (DH_MEM ** -0.5)
    if rows > tq:
        q_all = jnp.concatenate([q_all, jnp.zeros((rows - tq, D_MEM), F32)], axis=0)
    q_all = q_all.astype(BF16)
    outs = []
    for h in range(H_MEM):
        lo = h * DH_MEM
        q = q_all[:, lo:lo + DH_MEM]
        k = k_ref[0, :, lo:lo + DH_MEM].astype(BF16)
        v = v_ref[0, :, lo:lo + DH_MEM].astype(BF16)
        z = _dot_nt(q, k)
        p = jnp.exp(z - jnp.max(z, axis=1, keepdims=True))
        outs.append(_dot(p.astype(BF16), v) / jnp.sum(p, axis=1, keepdims=True))
    o_ref[0] = jnp.concatenate(outs, axis=1)[0:tq]


def _mem_attn(q, q_block, k, k_block, v, v_block, tq, name):
    b, t, _ = q.shape
    m = k.shape[1]
    return pl.pallas_call(
        functools.partial(_mem_attn_kernel, tq=tq),
        grid=(b, t // tq),
        in_specs=[
            pl.BlockSpec((1, tq, D_MEM), lambda bi, i: (bi, i, q_block)),
            pl.BlockSpec((1, m, D_MEM), lambda bi, i: (bi, 0, k_block)),
            pl.BlockSpec((1, m, D_MEM), lambda bi, i: (bi, 0, v_block)),
        ],
        out_specs=pl.BlockSpec((1, tq, D_MEM), lambda bi, i: (bi, i, 0)),
        out_shape=jax.ShapeDtypeStruct((b, t, D_MEM), F32),
        compiler_params=_params(("parallel", "parallel")),
        name=name,
    )(q, k, v)


def _pad_rows(x, rows):
    return jnp.concatenate([x, jnp.zeros((rows - x.shape[0], x.shape[1]), x.dtype)], axis=0)


def _query_columns(q, n_heads, width):
    t = q.shape[0]
    rows = jnp.concatenate(
        [jnp.broadcast_to(q[i:i + 1], (SUBLANES, q.shape[1])) for i in range(t)], axis=0)
    r = lax.broadcasted_iota(I32, rows.shape, 0)
    c = lax.broadcasted_iota(I32, rows.shape, 1)
    rows = jnp.where(jnp.right_shift(c, int(math.log2(width))) == (r & (SUBLANES - 1)), rows, 0.0)
    return rows


def _sb_decode_kernel(pt_ref, q_ref, kn_ref, vn_ref, *refs, n_tok, n_pages):
    k_refs = refs[:PAGES_PER_STEP]
    v_refs = refs[PAGES_PER_STEP:2 * PAGES_PER_STEP]
    o_ref, qt_ref, acc_ref, c_ref = refs[2 * PAGES_PER_STEP:]
    j = pl.program_id(1)
    n_cols = n_tok * SUBLANES
    r = lax.broadcasted_iota(I32, (PAGE_SIZE, LANES), 0)
    c = lax.broadcasted_iota(I32, (PAGE_SIZE, LANES), 1)
    tri = (c >= r).astype(BF16)

    def page(k, v, mask):
        z = _dot_nt(k, qt_ref[...])
        log_keep = -_softplus(z)
        if mask is not None:
            log_keep = jnp.where(mask, log_keep, 0.0)
        hi, lo_part = _split_bf16(log_keep)
        suffix = _dot(tri, hi) + _dot(tri, lo_part)
        log_w = z + suffix + c_ref[...]
        if mask is not None:
            log_w = jnp.where(mask, log_w, NEG_INF)
        a_t = jnp.exp(log_w).T[0:n_cols]
        acc_ref[...] += _dot(a_t.astype(BF16), v)
        c_ref[...] += suffix[0:1, :]

    @pl.when(j == 0)
    def _():
        q = q_ref[0] * (DH_SB ** -0.5)
        qt_ref[...] = _pad_rows(_query_columns(q, H_SB, DH_SB), LANES).astype(BF16)
        acc_ref[...] = jnp.zeros_like(acc_ref)
        c_ref[...] = jnp.zeros_like(c_ref)
        page(_pad_rows(kn_ref[0], PAGE_SIZE).astype(BF16),
             _pad_rows(vn_ref[0], PAGE_SIZE).astype(BF16), r < jnp.right_shift(c, 3))

    for p in range(PAGES_PER_STEP):
        page(k_refs[p][0].astype(BF16), v_refs[p][0].astype(BF16), None)

    @pl.when(j == n_pages // PAGES_PER_STEP - 1)
    def _():
        acc = acc_ref[...]
        rr = lax.broadcasted_iota(I32, acc.shape, 0)
        cc = lax.broadcasted_iota(I32, acc.shape, 1)
        acc = jnp.where(jnp.right_shift(cc, 6) == (rr & (SUBLANES - 1)), acc, 0.0)
        o_ref[0] = jnp.sum(acc.reshape(n_tok, SUBLANES, D_SB), axis=1)


def _page_specs(width, n_pages):
    def spec(p):
        return pl.BlockSpec(
            (1, PAGE_SIZE, width),
            lambda bi, j, pt: (pt[bi, n_pages - 1 - (j * PAGES_PER_STEP + p)], 0, 0))
    return [spec(p) for p in range(PAGES_PER_STEP)]


def _sb_decode(q, k_new, v_new, cache_k, cache_v, page_table):
    b, t, _ = q.shape
    n_pages = page_table.shape[1]
    tok = pl.BlockSpec((1, t, D_SB), lambda bi, j, pt: (bi, 0, 0))
    grid_spec = pltpu.PrefetchScalarGridSpec(
        num_scalar_prefetch=1,
        grid=(b, n_pages // PAGES_PER_STEP),
        in_specs=[tok, tok, tok] + _page_specs(D_SB, n_pages) + _page_specs(D_SB, n_pages),
        out_specs=tok,
        scratch_shapes=[
            pltpu.VMEM((LANES, D_SB), BF16),
            pltpu.VMEM((t * SUBLANES, D_SB), F32),
            pltpu.VMEM((1, LANES), F32),
        ],
    )
    return pl.pallas_call(
        functools.partial(_sb_decode_kernel, n_tok=t, n_pages=n_pages),
        grid_spec=grid_spec,
        out_shape=jax.ShapeDtypeStruct((b, t, D_SB), F32),
        compiler_params=_params(("parallel", "arbitrary")),
        name="sb_decode",
    )(page_table, q, k_new, v_new, *([cache_k] * PAGES_PER_STEP), *([cache_v] * PAGES_PER_STEP))


def _diff_decode_kernel(pt_ref, lam_ref, q_ref, kn_ref, vn_ref, bias_ref, g_ref, *refs,
                        n_tok, n_pages, lam_init):
    k_refs = refs[:PAGES_PER_STEP]
    v_refs = refs[PAGES_PER_STEP:2 * PAGES_PER_STEP]
    o_ref, qt_ref, acc_ref, m_ref, l_ref = refs[2 * PAGES_PER_STEP:]
    j = pl.program_id(1)
    half = n_tok * SUBLANES
    r = lax.broadcasted_iota(I32, (PAGE_SIZE, LANES), 0)
    c = lax.broadcasted_iota(I32, (PAGE_SIZE, LANES), 1)
    scale = DQ_DIFF ** -0.5

    def page(k, v, bias, mask):
        z = _dot_nt(k, qt_ref[...]) * scale + bias
        if mask is not None:
            z = jnp.where(mask, z, NEG_INF)
        m_prev = m_ref[...]
        m_new = jnp.maximum(m_prev, jnp.max(z, axis=0, keepdims=True))
        alpha = jnp.exp(m_prev - m_new)
        p = jnp.exp(z - m_new)
        l_ref[...] = alpha * l_ref[...] + jnp.sum(p, axis=0, keepdims=True)
        p_t = p.T[0:2 * half]
        acc_ref[...] = _row_to_col(alpha, 2 * half) * acc_ref[...] + _dot(p_t.astype(BF16), v)
        m_ref[...] = m_new

    @pl.when(j == 0)
    def _():
        q = _query_columns(q_ref[0], H_DIFF, 2 * DQ_DIFF)
        lane = lax.broadcasted_iota(I32, q.shape, 1)
        first = (jnp.right_shift(lane, 5) & 1) == 0
        q2 = jnp.concatenate([jnp.where(first, q, 0.0), jnp.where(first, 0.0, q)], axis=0)
        qt_ref[...] = _pad_rows(q2, LANES).astype(BF16)
        acc_ref[...] = jnp.zeros_like(acc_ref)
        m_ref[...] = jnp.full_like(m_ref, NEG_INF)
        l_ref[...] = jnp.zeros_like(l_ref)
        page(_pad_rows(kn_ref[0], PAGE_SIZE).astype(BF16),
             _pad_rows(vn_ref[0], PAGE_SIZE).astype(BF16), bias_ref[1],
             r <= (jnp.right_shift(c, 3) & (n_tok - 1)))
        page(k_refs[0][0].astype(BF16), v_refs[0][0].astype(BF16), bias_ref[0], None)

    @pl.when(j != 0)
    def _():
        page(k_refs[0][0].astype(BF16), v_refs[0][0].astype(BF16), bias_ref[2], None)

    for p in range(1, PAGES_PER_STEP):
        page(k_refs[p][0].astype(BF16), v_refs[p][0].astype(BF16), bias_ref[2], None)

    @pl.when(j == n_pages // PAGES_PER_STEP - 1)
    def _():
        o = acc_ref[...] / _row_to_col(l_ref[...], 2 * half)
        o = o[0:half] - lam_ref[0, 0] * o[half:2 * half]
        rr = lax.broadcasted_iota(I32, o.shape, 0)
        cc = lax.broadcasted_iota(I32, o.shape, 1)
        o = jnp.where(jnp.right_shift(cc, 6) == (rr & (SUBLANES - 1)), o, 0.0)
        ms = jnp.sum(o * o, axis=1, keepdims=True) * (1.0 / DV_DIFF)
        o = (o * lax.rsqrt(ms + EPS)) * g_ref[...] * (1.0 - lam_init)
        o_ref[0] = jnp.sum(o.reshape(n_tok, SUBLANES, D_DIFF), axis=1)


def _diff_decode(q, k_new, v_new, cache_k, cache_v, page_table, lam, dec_bias, g_tiled, lam_init):
    b, t, _ = q.shape
    n_pages = page_table.shape[1]
    tok = pl.BlockSpec((1, t, D_DIFF), lambda bi, j, pt: (bi, 0, 0))
    grid_spec = pltpu.PrefetchScalarGridSpec(
        num_scalar_prefetch=1,
        grid=(b, n_pages // PAGES_PER_STEP),
        in_specs=[
            pl.BlockSpec(memory_space=pltpu.SMEM),
            tok, tok, tok,
            pl.BlockSpec((3, PAGE_SIZE, LANES), lambda bi, j, pt: (0, 0, 0)),
            pl.BlockSpec((1, D_DIFF), lambda bi, j, pt: (0, 0)),
        ] + _page_specs(D_DIFF, n_pages) + _page_specs(D_DIFF, n_pages),
        out_specs=tok,
        scratch_shapes=[
            pltpu.VMEM((LANES, D_QK_DIFF), BF16),
            pltpu.VMEM((2 * t * SUBLANES, D_DIFF), F32),
            pltpu.VMEM((1, LANES), F32),
            pltpu.VMEM((1, LANES), F32),
        ],
    )
    return pl.pallas_call(
        functools.partial(_diff_decode_kernel, n_tok=t, n_pages=n_pages, lam_init=lam_init),
        grid_spec=grid_spec,
        out_shape=jax.ShapeDtypeStruct((b, t, D_DIFF), F32),
        compiler_params=_params(("parallel", "arbitrary")),
        name="diff_decode",
    )(page_table, lam, q, k_new, v_new, dec_bias, g_tiled,
      *([cache_k] * PAGES_PER_STEP), *([cache_v] * PAGES_PER_STEP))


def _finish_kernel(x_ref, osb_ref, odf_ref, omem_ref, gmix_ref, wg_ref, bg_ref, wsb_ref, wdf_ref,
                   wmem_ref, wout_ref, gffn_ref, wr_ref, br_ref, xmid_ref, xn_ref, route_ref):
    x = x_ref[...]
    u = _rmsnorm(x, gmix_ref[...]).astype(BF16)
    gates = jax.nn.sigmoid(_dot(u, wg_ref[...]) + bg_ref[...])
    h = (gates[:, 0:D_MODEL] * _dot(osb_ref[...].astype(BF16), wsb_ref[...])
         + gates[:, D_MODEL:2 * D_MODEL] * _dot(odf_ref[...].astype(BF16), wdf_ref[...])
         + gates[:, 2 * D_MODEL:3 * D_MODEL] * _dot(omem_ref[...].astype(BF16), wmem_ref[...]))
    xm = x + _dot(h.astype(BF16), wout_ref[...])
    xmid_ref[...] = xm
    xn = _rmsnorm(xm, gffn_ref[...])
    xn_ref[...] = xn
    xh, xl = _split_bf16(xn)
    wh, wl = _split_bf16(wr_ref[...])
    lg = _dot(xh, wh) + _dot(xh, wl) + _dot(xl, wh) + br_ref[...]
    lane = lax.broadcasted_iota(I32, lg.shape, 1)
    is_group = lane < N_GROUPS
    gl = jnp.where(is_group, lg, -jnp.inf)
    gmax = jnp.max(gl, axis=1, keepdims=True)
    grp = jnp.min(jnp.where(gl == gmax, lane, LANES), axis=1, keepdims=True)
    p_grp = 1.0 / jnp.sum(jnp.where(is_group, jnp.exp(gl - gmax), 0.0), axis=1, keepdims=True)
    in_group = (lane >= N_GROUPS) & (lane < N_GROUPS + N_EXPERTS) & (
        jnp.right_shift(lane - N_GROUPS, 3) == grp)
    el = jnp.where(in_group, lg, -jnp.inf)
    v1 = jnp.max(el, axis=1, keepdims=True)
    i1 = jnp.min(jnp.where(el == v1, lane, LANES), axis=1, keepdims=True)
    el2 = jnp.where(lane == i1, -jnp.inf, el)
    v2 = jnp.max(el2, axis=1, keepdims=True)
    i2 = jnp.min(jnp.where(el2 == v2, lane, LANES), axis=1, keepdims=True)
    e = jnp.exp(v2 - v1)
    w1 = (1.0 / (1.0 + e)) * p_grp
    w2 = (e / (1.0 + e)) * p_grp
    rec = jnp.where(lane == R_EID1, (i1 - N_GROUPS).astype(F32), 0.0)
    rec = jnp.where(lane == R_EID2, (i2 - N_GROUPS).astype(F32), rec)
    rec = jnp.where(lane == R_W1, w1, rec)
    rec = jnp.where(lane == R_W2, w2, rec)
    route_ref[...] = rec


def _finish(x, o_sb, o_df, o_mem, g_mix, wg, bg, wsb, wdf, wmem, wout, g_ffn, wr, br, name):
    n, d = x.shape
    tm = min(TOK_TILE, n)

    def rows(width):
        return pl.BlockSpec((tm, width), lambda i: (i, 0))

    def whole(a):
        return pl.BlockSpec(a.shape, lambda i: (0, 0))

    args = (x, o_sb, o_df, o_mem, g_mix.reshape(1, d), wg, bg.reshape(1, -1), wsb, wdf, wmem, wout,
            g_ffn.reshape(1, d), wr, br)
    in_specs = [rows(d), rows(D_SB), rows(D_DIFF), rows(D_MEM)] + [whole(a) for a in args[4:]]
    return pl.pallas_call(
        _finish_kernel,
        grid=(n // tm,),
        in_specs=in_specs,
        out_specs=[rows(d), rows(d), rows(LANES)],
        out_shape=[
            jax.ShapeDtypeStruct((n, d), F32),
            jax.ShapeDtypeStruct((n, d), F32),
            jax.ShapeDtypeStruct((n, LANES), F32),
        ],
        compiler_params=_params(("parallel",)),
        name=name,
    )(*args)


def _one_hots(route):
    lane = lax.broadcasted_iota(I32, route.shape, 1)
    oh1 = lane == route[:, R_EID1:R_EID1 + 1].astype(I32)
    oh2 = lane == route[:, R_EID2:R_EID2 + 1].astype(I32)
    return oh1, oh2


def _moe_rank_kernel(route_ref, rank_ref, count_ref, carry_ref):
    i = pl.program_id(0)
    tm = route_ref.shape[0]

    @pl.when(i == 0)
    def _():
        carry_ref[...] = jnp.zeros_like(carry_ref)

    oh1, oh2 = _one_hots(route_ref[...])
    both = jnp.where(oh1 | oh2, 1.0, 0.0)
    r = lax.broadcasted_iota(I32, (tm, tm), 0)
    c = lax.broadcasted_iota(I32, (tm, tm), 1)
    earlier = (c < r).astype(BF16)
    before = _dot(earlier, both.astype(BF16)) + carry_ref[...]
    lane = lax.broadcasted_iota(I32, (tm, LANES), 1)
    rank1 = jnp.sum(jnp.where(oh1, before, 0.0), axis=1, keepdims=True)
    rank2 = jnp.sum(jnp.where(oh2, before, 0.0), axis=1, keepdims=True)
    rank_ref[...] = jnp.where(lane == 0, rank1, jnp.where(lane == 1, rank2, 0.0))
    carry_ref[...] += jnp.sum(both, axis=0, keepdims=True)
    count_ref[...] = jnp.broadcast_to(carry_ref[...], count_ref.shape)


def _moe_rank(route):
    n = route.shape[0]
    tm = min(TOK_TILE, n)
    return pl.pallas_call(
        _moe_rank_kernel,
        grid=(n // tm,),
        in_specs=[pl.BlockSpec((tm, LANES), lambda i: (i, 0))],
        out_specs=[pl.BlockSpec((tm, LANES), lambda i: (i, 0)),
                   pl.BlockSpec((SUBLANES, LANES), lambda i: (0, 0))],
        out_shape=[jax.ShapeDtypeStruct((n, LANES), F32),
                   jax.ShapeDtypeStruct((SUBLANES, LANES), F32)],
        scratch_shapes=[pltpu.VMEM((1, LANES), F32)],
        compiler_params=_params(("arbitrary",)),
        name="moe_rank",
    )(route)


def _moe_dest_kernel(route_ref, rank_ref, count_ref, dest_ref, blk_ref):
    n_blk = blk_ref.shape[0]
    blocks = jnp.right_shift(count_ref[...].astype(I32) + (MOE_ROWS - 1), MOE_ROWS_LOG2).astype(F32)
    r = lax.broadcasted_iota(I32, (LANES, LANES), 0)
    c = lax.broadcasted_iota(I32, (LANES, LANES), 1)
    upto = (r <= c).astype(BF16)
    block_end = _dot(blocks.astype(BF16), upto)
    row_start = (block_end - blocks)[0:1] * float(MOE_ROWS)
    oh1, oh2 = _one_hots(route_ref[...])
    rank = rank_ref[...]
    d1 = jnp.sum(jnp.where(oh1, row_start, 0.0), axis=1, keepdims=True) + rank[:, 0:1]
    d2 = jnp.sum(jnp.where(oh2, row_start, 0.0), axis=1, keepdims=True) + rank[:, 1:2]
    lane = lax.broadcasted_iota(I32, rank.shape, 1)
    dest_ref[...] = jnp.where(lane == 0, d1, jnp.where(lane == 1, d2, 0.0)).astype(I32)

    @pl.when(pl.program_id(0) == 0)
    def _():
        b_idx = lax.broadcasted_iota(I32, (n_blk, LANES), 0).astype(F32)
        lane_b = lax.broadcasted_iota(I32, (n_blk, LANES), 1)
        done = (block_end[0:1] <= b_idx) & (lane_b < N_EXPERTS)
        expert = jnp.minimum(jnp.sum(jnp.where(done, 1.0, 0.0), axis=1, keepdims=True),
                             float(N_EXPERTS - 1))
        used = block_end[0:1, N_EXPERTS - 1:N_EXPERTS]
        blk_ref[...] = jnp.where(lane_b == 0, expert, jnp.where(lane_b == 1, used, 0.0)).astype(I32)


def _moe_dest(route, rank, counts, n_blk):
    n = route.shape[0]
    tm = min(TOK_TILE, n)
    n_blk_pad = -(-n_blk // SUBLANES) * SUBLANES
    return pl.pallas_call(
        _moe_dest_kernel,
        grid=(n // tm,),
        in_specs=[pl.BlockSpec((tm, LANES), lambda i: (i, 0)),
                  pl.BlockSpec((tm, LANES), lambda i: (i, 0)),
                  pl.BlockSpec((SUBLANES, LANES), lambda i: (0, 0))],
        out_specs=[pl.BlockSpec((tm, LANES), lambda i: (i, 0)),
                   pl.BlockSpec((n_blk_pad, LANES), lambda i: (0, 0))],
        out_shape=[jax.ShapeDtypeStruct((n, LANES), I32),
                   jax.ShapeDtypeStruct((n_blk_pad, LANES), I32)],
        compiler_params=_params(("arbitrary",)),
        name="moe_dest",
    )(route, rank, counts)


def _dispatch_kernel(dest_ref, x_ref, init_ref, xs_ref, sem):
    del init_ref
    tm = x_ref.shape[0]

    def row_copy(t, d):
        return pltpu.make_async_copy(x_ref.at[pl.ds(t, 1)], xs_ref.at[pl.ds(d, 1)], sem)

    def start(t, carry):
        row_copy(t, dest_ref[0, 0, 2 * t]).start()
        row_copy(t, dest_ref[0, 0, 2 * t + 1]).start()
        return carry

    def wait(t, carry):
        row_copy(t, dest_ref[0, 0, 2 * t]).wait()
        row_copy(t, dest_ref[0, 0, 2 * t + 1]).wait()
        return carry

    lax.fori_loop(0, tm, start, 0)
    lax.fori_loop(0, tm, wait, 0)


def _dispatch(dest_tiles, xn, n_rows):
    n, d = xn.shape
    tm = dest_tiles.shape[2] // 2
    return pl.pallas_call(
        _dispatch_kernel,
        grid=(n // tm,),
        in_specs=[
            pl.BlockSpec((1, 1, 2 * tm), lambda i: (i, 0, 0), memory_space=pltpu.SMEM),
            pl.BlockSpec((tm, d), lambda i: (i, 0)),
            pl.BlockSpec(memory_space=pl.ANY),
        ],
        out_specs=pl.BlockSpec(memory_space=pl.ANY),
        out_shape=jax.ShapeDtypeStruct((n_rows, d), F32),
        scratch_shapes=[pltpu.SemaphoreType.DMA(())],
        input_output_aliases={2: 0},
        compiler_params=_params(("arbitrary",)),
        name="moe_dispatch",
    )(dest_tiles, xn, jnp.zeros((n_rows, d), F32))


def _expert_kernel(be_ref, used_ref, xs_ref, wg_ref, wu_ref, wd_ref, y_ref):
    b = pl.program_id(0)

    @pl.when(b < used_ref[0])
    def _():
        x = xs_ref[...].astype(BF16)
        h = jax.nn.silu(_dot(x, wg_ref[0])) * _dot(x, wu_ref[0])
        y_ref[...] = _dot(h.astype(BF16), wd_ref[0])

    @pl.when(b >= used_ref[0])
    def _():
        y_ref[...] = jnp.zeros_like(y_ref)


def _experts(blk_exp, used, xs, wg, wu, wd):
    n_rows, d = xs.shape
    n_blk = n_rows // MOE_ROWS
    grid_spec = pltpu.PrefetchScalarGridSpec(
        num_scalar_prefetch=2,
        grid=(n_blk,),
        in_specs=[
            pl.BlockSpec((MOE_ROWS, d), lambda b, be, used: (b, 0)),
            pl.BlockSpec((1, d, D_EXPERT), lambda b, be, used: (be[b], 0, 0)),
            pl.BlockSpec((1, d, D_EXPERT), lambda b, be, used: (be[b], 0, 0)),
            pl.BlockSpec((1, D_EXPERT, d), lambda b, be, used: (be[b], 0, 0)),
        ],
        out_specs=pl.BlockSpec((MOE_ROWS, d), lambda b, be, used: (b, 0)),
    )
    return pl.pallas_call(
        _expert_kernel,
        grid_spec=grid_spec,
        out_shape=jax.ShapeDtypeStruct((n_rows, d), F32),
        compiler_params=_params(("arbitrary",)),
        name="moe_experts",
    )(blk_exp, used, xs, wg, wu, wd)


def _combine_kernel(dest_ref, route_ref, xmid_ref, g_ref, yb_ref, out_ref, buf_ref, sem):
    tm = xmid_ref.shape[0]

    def row_copy(t, k, d):
        return pltpu.make_async_copy(yb_ref.at[pl.ds(d, 1)], buf_ref.at[k, pl.ds(t, 1)], sem)

    def start(t, carry):
        row_copy(t, 0, dest_ref[0, 0, 2 * t]).start()
        row_copy(t, 1, dest_ref[0, 0, 2 * t + 1]).start()
        return carry

    def wait(t, carry):
        row_copy(t, 0, dest_ref[0, 0, 2 * t]).wait()
        row_copy(t, 1, dest_ref[0, 0, 2 * t + 1]).wait()
        return carry

    lax.fori_loop(0, tm, start, 0)
    lax.fori_loop(0, tm, wait, 0)
    route = route_ref[...]
    y = buf_ref[0] * route[:, R_W1:R_W1 + 1] + buf_ref[1] * route[:, R_W2:R_W2 + 1]
    out_ref[...] = _rmsnorm(xmid_ref[...] + y, g_ref[...])


def _combine(dest_tiles, route, xmid, g_final, yb):
    n, d = xmid.shape
    tm = dest_tiles.shape[2] // 2
    return pl.pallas_call(
        _combine_kernel,
        grid=(n // tm,),
        in_specs=[
            pl.BlockSpec((1, 1, 2 * tm), lambda i: (i, 0, 0), memory_space=pltpu.SMEM),
            pl.BlockSpec((tm, LANES), lambda i: (i, 0)),
            pl.BlockSpec((tm, d), lambda i: (i, 0)),
            pl.BlockSpec((1, d), lambda i: (0, 0)),
            pl.BlockSpec(memory_space=pl.ANY),
        ],
        out_specs=pl.BlockSpec((tm, d), lambda i: (i, 0)),
        out_shape=jax.ShapeDtypeStruct((n, d), F32),
        scratch_shapes=[pltpu.VMEM((2, tm, d), F32), pltpu.SemaphoreType.DMA(())],
        compiler_params=_params(("arbitrary",)),
        name="moe_combine",
    )(dest_tiles, route, xmid, g_final.reshape(1, d), yb)


def _moe_and_final_norm(xmid, xn, route, wg, wu, wd, g_final):
    n, d = xmid.shape
    n_blk = -(-(2 * n + N_EXPERTS * (MOE_ROWS - 1)) // MOE_ROWS)
    rank, counts = _moe_rank(route)
    dest, blk = _moe_dest(route, rank, counts, n_blk)
    tm = min(ROW_TILE, n)
    dest_tiles = dest[:, 0:2].reshape(n // tm, 1, 2 * tm)
    xs = _dispatch(dest_tiles, xn, n_blk * MOE_ROWS)
    yb = _experts(blk[0:n_blk, 0], blk[0, 1:2], xs, wg, wu, wd)
    return _combine(dest_tiles, route, xmid, g_final, yb)


def _col_scale():
    s = jnp.ones((1, D_IN), F32)
    return s.at[:, OFF_Q_SB:OFF_Q_SB + D_SB].set(DH_SB ** -0.5)


def kernel(x_prompt, x_sample, cache_sb_k, cache_sb_v, cache_diff_k, cache_diff_v, cache_mem_k, cache_mem_v, page_table, mem_prompt, norm_mix_g, w_in, diff_lam_q1, diff_lam_k1, diff_lam_q2, diff_lam_k2, diff_norm_g, mem_norm_g, w_mem_kv, w_gate, b_gate, w_br_sb, w_br_diff, w_br_mem, w_out, norm_ffn_g, w_router_group, b_router_group, w_router_expert, b_router_expert, w_exp_gate, w_exp_up, w_exp_down, rel_bias, norm_final_g):
    depth = w_in.shape[0]
    assert depth == 1, "single-layer stack only"
    b, t, d = x_prompt.shape
    bs, ts, _ = x_sample.shape
    n_mem = mem_prompt.shape[1]
    n_pool = cache_sb_k.shape[1]
    lam_init = 0.8 - 0.6 * math.exp(-0.3 * 0)

    w_in_b = w_in[0].astype(BF16)
    wg_b, wsb_b, wdf_b = w_gate[0].astype(BF16), w_br_sb[0].astype(BF16), w_br_diff[0].astype(BF16)
    wmem_b, wout_b = w_br_mem[0].astype(BF16), w_out[0].astype(BF16)
    weg_b, weu_b, wed_b = w_exp_gate[0].astype(BF16), w_exp_up[0].astype(BF16), w_exp_down[0].astype(BF16)
    pad = LANES - N_GROUPS - N_EXPERTS
    w_router = jnp.concatenate(
        [w_router_group[0], w_router_expert[0], jnp.zeros((d, pad), F32)], axis=1)
    b_router = jnp.concatenate(
        [b_router_group[0], b_router_expert[0], jnp.zeros((pad,), F32)]).reshape(1, LANES)
    lam_vecs = jnp.concatenate([diff_lam_q1, diff_lam_k1, diff_lam_q2, diff_lam_k2], axis=0)

    tz, dec_bias, lam_tile = _prep(rel_bias, lam_vecs, ATT_BLOCK, lam_init)
    lam = lam_tile[0:1, 0:1]

    k_sb, v_sb, k_df, v_df, pb = _norm_proj(
        x_prompt.reshape(b * t, d), norm_mix_g[0], w_in_b, _col_scale(),
        [(OFF_K_SB, D_SB), (OFF_V_SB, D_SB), (OFF_K_DF, D_QK_DIFF), (OFF_V_DF, D_DIFF)], "proj_prompt")
    pb = pb.reshape(b, t, D_IN)
    mk, mv, mem_b = _norm_proj(
        mem_prompt.reshape(b * n_mem, d), mem_norm_g[0], w_mem_kv[0].astype(BF16),
        jnp.ones((1, 2 * D_MEM), F32), [(0, D_MEM), (D_MEM, D_MEM)], "proj_memory")
    mem_b = mem_b.reshape(b, n_mem, 2 * D_MEM)
    o_sb = _sb_prompt(pb, ATT_BLOCK)
    o_df = _diff_prompt(pb, tz, lam, rel_bias, diff_norm_g[0], ATT_BLOCK, lam_init)
    o_mem = _mem_attn(pb, OFF_Q_MEM // D_MEM, mem_b, 0, mem_b, 1, ATT_BLOCK, "mem_prompt")
    xmid, xn, route = _finish(
        x_prompt.reshape(b * t, d), o_sb.reshape(b * t, D_SB), o_df.reshape(b * t, D_DIFF),
        o_mem.reshape(b * t, D_MEM), norm_mix_g[0], wg_b, b_gate[0], wsb_b, wdf_b, wmem_b, wout_b,
        norm_ffn_g[0], w_router, b_router, "finish_prompt")
    y_prompt = _moe_and_final_norm(xmid, xn, route, weg_b, weu_b, wed_b, norm_final_g)

    q_sb_s, k_sb_s, v_sb_s, q_df_s, k_df_s, v_df_s, q_mem_s, _ = _norm_proj(
        x_sample.reshape(bs * ts, d), norm_mix_g[0], w_in_b, jnp.ones((1, D_IN), F32),
        [(OFF_Q_SB, D_SB), (OFF_K_SB, D_SB), (OFF_V_SB, D_SB), (OFF_Q_DF, D_QK_DIFF),
         (OFF_K_DF, D_QK_DIFF), (OFF_V_DF, D_DIFF), (OFF_Q_MEM, D_MEM)], "proj_sample")
    o_sb_s = _sb_decode(
        q_sb_s.reshape(bs, ts, D_SB), k_sb_s.reshape(bs, ts, D_SB), v_sb_s.reshape(bs, ts, D_SB),
        cache_sb_k[0].reshape(n_pool, PAGE_SIZE, D_SB), cache_sb_v[0].reshape(n_pool, PAGE_SIZE, D_SB),
        page_table)
    o_df_s = _diff_decode(
        q_df_s.reshape(bs, ts, D_QK_DIFF), k_df_s.reshape(bs, ts, D_QK_DIFF),
        v_df_s.reshape(bs, ts, D_DIFF), cache_diff_k[0].reshape(n_pool, PAGE_SIZE, D_QK_DIFF),
        cache_diff_v[0].reshape(n_pool, PAGE_SIZE, D_DIFF), page_table, lam, dec_bias,
        jnp.tile(diff_norm_g[0], H_DIFF).reshape(1, D_DIFF), lam_init)
    o_mem_s = _mem_attn(
        q_mem_s.reshape(bs, ts, D_MEM), 0, cache_mem_k[0].reshape(bs, n_mem, D_MEM), 0,
        cache_mem_v[0].reshape(bs, n_mem, D_MEM), 0, ts, "mem_sample")
    xmid_s, xn_s, route_s = _finish(
        x_sample.reshape(bs * ts, d), o_sb_s.reshape(bs * ts, D_SB), o_df_s.reshape(bs * ts, D_DIFF),
        o_mem_s.reshape(bs * ts, D_MEM), norm_mix_g[0], wg_b, b_gate[0], wsb_b, wdf_b, wmem_b, wout_b,
        norm_ffn_g[0], w_router, b_router, "finish_sample")
    y_sample = _moe_and_final_norm(xmid_s, xn_s, route_s, weg_b, weu_b, wed_b, norm_final_g)

    return (y_prompt.reshape(b, t, d), y_sample.reshape(bs, ts, d),
            k_sb.reshape(1, b, t, H_SB, DH_SB), v_sb.reshape(1, b, t, H_SB, DH_SB),
            k_df.reshape(1, b, t, H_DIFF, 2 * DQ_DIFF), v_df.reshape(1, b, t, H_DIFF, DV_DIFF),
            mk.reshape(1, b, n_mem, H_MEM, DH_MEM), mv.reshape(1, b, n_mem, H_MEM, DH_MEM),
            k_sb_s.reshape(1, bs, ts, H_SB, DH_SB), v_sb_s.reshape(1, bs, ts, H_SB, DH_SB),
            k_df_s.reshape(1, bs, ts, H_DIFF, 2 * DQ_DIFF), v_df_s.reshape(1, bs, ts, H_DIFF, DV_DIFF))
```

```python
import functools
import math

import jax
import jax.numpy as jnp
from jax import lax
from jax.experimental import pallas as pl
from jax.experimental.pallas import tpu as pltpu

F32 = jnp.float32
BF16 = jnp.bfloat16
I32 = jnp.int32

D_MODEL = 1024
PAGE_SIZE = 128
H_SB = 8
DH_SB = 64
H_DIFF = 4
DQ_DIFF = 32
DV_DIFF = 64
H_MEM = 4
DH_MEM = 64
D_SB = H_SB * DH_SB
D_DIFF = H_DIFF * DV_DIFF
D_MEM = H_MEM * DH_MEM
D_QK_DIFF = H_DIFF * 2 * DQ_DIFF
D_IN = 3 * D_SB + 2 * D_QK_DIFF + D_DIFF + D_MEM
N_BUCKETS = 32
MAX_EXACT = 16
MAX_DISTANCE = 128
N_GROUPS = 4
EXPERTS_PER_GROUP = 8
N_EXPERTS = N_GROUPS * EXPERTS_PER_GROUP
D_EXPERT = 512
EPS = 1e-6
NEG_INF = -1e30

LANES = 128
SUBLANES = 8
VMEM_LIMIT = 48 * 1024 * 1024

OFF_Q_SB = 0
OFF_K_SB = D_SB
OFF_V_SB = 2 * D_SB
OFF_Q_DF = 3 * D_SB
OFF_K_DF = OFF_Q_DF + D_QK_DIFF
OFF_V_DF = OFF_K_DF + D_QK_DIFF
OFF_Q_MEM = OFF_V_DF + D_DIFF

MEM_QUERY_BLOCK = 256
SB_QUERY_BLOCK = 1024
SB_KEY_BLOCK = 256
DIFF_BLOCK = 512
MOE_ROWS = 256
MOE_ROWS_LOG2 = 8
TOK_TILE = 256
ROW_TILE = 128
PAGES_PER_STEP = 8

R_EID1, R_EID2, R_W1, R_W2 = 0, 1, 2, 3


def _params(sem, vmem=VMEM_LIMIT):
    return pltpu.CompilerParams(dimension_semantics=sem, vmem_limit_bytes=vmem)


def _rmsnorm(x, g):
    ms = jnp.mean(x * x, axis=-1, keepdims=True)
    return (x * lax.rsqrt(ms + EPS)) * g


def _dot(a, b):
    return jnp.dot(a, b, preferred_element_type=F32)


def _dot_nt(a, b):
    return lax.dot_general(a, b, (((1,), (1,)), ((), ())), preferred_element_type=F32)


def _split_bf16(x):
    hi = x.astype(BF16)
    lo = (x - hi.astype(F32)).astype(BF16)
    return hi, lo


def _split_trunc(x):
    hi = pltpu.bitcast(pltpu.bitcast(x, jnp.uint32) & jnp.uint32(0xFFFF0000), F32)
    return hi.astype(BF16), (x - hi).astype(BF16)


def _softplus(z):
    neg_abs = pltpu.bitcast(pltpu.bitcast(z, jnp.uint32) | jnp.uint32(0x80000000), F32)
    return jnp.maximum(z, 0.0) + jnp.log(1.0 + jnp.exp(neg_abs))


def _col_to_row(v):
    n = v.shape[0]
    r = lax.broadcasted_iota(I32, (n, LANES), 0)
    c = lax.broadcasted_iota(I32, (n, LANES), 1)
    return jnp.sum(jnp.where(r == c, v, 0.0), axis=0, keepdims=True)


def _norm_proj_kernel(x_ref, g_ref, w_ref, s_ref, *out_refs, f32_cols, transposed):
    u = _rmsnorm(x_ref[0], g_ref[...]).astype(BF16)
    p = _dot(u, w_ref[...])
    for ref, (lo, width) in zip(out_refs[:-1], f32_cols):
        ref[0] = p[:, lo:lo + width].T if transposed else p[:, lo:lo + width]
    out_refs[-1][0] = (p * s_ref[...]).astype(BF16)


def _norm_proj(x, g, w_bf16, col_scale, f32_cols, transposed, name):
    b, t, d = x.shape
    n_out = w_bf16.shape[1]
    tm = min(TOK_TILE, t)
    if transposed:
        out_shape = [jax.ShapeDtypeStruct((b, width, t), F32) for _, width in f32_cols]
        out_specs = [pl.BlockSpec((1, width, tm), lambda bi, i: (bi, 0, i)) for _, width in f32_cols]
    else:
        out_shape = [jax.ShapeDtypeStruct((b, t, width), F32) for _, width in f32_cols]
        out_specs = [pl.BlockSpec((1, tm, width), lambda bi, i: (bi, i, 0)) for _, width in f32_cols]
    out_shape.append(jax.ShapeDtypeStruct((b, t, n_out), BF16))
    out_specs.append(pl.BlockSpec((1, tm, n_out), lambda bi, i: (bi, i, 0)))
    return pl.pallas_call(
        functools.partial(_norm_proj_kernel, f32_cols=tuple(f32_cols), transposed=transposed),
        grid=(b, t // tm),
        in_specs=[
            pl.BlockSpec((1, tm, d), lambda bi, i: (bi, i, 0)),
            pl.BlockSpec((1, d), lambda bi, i: (0, 0)),
            pl.BlockSpec((d, n_out), lambda bi, i: (0, 0)),
            pl.BlockSpec((1, n_out), lambda bi, i: (0, 0)),
        ],
        out_specs=out_specs,
        out_shape=out_shape,
        compiler_params=_params(("parallel", "parallel")),
        name=name,
    )(x, g.reshape(1, d), w_bf16, col_scale)


def _t5_bucket(delta):
    n = jnp.maximum(delta, 0)
    nf = jnp.maximum(n, 1).astype(F32)
    large = MAX_EXACT + (jnp.log(nf / MAX_EXACT) / math.log(MAX_DISTANCE / MAX_EXACT)
                         * (N_BUCKETS - MAX_EXACT)).astype(I32)
    large = jnp.minimum(large, N_BUCKETS - 1)
    return jnp.where(n < MAX_EXACT, n, large)


def _bias_of_bucket(bucket, rel_ref, head):
    out = jnp.zeros(bucket.shape, F32)
    for b in range(N_BUCKETS):
        out = jnp.where(bucket == b, rel_ref[b, head], out)
    return out


def _bias_by_head(bucket, head, rel_ref):
    acc = jnp.zeros(bucket.shape, F32)
    for h in range(H_DIFF):
        acc = jnp.where(head == h, _bias_of_bucket(bucket, rel_ref, h), acc)
    return acc


def _prep_kernel(rel_ref, lam_ref, tz_ref, dec_ref, lam_out_ref, *, blk, lam_init):
    r = lax.broadcasted_iota(I32, (blk, blk), 0)
    c = lax.broadcasted_iota(I32, (blk, blk), 1)
    for off in range(2):
        bucket = _t5_bucket(r - c + off * blk)
        for h in range(H_DIFF):
            tz_ref[h, off] = _bias_of_bucket(bucket, rel_ref, h)
    key = lax.broadcasted_iota(I32, (LANES, PAGE_SIZE), 1)
    row = lax.broadcasted_iota(I32, (LANES, PAGE_SIZE), 0)
    qi = jnp.right_shift(row, 3) & 3
    head = row & (SUBLANES - 1)
    dec_ref[0] = _bias_by_head(_t5_bucket(PAGE_SIZE + qi - key), head, rel_ref)
    dec_ref[1] = _bias_by_head(_t5_bucket(qi - key), head, rel_ref)
    dec_ref[2] = _bias_by_head(jnp.full((LANES, PAGE_SIZE), N_BUCKETS - 1, I32), head, rel_ref)
    lq1, lk1, lq2, lk2 = lam_ref[0:1, :], lam_ref[1:2, :], lam_ref[2:3, :], lam_ref[3:4, :]
    lam = (jnp.exp(jnp.sum(lq1 * lk1, axis=-1, keepdims=True))
           - jnp.exp(jnp.sum(lq2 * lk2, axis=-1, keepdims=True)) + lam_init)
    lam_out_ref[...] = jnp.broadcast_to(lam, (SUBLANES, LANES))


def _prep(rel_bias, lam_vecs, blk, lam_init):
    return pl.pallas_call(
        functools.partial(_prep_kernel, blk=blk, lam_init=lam_init),
        in_specs=[
            pl.BlockSpec(memory_space=pltpu.SMEM),
            pl.BlockSpec(memory_space=pltpu.VMEM),
        ],
        out_specs=[pl.BlockSpec(memory_space=pltpu.VMEM)] * 3,
        out_shape=[
            jax.ShapeDtypeStruct((H_DIFF, 2, blk, blk), F32),
            jax.ShapeDtypeStruct((3, PAGE_SIZE, LANES), F32),
            jax.ShapeDtypeStruct((SUBLANES, LANES), F32),
        ],
        name="prep_bias_lambda",
    )(rel_bias, lam_vecs)


def _sb_prompt_kernel(q_ref, k_ref, v_ref, o_ref, acc_ref, c_ref, *, tq, tk):
    i = pl.program_id(2)
    n_diag = tq // tk
    r = lax.broadcasted_iota(I32, (2 * tk, tk), 0)
    c = lax.broadcasted_iota(I32, (2 * tk, tk), 1)
    tri2 = (jnp.where(r >= tk, r - tk, r) >= c).astype(BF16)

    def block(j, row0, diag):
        start = pl.multiple_of(j * tk, tk)
        rows = tq - row0
        if diag:
            rr = lax.broadcasted_iota(I32, (rows, tk), 0)
            cc = lax.broadcasted_iota(I32, (rows, tk), 1)
            strict = cc < rr
        for hh in range(2):
            lo = hh * DH_SB
            q = q_ref[0, row0:tq, lo:lo + DH_SB]
            k = k_ref[0, pl.ds(start, tk), lo:lo + DH_SB]
            v = v_ref[0, pl.ds(start, tk), lo:lo + DH_SB]
            z = _dot_nt(q, k)
            drop = _softplus(z)
            if diag:
                drop = jnp.where(strict, drop, 0.0)
            hi, lo_part = _split_trunc(drop)
            suffix = _dot(jnp.concatenate([hi, lo_part], axis=1), tri2)
            log_w = z - suffix - c_ref[hh, row0:tq]
            if diag:
                log_w = jnp.where(strict, log_w, NEG_INF)
            a = jnp.exp(log_w)
            acc_ref[hh, row0:tq] += _dot(a.astype(BF16), v)
            c_ref[hh, row0:tq] += suffix[:, 0:1]

    acc_ref[...] = jnp.zeros_like(acc_ref)
    c_ref[...] = jnp.zeros_like(c_ref)
    for dd in range(n_diag - 1, -1, -1):
        block(i * n_diag + dd, dd * tk, True)

    def body(kk, carry):
        block(i * n_diag - 1 - kk, 0, False)
        return carry

    lax.fori_loop(0, i * n_diag, body, 0)
    o_ref[0] = jnp.concatenate([acc_ref[0], acc_ref[1]], axis=1).astype(BF16)


def _sb_prompt(pb, tq, tk):
    b, t, _ = pb.shape
    qb, kb, vb = OFF_Q_SB // LANES, OFF_K_SB // LANES, OFF_V_SB // LANES
    return pl.pallas_call(
        functools.partial(_sb_prompt_kernel, tq=tq, tk=tk),
        grid=(b, H_SB // 2, t // tq),
        in_specs=[
            pl.BlockSpec((1, tq, LANES), lambda bi, hp, i: (bi, i, qb + hp)),
            pl.BlockSpec((1, t, LANES), lambda bi, hp, i: (bi, 0, kb + hp)),
            pl.BlockSpec((1, t, LANES), lambda bi, hp, i: (bi, 0, vb + hp)),
        ],
        out_specs=pl.BlockSpec((1, tq, LANES), lambda bi, hp, i: (bi, i, hp)),
        out_shape=jax.ShapeDtypeStruct((b, t, D_SB), BF16),
        scratch_shapes=[pltpu.VMEM((2, tq, DH_SB), F32), pltpu.VMEM((2, tq, 1), F32)],
        compiler_params=_params(("parallel", "parallel", "arbitrary")),
        name="sb_prompt",
    )(pb, pb, pb)


def _diff_prompt_kernel(lam_ref, rel_ref, q_ref, k_ref, v_ref, tz_ref, g_ref, o_ref,
                        q2_ref, m_ref, acc_ref, *, blk, lam_init):
    hp = pl.program_id(1)
    i = pl.program_id(2)
    r = lax.broadcasted_iota(I32, (2 * blk, blk), 0)
    c = lax.broadcasted_iota(I32, (2 * blk, blk), 1)
    causal = c <= jnp.where(r >= blk, r - blk, r)
    lane = lax.broadcasted_iota(I32, (blk, DV_DIFF), 1)
    ones_col = (lane == 0).astype(BF16)
    scale = DQ_DIFF ** -0.5
    for hh in range(2):
        q = q_ref[0, :, hh * DV_DIFF:(hh + 1) * DV_DIFF]
        zero = jnp.zeros_like(q)
        q2_ref[hh, 0:blk, :] = jnp.where(lane < DQ_DIFF, q, zero)
        q2_ref[hh, blk:2 * blk, :] = jnp.where(lane >= DQ_DIFF, q, zero)
    m_ref[...] = jnp.full_like(m_ref, NEG_INF)
    acc_ref[...] = jnp.zeros_like(acc_ref)

    def block(j, kind):
        start = pl.multiple_of(j * blk, blk)
        for hh in range(2):
            lo = hh * DV_DIFF
            k = k_ref[0, pl.ds(start, blk), lo:lo + DV_DIFF]
            v = jnp.concatenate([v_ref[0, pl.ds(start, blk), lo:lo + DV_DIFF], ones_col], axis=1)
            z = _dot_nt(q2_ref[hh], k) * scale
            if kind == 2:
                z = z + rel_ref[N_BUCKETS - 1, hp * 2 + hh]
            else:
                bias = tz_ref[hh, kind]
                z = z + jnp.concatenate([bias, bias], axis=0)
            if kind == 0:
                z = jnp.where(causal, z, NEG_INF)
            m_prev = m_ref[hh]
            m_new = jnp.maximum(m_prev, jnp.max(z, axis=1, keepdims=True))
            alpha = jnp.exp(m_prev - m_new)
            p = jnp.exp(z - m_new)
            acc_ref[hh] = alpha * acc_ref[hh] + _dot(p.astype(BF16), v)
            m_ref[hh] = m_new

    block(i, 0)

    @pl.when(i >= 1)
    def _():
        block(i - 1, 1)

    def body(kk, carry):
        block(i - 2 - kk, 2)
        return carry

    lax.fori_loop(0, jnp.maximum(i - 1, 0), body, 0)

    lam = lam_ref[0, 0]
    outs = []
    for hh in range(2):
        acc = acc_ref[hh]
        o = acc[:, 0:DV_DIFF] / acc[:, DV_DIFF:DV_DIFF + 1]
        o = o[0:blk] - lam * o[blk:2 * blk]
        outs.append(_rmsnorm(o, g_ref[...]) * (1.0 - lam_init))
    o_ref[0] = jnp.concatenate(outs, axis=1).astype(BF16)


def _diff_prompt(pb, tz, lam, rel_bias, g_diff, blk, lam_init):
    b, t, _ = pb.shape
    qb, kb, vb = OFF_Q_DF // LANES, OFF_K_DF // LANES, OFF_V_DF // LANES
    return pl.pallas_call(
        functools.partial(_diff_prompt_kernel, blk=blk, lam_init=lam_init),
        grid=(b, H_DIFF // 2, t // blk),
        in_specs=[
            pl.BlockSpec(memory_space=pltpu.SMEM),
            pl.BlockSpec(memory_space=pltpu.SMEM),
            pl.BlockSpec((1, blk, LANES), lambda bi, hp, i: (bi, i, qb + hp)),
            pl.BlockSpec((1, t, LANES), lambda bi, hp, i: (bi, 0, kb + hp)),
            pl.BlockSpec((1, t, LANES), lambda bi, hp, i: (bi, 0, vb + hp)),
            pl.BlockSpec((2, 2, blk, blk), lambda bi, hp, i: (hp, 0, 0, 0)),
            pl.BlockSpec((1, DV_DIFF), lambda bi, hp, i: (0, 0)),
        ],
        out_specs=pl.BlockSpec((1, blk, LANES), lambda bi, hp, i: (bi, i, hp)),
        out_shape=jax.ShapeDtypeStruct((b, t, D_DIFF), BF16),
        scratch_shapes=[
            pltpu.VMEM((2, 2 * blk, DV_DIFF), BF16),
            pltpu.VMEM((2, 2 * blk, 1), F32),
            pltpu.VMEM((2, 2 * blk, 2 * DV_DIFF), F32),
        ],
        compiler_params=_params(("parallel", "parallel", "arbitrary")),
        name="diff_prompt",
    )(lam, rel_bias, pb, pb, pb, tz, g_diff.reshape(1, DV_DIFF))


def _mem_attn_kernel(q_ref, k_ref, v_ref, o_ref, *, tq):
    rows = max(tq, SUBLANES)
    q_all = q_ref[0].astype(F32) * (DH_MEM ** -0.5)
    if rows > tq:
        q_all = jnp.concatenate([q_all, jnp.zeros((rows - tq, D_MEM), F32)], axis=0)
    q_all = q_all.astype(BF16)
    outs = []
    for h in range(H_MEM):
        lo = h * DH_MEM
        q = q_all[:, lo:lo + DH_MEM]
        k = k_ref[0, :, lo:lo + DH_MEM].astype(BF16)
        v = v_ref[0, :, lo:lo + DH_MEM].astype(BF16)
        z = _dot_nt(q, k)
        p = jnp.exp(z - jnp.max(z, axis=1, keepdims=True))
        outs.append(_dot(p.astype(BF16), v) / jnp.sum(p, axis=1, keepdims=True))
    o_ref[0] = jnp.concatenate(outs, axis=1)[0:tq]


def _mem_attn(q, q_block, k, k_block, v, v_block, tq, name):
    b, t, _ = q.shape
    m = k.shape[1]
    return pl.pallas_call(
        functools.partial(_mem_attn_kernel, tq=tq),
        grid=(b, t // tq),
        in_specs=[
            pl.BlockSpec((1, tq, D_MEM), lambda bi, i: (bi, i, q_block)),
            pl.BlockSpec((1, m, D_MEM), lambda bi, i: (bi, 0, k_block)),
            pl.BlockSpec((1, m, D_MEM), lambda bi, i: (bi, 0, v_block)),
        ],
        out_specs=pl.BlockSpec((1, tq, D_MEM), lambda bi, i: (bi, i, 0)),
        out_shape=jax.ShapeDtypeStruct((b, t, D_MEM), F32),
        compiler_params=_params(("parallel", "parallel")),
        name=name,
    )(q, k, v)


def _pad_rows(x, rows):
    return jnp.concatenate([x, jnp.zeros((rows - x.shape[0], x.shape[1]), x.dtype)], axis=0)


def _query_columns(q, n_heads, width):
    t = q.shape[0]
    rows = jnp.concatenate(
        [jnp.broadcast_to(q[i:i + 1], (SUBLANES, q.shape[1])) for i in range(t)], axis=0)
    r = lax.broadcasted_iota(I32, rows.shape, 0)
    c = lax.broadcasted_iota(I32, rows.shape, 1)
    rows = jnp.where(jnp.right_shift(c, int(math.log2(width))) == (r & (SUBLANES - 1)), rows, 0.0)
    return rows


def _suffix_sum_lanes(x):
    lane = lax.broadcasted_iota(I32, x.shape, 1)
    shift = 1
    while shift < PAGE_SIZE:
        x = x + jnp.where(lane < PAGE_SIZE - shift, pltpu.roll(x, PAGE_SIZE - shift, axis=1), 0.0)
        shift *= 2
    return x


def _head_rows_to_tokens(acc_t, n_tok, width):
    rows = _pad_lanes(acc_t, LANES).T[0:n_tok * SUBLANES]
    rr = lax.broadcasted_iota(I32, rows.shape, 0)
    cc = lax.broadcasted_iota(I32, rows.shape, 1)
    rows = jnp.where(jnp.right_shift(cc, int(math.log2(width))) == (rr & (SUBLANES - 1)), rows, 0.0)
    return rows


def _pad_lanes(x, lanes):
    return jnp.concatenate([x, jnp.zeros((x.shape[0], lanes - x.shape[1]), x.dtype)], axis=1)


def _sb_decode_kernel(pt_ref, q_ref, kn_ref, vn_ref, *refs, n_tok, n_pages):
    kt_refs = refs[:PAGES_PER_STEP]
    vt_refs = refs[PAGES_PER_STEP:2 * PAGES_PER_STEP]
    o_ref, qt_ref, acc_ref, c_ref = refs[2 * PAGES_PER_STEP:]
    j = pl.program_id(1)
    n_rows = n_tok * SUBLANES

    def attend(z_pages, vt, masks):
        drops, suffixes = [], []
        for z, mask in zip(z_pages, masks):
            drop = _softplus(z)
            if mask is not None:
                drop = jnp.where(mask, drop, 0.0)
            suffixes.append(_suffix_sum_lanes(drop))
        carry = c_ref[...]
        weights = []
        for z, suffix, mask in zip(z_pages, suffixes, masks):
            log_w = z - suffix - carry
            if mask is not None:
                log_w = jnp.where(mask, log_w, NEG_INF)
            weights.append(jnp.exp(log_w).astype(BF16))
            carry = carry + suffix[:, 0:1]
        c_ref[...] = carry
        acc_ref[...] += _dot_nt(vt, jnp.concatenate(weights, axis=1))

    @pl.when(j == 0)
    def _():
        q = q_ref[0] * (DH_SB ** -0.5)
        qt_ref[...] = _query_columns(q, H_SB, DH_SB).astype(BF16)
        acc_ref[...] = jnp.zeros_like(acc_ref)
        c_ref[...] = jnp.zeros_like(c_ref)
        kn = _pad_rows(kn_ref[0], PAGE_SIZE).astype(BF16)
        vn_t = _pad_rows(vn_ref[0], PAGE_SIZE).T.astype(BF16)
        row = lax.broadcasted_iota(I32, (n_rows, PAGE_SIZE), 0)
        key = lax.broadcasted_iota(I32, (n_rows, PAGE_SIZE), 1)
        attend([_dot_nt(qt_ref[...], kn)], vn_t, [key < jnp.right_shift(row, 3)])

    kt = jnp.concatenate([ref[0].astype(BF16) for ref in kt_refs], axis=1)
    vt = jnp.concatenate([ref[0].astype(BF16) for ref in vt_refs], axis=1)
    z = _dot(qt_ref[...], kt)
    attend([z[:, p * PAGE_SIZE:(p + 1) * PAGE_SIZE] for p in range(PAGES_PER_STEP)], vt,
           [None] * PAGES_PER_STEP)

    @pl.when(j == n_pages // PAGES_PER_STEP - 1)
    def _():
        rows = _head_rows_to_tokens(acc_ref[...], n_tok, DH_SB)
        o_ref[0] = jnp.sum(rows.reshape(n_tok, SUBLANES, D_SB), axis=1)


def _page_specs(width, n_pages):
    def spec(p):
        return pl.BlockSpec(
            (1, width, PAGE_SIZE),
            lambda bi, j, pt: (pt[bi, n_pages - 1 - (j * PAGES_PER_STEP + p)], 0, 0))
    return [spec(p) for p in range(PAGES_PER_STEP)]


def _sb_decode(q, k_new, v_new, cache_kt, cache_vt, page_table):
    b, t, _ = q.shape
    n_pages = page_table.shape[1]
    tok = pl.BlockSpec((1, t, D_SB), lambda bi, j, pt: (bi, 0, 0))
    grid_spec = pltpu.PrefetchScalarGridSpec(
        num_scalar_prefetch=1,
        grid=(b, n_pages // PAGES_PER_STEP),
        in_specs=[tok, tok, tok] + _page_specs(D_SB, n_pages) + _page_specs(D_SB, n_pages),
        out_specs=tok,
        scratch_shapes=[
            pltpu.VMEM((t * SUBLANES, D_SB), BF16),
            pltpu.VMEM((D_SB, t * SUBLANES), F32),
            pltpu.VMEM((t * SUBLANES, 1), F32),
        ],
    )
    return pl.pallas_call(
        functools.partial(_sb_decode_kernel, n_tok=t, n_pages=n_pages),
        grid_spec=grid_spec,
        out_shape=jax.ShapeDtypeStruct((b, t, D_SB), F32),
        compiler_params=_params(("parallel", "arbitrary")),
        name="sb_decode",
    )(page_table, q, k_new, v_new, *([cache_kt] * PAGES_PER_STEP), *([cache_vt] * PAGES_PER_STEP))


def _diff_decode_kernel(pt_ref, lam_ref, q_ref, kn_ref, vn_ref, bias_ref, g_ref, *refs,
                        n_tok, n_pages, lam_init):
    kt_refs = refs[:PAGES_PER_STEP]
    vt_refs = refs[PAGES_PER_STEP:2 * PAGES_PER_STEP]
    o_ref, qt_ref, acc_ref, m_ref, l_ref = refs[2 * PAGES_PER_STEP:]
    j = pl.program_id(1)
    half = n_tok * SUBLANES
    scale = DQ_DIFF ** -0.5

    def attend(z, vt):
        m_prev = m_ref[...]
        m_new = jnp.maximum(m_prev, jnp.max(z, axis=1, keepdims=True))
        alpha = jnp.exp(m_prev - m_new)
        p = jnp.exp(z - m_new)
        l_ref[...] = alpha * l_ref[...] + jnp.sum(p, axis=1, keepdims=True)
        acc_ref[...] = (acc_ref[...] * _col_to_row(alpha)[:, 0:2 * half]
                        + _dot_nt(vt, p.astype(BF16)))
        m_ref[...] = m_new

    @pl.when(j == 0)
    def _():
        q = _query_columns(q_ref[0], H_DIFF, 2 * DQ_DIFF)
        lane = lax.broadcasted_iota(I32, q.shape, 1)
        first = (jnp.right_shift(lane, 5) & 1) == 0
        q2 = jnp.concatenate([jnp.where(first, q, 0.0), jnp.where(first, 0.0, q)], axis=0)
        qt_ref[...] = q2.astype(BF16)
        acc_ref[...] = jnp.zeros_like(acc_ref)
        m_ref[...] = jnp.full_like(m_ref, NEG_INF)
        l_ref[...] = jnp.zeros_like(l_ref)
        kn = _pad_rows(kn_ref[0], PAGE_SIZE).astype(BF16)
        vn_t = _pad_rows(vn_ref[0], PAGE_SIZE).T.astype(BF16)
        row = lax.broadcasted_iota(I32, (2 * half, PAGE_SIZE), 0)
        key = lax.broadcasted_iota(I32, (2 * half, PAGE_SIZE), 1)
        z_new = _dot_nt(qt_ref[...], kn) * scale + bias_ref[1, 0:2 * half]
        attend(jnp.where(key <= (jnp.right_shift(row, 3) & (n_tok - 1)), z_new, NEG_INF), vn_t)

    kt = jnp.concatenate([ref[0].astype(BF16) for ref in kt_refs], axis=1)
    vt = jnp.concatenate([ref[0].astype(BF16) for ref in vt_refs], axis=1)
    far = bias_ref[2, 0:2 * half]
    near = jnp.where(j == 0, bias_ref[0, 0:2 * half], far)
    bias = jnp.concatenate([near] + [far] * (PAGES_PER_STEP - 1), axis=1)
    attend(_dot(qt_ref[...], kt) * scale + bias, vt)

    @pl.when(j == n_pages // PAGES_PER_STEP - 1)
    def _():
        rows = _pad_lanes(acc_ref[...], LANES).T[0:2 * half] / l_ref[...]
        o = rows[0:half] - lam_ref[0, 0] * rows[half:2 * half]
        rr = lax.broadcasted_iota(I32, o.shape, 0)
        cc = lax.broadcasted_iota(I32, o.shape, 1)
        o = jnp.where(jnp.right_shift(cc, 6) == (rr & (SUBLANES - 1)), o, 0.0)
        ms = jnp.sum(o * o, axis=1, keepdims=True) * (1.0 / DV_DIFF)
        o = (o * lax.rsqrt(ms + EPS)) * g_ref[...] * (1.0 - lam_init)
        o_ref[0] = jnp.sum(o.reshape(n_tok, SUBLANES, D_DIFF), axis=1)


def _diff_decode(q, k_new, v_new, cache_kt, cache_vt, page_table, lam, dec_bias, g_tiled, lam_init):
    b, t, _ = q.shape
    n_pages = page_table.shape[1]
    tok = pl.BlockSpec((1, t, D_DIFF), lambda bi, j, pt: (bi, 0, 0))
    grid_spec = pltpu.PrefetchScalarGridSpec(
        num_scalar_prefetch=1,
        grid=(b, n_pages // PAGES_PER_STEP),
        in_specs=[
            pl.BlockSpec(memory_space=pltpu.SMEM),
            tok, tok, tok,
            pl.BlockSpec((3, LANES, PAGE_SIZE), lambda bi, j, pt: (0, 0, 0)),
            pl.BlockSpec((1, D_DIFF), lambda bi, j, pt: (0, 0)),
        ] + _page_specs(D_QK_DIFF, n_pages) + _page_specs(D_DIFF, n_pages),
        out_specs=tok,
        scratch_shapes=[
            pltpu.VMEM((2 * t * SUBLANES, D_QK_DIFF), BF16),
            pltpu.VMEM((D_DIFF, 2 * t * SUBLANES), F32),
            pltpu.VMEM((2 * t * SUBLANES, 1), F32),
            pltpu.VMEM((2 * t * SUBLANES, 1), F32),
        ],
    )
    return pl.pallas_call(
        functools.partial(_diff_decode_kernel, n_tok=t, n_pages=n_pages, lam_init=lam_init),
        grid_spec=grid_spec,
        out_shape=jax.ShapeDtypeStruct((b, t, D_DIFF), F32),
        compiler_params=_params(("parallel", "arbitrary")),
        name="diff_decode",
    )(page_table, lam, q, k_new, v_new, dec_bias, g_tiled,
      *([cache_kt] * PAGES_PER_STEP), *([cache_vt] * PAGES_PER_STEP))


def _finish_kernel(x_ref, osb_ref, odf_ref, omem_ref, gmix_ref, wg_ref, bg_ref, wsb_ref, wdf_ref,
                   wmem_ref, wout_ref, gffn_ref, wr_ref, br_ref, xmid_ref, xn_ref, route_ref):
    x = x_ref[...]
    u = _rmsnorm(x, gmix_ref[...]).astype(BF16)
    gates = jax.nn.sigmoid(_dot(u, wg_ref[...]) + bg_ref[...])
    h = (gates[:, 0:D_MODEL] * _dot(osb_ref[...].astype(BF16), wsb_ref[...])
         + gates[:, D_MODEL:2 * D_MODEL] * _dot(odf_ref[...].astype(BF16), wdf_ref[...])
         + gates[:, 2 * D_MODEL:3 * D_MODEL] * _dot(omem_ref[...].astype(BF16), wmem_ref[...]))
    xm = x + _dot(h.astype(BF16), wout_ref[...])
    xmid_ref[...] = xm
    xn = _rmsnorm(xm, gffn_ref[...])
    xn_ref[...] = xn
    xh, xl = _split_bf16(xn)
    wh, wl = _split_bf16(wr_ref[...])
    lg = _dot(xh, wh) + _dot(xh, wl) + _dot(xl, wh) + br_ref[...]
    lane = lax.broadcasted_iota(I32, lg.shape, 1)
    is_group = lane < N_GROUPS
    gl = jnp.where(is_group, lg, -jnp.inf)
    gmax = jnp.max(gl, axis=1, keepdims=True)
    grp = jnp.min(jnp.where(gl == gmax, lane, LANES), axis=1, keepdims=True)
    p_grp = 1.0 / jnp.sum(jnp.where(is_group, jnp.exp(gl - gmax), 0.0), axis=1, keepdims=True)
    in_group = (lane >= N_GROUPS) & (lane < N_GROUPS + N_EXPERTS) & (
        jnp.right_shift(lane - N_GROUPS, 3) == grp)
    el = jnp.where(in_group, lg, -jnp.inf)
    v1 = jnp.max(el, axis=1, keepdims=True)
    i1 = jnp.min(jnp.where(el == v1, lane, LANES), axis=1, keepdims=True)
    el2 = jnp.where(lane == i1, -jnp.inf, el)
    v2 = jnp.max(el2, axis=1, keepdims=True)
    i2 = jnp.min(jnp.where(el2 == v2, lane, LANES), axis=1, keepdims=True)
    e = jnp.exp(v2 - v1)
    w1 = (1.0 / (1.0 + e)) * p_grp
    w2 = (e / (1.0 + e)) * p_grp
    rec = jnp.where(lane == R_EID1, (i1 - N_GROUPS).astype(F32), 0.0)
    rec = jnp.where(lane == R_EID2, (i2 - N_GROUPS).astype(F32), rec)
    rec = jnp.where(lane == R_W1, w1, rec)
    rec = jnp.where(lane == R_W2, w2, rec)
    route_ref[...] = rec


def _finish(x, o_sb, o_df, o_mem, g_mix, wg, bg, wsb, wdf, wmem, wout, g_ffn, wr, br, name):
    n, d = x.shape
    tm = min(TOK_TILE, n)

    def rows(width):
        return pl.BlockSpec((tm, width), lambda i: (i, 0))

    def whole(a):
        return pl.BlockSpec(a.shape, lambda i: (0, 0))

    args = (x, o_sb, o_df, o_mem, g_mix.reshape(1, d), wg, bg.reshape(1, -1), wsb, wdf, wmem, wout,
            g_ffn.reshape(1, d), wr, br)
    in_specs = [rows(d), rows(D_SB), rows(D_DIFF), rows(D_MEM)] + [whole(a) for a in args[4:]]
    return pl.pallas_call(
        _finish_kernel,
        grid=(n // tm,),
        in_specs=in_specs,
        out_specs=[rows(d), rows(d), rows(LANES)],
        out_shape=[
            jax.ShapeDtypeStruct((n, d), F32),
            jax.ShapeDtypeStruct((n, d), F32),
            jax.ShapeDtypeStruct((n, LANES), F32),
        ],
        compiler_params=_params(("parallel",)),
        name=name,
    )(*args)


def _one_hots(route):
    lane = lax.broadcasted_iota(I32, route.shape, 1)
    oh1 = lane == route[:, R_EID1:R_EID1 + 1].astype(I32)
    oh2 = lane == route[:, R_EID2:R_EID2 + 1].astype(I32)
    return oh1, oh2


def _moe_rank_kernel(route_ref, rank_ref, count_ref, carry_ref):
    i = pl.program_id(0)
    tm = route_ref.shape[0]

    @pl.when(i == 0)
    def _():
        carry_ref[...] = jnp.zeros_like(carry_ref)

    oh1, oh2 = _one_hots(route_ref[...])
    both = jnp.where(oh1 | oh2, 1.0, 0.0)
    r = lax.broadcasted_iota(I32, (tm, tm), 0)
    c = lax.broadcasted_iota(I32, (tm, tm), 1)
    earlier = (c < r).astype(BF16)
    before = _dot(earlier, both.astype(BF16)) + carry_ref[...]
    lane = lax.broadcasted_iota(I32, (tm, LANES), 1)
    rank1 = jnp.sum(jnp.where(oh1, before, 0.0), axis=1, keepdims=True)
    rank2 = jnp.sum(jnp.where(oh2, before, 0.0), axis=1, keepdims=True)
    rank_ref[...] = jnp.where(lane == 0, rank1, jnp.where(lane == 1, rank2, 0.0))
    carry_ref[...] += jnp.sum(both, axis=0, keepdims=True)
    count_ref[...] = jnp.broadcast_to(carry_ref[...], count_ref.shape)


def _moe_rank(route):
    n = route.shape[0]
    tm = min(TOK_TILE, n)
    return pl.pallas_call(
        _moe_rank_kernel,
        grid=(n // tm,),
        in_specs=[pl.BlockSpec((tm, LANES), lambda i: (i, 0))],
        out_specs=[pl.BlockSpec((tm, LANES), lambda i: (i, 0)),
                   pl.BlockSpec((SUBLANES, LANES), lambda i: (0, 0))],
        out_shape=[jax.ShapeDtypeStruct((n, LANES), F32),
                   jax.ShapeDtypeStruct((SUBLANES, LANES), F32)],
        scratch_shapes=[pltpu.VMEM((1, LANES), F32)],
        compiler_params=_params(("arbitrary",)),
        name="moe_rank",
    )(route)


def _moe_dest_kernel(route_ref, rank_ref, count_ref, dest_ref, blk_ref):
    n_blk = blk_ref.shape[0]
    blocks = jnp.right_shift(count_ref[...].astype(I32) + (MOE_ROWS - 1), MOE_ROWS_LOG2).astype(F32)
    r = lax.broadcasted_iota(I32, (LANES, LANES), 0)
    c = lax.broadcasted_iota(I32, (LANES, LANES), 1)
    upto = (r <= c).astype(BF16)
    block_end = _dot(blocks.astype(BF16), upto)
    row_start = (block_end - blocks)[0:1] * float(MOE_ROWS)
    oh1, oh2 = _one_hots(route_ref[...])
    rank = rank_ref[...]
    d1 = jnp.sum(jnp.where(oh1, row_start, 0.0), axis=1, keepdims=True) + rank[:, 0:1]
    d2 = jnp.sum(jnp.where(oh2, row_start, 0.0), axis=1, keepdims=True) + rank[:, 1:2]
    lane = lax.broadcasted_iota(I32, rank.shape, 1)
    dest_ref[...] = jnp.where(lane == 0, d1, jnp.where(lane == 1, d2, 0.0)).astype(I32)

    @pl.when(pl.program_id(0) == 0)
    def _():
        b_idx = lax.broadcasted_iota(I32, (n_blk, LANES), 0).astype(F32)
        lane_b = lax.broadcasted_iota(I32, (n_blk, LANES), 1)
        done = (block_end[0:1] <= b_idx) & (lane_b < N_EXPERTS)
        expert = jnp.minimum(jnp.sum(jnp.where(done, 1.0, 0.0), axis=1, keepdims=True),
                             float(N_EXPERTS - 1))
        used = block_end[0:1, N_EXPERTS - 1:N_EXPERTS]
        blk_ref[...] = jnp.where(lane_b == 0, expert, jnp.where(lane_b == 1, used, 0.0)).astype(I32)


def _moe_dest(route, rank, counts, n_blk):
    n = route.shape[0]
    tm = min(TOK_TILE, n)
    n_blk_pad = -(-n_blk // SUBLANES) * SUBLANES
    return pl.pallas_call(
        _moe_dest_kernel,
        grid=(n // tm,),
        in_specs=[pl.BlockSpec((tm, LANES), lambda i: (i, 0)),
                  pl.BlockSpec((tm, LANES), lambda i: (i, 0)),
                  pl.BlockSpec((SUBLANES, LANES), lambda i: (0, 0))],
        out_specs=[pl.BlockSpec((tm, LANES), lambda i: (i, 0)),
                   pl.BlockSpec((n_blk_pad, LANES), lambda i: (0, 0))],
        out_shape=[jax.ShapeDtypeStruct((n, LANES), I32),
                   jax.ShapeDtypeStruct((n_blk_pad, LANES), I32)],
        compiler_params=_params(("arbitrary",)),
        name="moe_dest",
    )(route, rank, counts)


def _dispatch_kernel(dest_ref, x_ref, init_ref, xs_ref, sem):
    del init_ref
    tm = x_ref.shape[0]

    def row_copy(t, d):
        return pltpu.make_async_copy(x_ref.at[pl.ds(t, 1)], xs_ref.at[pl.ds(d, 1)], sem)

    def start(t, carry):
        row_copy(t, dest_ref[0, 0, 2 * t]).start()
        row_copy(t, dest_ref[0, 0, 2 * t + 1]).start()
        return carry

    def wait(t, carry):
        row_copy(t, dest_ref[0, 0, 2 * t]).wait()
        row_copy(t, dest_ref[0, 0, 2 * t + 1]).wait()
        return carry

    lax.fori_loop(0, tm, start, 0)
    lax.fori_loop(0, tm, wait, 0)


def _dispatch(dest_tiles, xn, n_rows):
    n, d = xn.shape
    tm = dest_tiles.shape[2] // 2
    return pl.pallas_call(
        _dispatch_kernel,
        grid=(n // tm,),
        in_specs=[
            pl.BlockSpec((1, 1, 2 * tm), lambda i: (i, 0, 0), memory_space=pltpu.SMEM),
            pl.BlockSpec((tm, d), lambda i: (i, 0)),
            pl.BlockSpec(memory_space=pl.ANY),
        ],
        out_specs=pl.BlockSpec(memory_space=pl.ANY),
        out_shape=jax.ShapeDtypeStruct((n_rows, d), F32),
        scratch_shapes=[pltpu.SemaphoreType.DMA(())],
        input_output_aliases={2: 0},
        compiler_params=_params(("arbitrary",)),
        name="moe_dispatch",
    )(dest_tiles, xn, jnp.zeros((n_rows, d), F32))


def _expert_kernel(be_ref, used_ref, xs_ref, wg_ref, wu_ref, wd_ref, y_ref):
    b = pl.program_id(0)

    @pl.when(b < used_ref[0])
    def _():
        x = xs_ref[...].astype(BF16)
        h = jax.nn.silu(_dot(x, wg_ref[0])) * _dot(x, wu_ref[0])
        y_ref[...] = _dot(h.astype(BF16), wd_ref[0])

    @pl.when(b >= used_ref[0])
    def _():
        y_ref[...] = jnp.zeros_like(y_ref)


def _experts(blk_exp, used, xs, wg, wu, wd):
    n_rows, d = xs.shape
    n_blk = n_rows // MOE_ROWS
    grid_spec = pltpu.PrefetchScalarGridSpec(
        num_scalar_prefetch=2,
        grid=(n_blk,),
        in_specs=[
            pl.BlockSpec((MOE_ROWS, d), lambda b, be, used: (b, 0)),
            pl.BlockSpec((1, d, D_EXPERT), lambda b, be, used: (be[b], 0, 0)),
            pl.BlockSpec((1, d, D_EXPERT), lambda b, be, used: (be[b], 0, 0)),
            pl.BlockSpec((1, D_EXPERT, d), lambda b, be, used: (be[b], 0, 0)),
        ],
        out_specs=pl.BlockSpec((MOE_ROWS, d), lambda b, be, used: (b, 0)),
    )
    return pl.pallas_call(
        _expert_kernel,
        grid_spec=grid_spec,
        out_shape=jax.ShapeDtypeStruct((n_rows, d), F32),
        compiler_params=_params(("arbitrary",)),
        name="moe_experts",
    )(blk_exp, used, xs, wg, wu, wd)


def _combine_kernel(dest_ref, route_ref, xmid_ref, g_ref, yb_ref, out_ref, buf_ref, sem):
    tm = xmid_ref.shape[0]

    def row_copy(t, k, d):
        return pltpu.make_async_copy(yb_ref.at[pl.ds(d, 1)], buf_ref.at[k, pl.ds(t, 1)], sem)

    def start(t, carry):
        row_copy(t, 0, dest_ref[0, 0, 2 * t]).start()
        row_copy(t, 1, dest_ref[0, 0, 2 * t + 1]).start()
        return carry

    def wait(t, carry):
        row_copy(t, 0, dest_ref[0, 0, 2 * t]).wait()
        row_copy(t, 1, dest_ref[0, 0, 2 * t + 1]).wait()
        return carry

    lax.fori_loop(0, tm, start, 0)
    lax.fori_loop(0, tm, wait, 0)
    route = route_ref[...]
    y = buf_ref[0] * route[:, R_W1:R_W1 + 1] + buf_ref[1] * route[:, R_W2:R_W2 + 1]
    out_ref[...] = _rmsnorm(xmid_ref[...] + y, g_ref[...])


def _combine(dest_tiles, route, xmid, g_final, yb):
    n, d = xmid.shape
    tm = dest_tiles.shape[2] // 2
    return pl.pallas_call(
        _combine_kernel,
        grid=(n // tm,),
        in_specs=[
            pl.BlockSpec((1, 1, 2 * tm), lambda i: (i, 0, 0), memory_space=pltpu.SMEM),
            pl.BlockSpec((tm, LANES), lambda i: (i, 0)),
            pl.BlockSpec((tm, d), lambda i: (i, 0)),
            pl.BlockSpec((1, d), lambda i: (0, 0)),
            pl.BlockSpec(memory_space=pl.ANY),
        ],
        out_specs=pl.BlockSpec((tm, d), lambda i: (i, 0)),
        out_shape=jax.ShapeDtypeStruct((n, d), F32),
        scratch_shapes=[pltpu.VMEM((2, tm, d), F32), pltpu.SemaphoreType.DMA(())],
        compiler_params=_params(("arbitrary",)),
        name="moe_combine",
    )(dest_tiles, route, xmid, g_final.reshape(1, d), yb)


def _moe_and_final_norm(xmid, xn, route, wg, wu, wd, g_final):
    n, d = xmid.shape
    n_blk = -(-(2 * n + N_EXPERTS * (MOE_ROWS - 1)) // MOE_ROWS)
    rank, counts = _moe_rank(route)
    dest, blk = _moe_dest(route, rank, counts, n_blk)
    tm = min(ROW_TILE, n)
    dest_tiles = dest[:, 0:2].reshape(n // tm, 1, 2 * tm)
    xs = _dispatch(dest_tiles, xn, n_blk * MOE_ROWS)
    yb = _experts(blk[0:n_blk, 0], blk[0, 1:2], xs, wg, wu, wd)
    return _combine(dest_tiles, route, xmid, g_final, yb)


def _col_scale():
    s = jnp.ones((1, D_IN), F32)
    return s.at[:, OFF_Q_SB:OFF_Q_SB + D_SB].set(DH_SB ** -0.5)


def kernel(x_prompt, x_sample, cache_sb_k, cache_sb_v, cache_diff_k, cache_diff_v, cache_mem_k, cache_mem_v, page_table, mem_prompt, norm_mix_g, w_in, diff_lam_q1, diff_lam_k1, diff_lam_q2, diff_lam_k2, diff_norm_g, mem_norm_g, w_mem_kv, w_gate, b_gate, w_br_sb, w_br_diff, w_br_mem, w_out, norm_ffn_g, w_router_group, b_router_group, w_router_expert, b_router_expert, w_exp_gate, w_exp_up, w_exp_down, rel_bias, norm_final_g):
    depth = w_in.shape[0]
    assert depth == 1, "single-layer stack only"
    b, t, d = x_prompt.shape
    bs, ts, _ = x_sample.shape
    n_mem = mem_prompt.shape[1]
    sb_tq, diff_blk = min(SB_QUERY_BLOCK, t), min(DIFF_BLOCK, t)
    lam_init = 0.8 - 0.6 * math.exp(-0.3 * 0)

    w_in_b = w_in[0].astype(BF16)
    wg_b, wsb_b, wdf_b = w_gate[0].astype(BF16), w_br_sb[0].astype(BF16), w_br_diff[0].astype(BF16)
    wmem_b, wout_b = w_br_mem[0].astype(BF16), w_out[0].astype(BF16)
    weg_b, weu_b, wed_b = w_exp_gate[0].astype(BF16), w_exp_up[0].astype(BF16), w_exp_down[0].astype(BF16)
    pad = LANES - N_GROUPS - N_EXPERTS
    w_router = jnp.concatenate(
        [w_router_group[0], w_router_expert[0], jnp.zeros((d, pad), F32)], axis=1)
    b_router = jnp.concatenate(
        [b_router_group[0], b_router_expert[0], jnp.zeros((pad,), F32)]).reshape(1, LANES)
    lam_vecs = jnp.concatenate([diff_lam_q1, diff_lam_k1, diff_lam_q2, diff_lam_k2], axis=0)

    tz, dec_bias, lam_tile = _prep(rel_bias, lam_vecs, diff_blk, lam_init)
    lam = lam_tile[0:1, 0:1]

    def heads_last(a_t, n_heads, width):
        bb, _, tt = a_t.shape
        return jnp.transpose(a_t.reshape(bb, n_heads, width, tt), (0, 3, 1, 2))[None]

    def keys_last(cache, n_heads, width):
        pool, page = cache.shape[1], cache.shape[2]
        return jnp.transpose(cache[0], (0, 2, 3, 1)).reshape(pool, n_heads * width, page)

    k_sb, v_sb, k_df, v_df, pb = _norm_proj(
        x_prompt, norm_mix_g[0], w_in_b, _col_scale(),
        [(OFF_K_SB, D_SB), (OFF_V_SB, D_SB), (OFF_K_DF, D_QK_DIFF), (OFF_V_DF, D_DIFF)], True,
        "proj_prompt")
    mk, mv, mem_b = _norm_proj(
        mem_prompt, mem_norm_g[0], w_mem_kv[0].astype(BF16),
        jnp.ones((1, 2 * D_MEM), F32), [(0, D_MEM), (D_MEM, D_MEM)], True, "proj_memory")
    o_sb = _sb_prompt(pb, sb_tq, SB_KEY_BLOCK)
    o_df = _diff_prompt(pb, tz, lam, rel_bias, diff_norm_g[0], diff_blk, lam_init)
    o_mem = _mem_attn(pb, OFF_Q_MEM // D_MEM, mem_b, 0, mem_b, 1, MEM_QUERY_BLOCK, "mem_prompt")
    xmid, xn, route = _finish(
        x_prompt.reshape(b * t, d), o_sb.reshape(b * t, D_SB), o_df.reshape(b * t, D_DIFF),
        o_mem.reshape(b * t, D_MEM), norm_mix_g[0], wg_b, b_gate[0], wsb_b, wdf_b, wmem_b, wout_b,
        norm_ffn_g[0], w_router, b_router, "finish_prompt")
    y_prompt = _moe_and_final_norm(xmid, xn, route, weg_b, weu_b, wed_b, norm_final_g)

    q_sb_s, k_sb_s, v_sb_s, q_df_s, k_df_s, v_df_s, q_mem_s, _ = _norm_proj(
        x_sample.reshape(1, bs * ts, d), norm_mix_g[0], w_in_b, jnp.ones((1, D_IN), F32),
        [(OFF_Q_SB, D_SB), (OFF_K_SB, D_SB), (OFF_V_SB, D_SB), (OFF_Q_DF, D_QK_DIFF),
         (OFF_K_DF, D_QK_DIFF), (OFF_V_DF, D_DIFF), (OFF_Q_MEM, D_MEM)], False, "proj_sample")
    o_sb_s = _sb_decode(
        q_sb_s.reshape(bs, ts, D_SB), k_sb_s.reshape(bs, ts, D_SB), v_sb_s.reshape(bs, ts, D_SB),
        keys_last(cache_sb_k, H_SB, DH_SB), keys_last(cache_sb_v, H_SB, DH_SB), page_table)
    o_df_s = _diff_decode(
        q_df_s.reshape(bs, ts, D_QK_DIFF), k_df_s.reshape(bs, ts, D_QK_DIFF),
        v_df_s.reshape(bs, ts, D_DIFF), keys_last(cache_diff_k, H_DIFF, 2 * DQ_DIFF),
        keys_last(cache_diff_v, H_DIFF, DV_DIFF), page_table, lam, dec_bias,
        jnp.tile(diff_norm_g[0], H_DIFF).reshape(1, D_DIFF), lam_init)
    o_mem_s = _mem_attn(
        q_mem_s.reshape(bs, ts, D_MEM), 0, cache_mem_k[0].reshape(bs, n_mem, D_MEM), 0,
        cache_mem_v[0].reshape(bs, n_mem, D_MEM), 0, ts, "mem_sample")
    xmid_s, xn_s, route_s = _finish(
        x_sample.reshape(bs * ts, d), o_sb_s.reshape(bs * ts, D_SB), o_df_s.reshape(bs * ts, D_DIFF),
        o_mem_s.reshape(bs * ts, D_MEM), norm_mix_g[0], wg_b, b_gate[0], wsb_b, wdf_b, wmem_b, wout_b,
        norm_ffn_g[0], w_router, b_router, "finish_sample")
    y_sample = _moe_and_final_norm(xmid_s, xn_s, route_s, weg_b, weu_b, wed_b, norm_final_g)

    return (y_prompt.reshape(b, t, d), y_sample.reshape(bs, ts, d),
            heads_last(k_sb, H_SB, DH_SB), heads_last(v_sb, H_SB, DH_SB),
            heads_last(k_df, H_DIFF, 2 * DQ_DIFF), heads_last(v_df, H_DIFF, DV_DIFF),
            heads_last(mk, H_MEM, DH_MEM), heads_last(mv, H_MEM, DH_MEM),
            k_sb_s.reshape(1, bs, ts, H_SB, DH_SB), v_sb_s.reshape(1, bs, ts, H_SB, DH_SB),
            k_df_s.reshape(1, bs, ts, H_DIFF, 2 * DQ_DIFF), v_df_s.reshape(1, bs, ts, H_DIFF, DV_DIFF))
```

```python
import functools
import math

import jax
import jax.numpy as jnp
from jax import lax
from jax.experimental import pallas as pl
from jax.experimental.pallas import tpu as pltpu

F32 = jnp.float32
BF16 = jnp.bfloat16
I32 = jnp.int32

D_MODEL = 1024
PAGE_SIZE = 128
H_SB = 8
DH_SB = 64
H_DIFF = 4
DQ_DIFF = 32
DV_DIFF = 64
H_MEM = 4
DH_MEM = 64
D_SB = H_SB * DH_SB
D_DIFF = H_DIFF * DV_DIFF
D_MEM = H_MEM * DH_MEM
D_QK_DIFF = H_DIFF * 2 * DQ_DIFF
D_IN = 3 * D_SB + 2 * D_QK_DIFF + D_DIFF + D_MEM
N_BUCKETS = 32
MAX_EXACT = 16
MAX_DISTANCE = 128
N_GROUPS = 4
EXPERTS_PER_GROUP = 8
N_EXPERTS = N_GROUPS * EXPERTS_PER_GROUP
D_EXPERT = 512
EPS = 1e-6
NEG_INF = -1e30

LANES = 128
SUBLANES = 8
VMEM_LIMIT = 48 * 1024 * 1024
DIFF_VMEM_LIMIT = 56 * 1024 * 1024

OFF_Q_SB = 0
OFF_K_SB = D_SB
OFF_V_SB = 2 * D_SB
OFF_Q_DF = 3 * D_SB
OFF_K_DF = OFF_Q_DF + D_QK_DIFF
OFF_V_DF = OFF_K_DF + D_QK_DIFF
OFF_Q_MEM = OFF_V_DF + D_DIFF

MEM_QUERY_BLOCK = 256
SB_QUERY_BLOCK = 2048
SB_KEY_BLOCK = 256
DIFF_BLOCK = 512
MOE_ROWS = 256
MOE_ROWS_LOG2 = 8
TOK_TILE = 256
ROW_TILE = 128
ROW_DMA_UNROLL = 8
PAGES_PER_STEP = 16

R_EID1, R_EID2, R_W1, R_W2 = 0, 1, 2, 3


def _params(sem, vmem=VMEM_LIMIT):
    return pltpu.CompilerParams(dimension_semantics=sem, vmem_limit_bytes=vmem)


def _rmsnorm(x, g):
    ms = jnp.mean(x * x, axis=-1, keepdims=True)
    return (x * lax.rsqrt(ms + EPS)) * g


def _dot(a, b):
    return jnp.dot(a, b, preferred_element_type=F32)


def _dot_nt(a, b):
    return lax.dot_general(a, b, (((1,), (1,)), ((), ())), preferred_element_type=F32)


def _split_bf16(x):
    hi = x.astype(BF16)
    lo = (x - hi.astype(F32)).astype(BF16)
    return hi, lo


def _split_trunc(x):
    hi = pltpu.bitcast(pltpu.bitcast(x, jnp.uint32) & jnp.uint32(0xFFFF0000), F32)
    return hi.astype(BF16), (x - hi).astype(BF16)


def _softplus(z):
    neg_abs = pltpu.bitcast(pltpu.bitcast(z, jnp.uint32) | jnp.uint32(0x80000000), F32)
    return jnp.maximum(z, 0.0) + jnp.log(1.0 + jnp.exp(neg_abs))


def _col_to_row(v):
    n = v.shape[0]
    r = lax.broadcasted_iota(I32, (n, LANES), 0)
    c = lax.broadcasted_iota(I32, (n, LANES), 1)
    return jnp.sum(jnp.where(r == c, v, 0.0), axis=0, keepdims=True)


def _norm_proj_kernel(x_ref, g_ref, w_ref, s_ref, *out_refs, f32_cols, transposed):
    u = _rmsnorm(x_ref[0], g_ref[...]).astype(BF16)
    p = _dot(u, w_ref[...])
    for ref, (lo, width) in zip(out_refs[:-1], f32_cols):
        ref[0] = p[:, lo:lo + width].T if transposed else p[:, lo:lo + width]
    out_refs[-1][0] = (p * s_ref[...]).astype(BF16)


def _norm_proj(x, g, w_bf16, col_scale, f32_cols, transposed, name):
    b, t, d = x.shape
    n_out = w_bf16.shape[1]
    tm = min(TOK_TILE, t)
    if transposed:
        out_shape = [jax.ShapeDtypeStruct((b, width, t), F32) for _, width in f32_cols]
        out_specs = [pl.BlockSpec((1, width, tm), lambda bi, i: (bi, 0, i)) for _, width in f32_cols]
    else:
        out_shape = [jax.ShapeDtypeStruct((b, t, width), F32) for _, width in f32_cols]
        out_specs = [pl.BlockSpec((1, tm, width), lambda bi, i: (bi, i, 0)) for _, width in f32_cols]
    out_shape.append(jax.ShapeDtypeStruct((b, t, n_out), BF16))
    out_specs.append(pl.BlockSpec((1, tm, n_out), lambda bi, i: (bi, i, 0)))
    return pl.pallas_call(
        functools.partial(_norm_proj_kernel, f32_cols=tuple(f32_cols), transposed=transposed),
        grid=(b, t // tm),
        in_specs=[
            pl.BlockSpec((1, tm, d), lambda bi, i: (bi, i, 0)),
            pl.BlockSpec((1, d), lambda bi, i: (0, 0)),
            pl.BlockSpec((d, n_out), lambda bi, i: (0, 0)),
            pl.BlockSpec((1, n_out), lambda bi, i: (0, 0)),
        ],
        out_specs=out_specs,
        out_shape=out_shape,
        compiler_params=_params(("parallel", "parallel")),
        name=name,
    )(x, g.reshape(1, d), w_bf16, col_scale)


def _t5_bucket(delta):
    n = jnp.maximum(delta, 0)
    nf = jnp.maximum(n, 1).astype(F32)
    large = MAX_EXACT + (jnp.log(nf / MAX_EXACT) / math.log(MAX_DISTANCE / MAX_EXACT)
                         * (N_BUCKETS - MAX_EXACT)).astype(I32)
    large = jnp.minimum(large, N_BUCKETS - 1)
    return jnp.where(n < MAX_EXACT, n, large)


def _bias_of_bucket(bucket, rel_ref, head):
    out = jnp.zeros(bucket.shape, F32)
    for b in range(N_BUCKETS):
        out = jnp.where(bucket == b, rel_ref[b, head], out)
    return out


def _bias_by_head(bucket, head, rel_ref):
    acc = jnp.zeros(bucket.shape, F32)
    for h in range(H_DIFF):
        acc = jnp.where(head == h, _bias_of_bucket(bucket, rel_ref, h), acc)
    return acc


def _prep_kernel(rel_ref, lam_ref, tz_ref, dec_ref, lam_out_ref, *, blk, lam_init):
    r = lax.broadcasted_iota(I32, (blk, blk), 0)
    c = lax.broadcasted_iota(I32, (blk, blk), 1)
    for off in range(2):
        bucket = _t5_bucket(r - c + off * blk)
        for h in range(H_DIFF):
            tz_ref[h, off] = _bias_of_bucket(bucket, rel_ref, h)
    key = lax.broadcasted_iota(I32, (LANES, PAGE_SIZE), 1)
    row = lax.broadcasted_iota(I32, (LANES, PAGE_SIZE), 0)
    qi = jnp.right_shift(row, 3) & 3
    head = row & (SUBLANES - 1)
    dec_ref[0] = _bias_by_head(_t5_bucket(PAGE_SIZE + qi - key), head, rel_ref)
    dec_ref[1] = _bias_by_head(_t5_bucket(qi - key), head, rel_ref)
    dec_ref[2] = _bias_by_head(jnp.full((LANES, PAGE_SIZE), N_BUCKETS - 1, I32), head, rel_ref)
    lq1, lk1, lq2, lk2 = lam_ref[0:1, :], lam_ref[1:2, :], lam_ref[2:3, :], lam_ref[3:4, :]
    lam = (jnp.exp(jnp.sum(lq1 * lk1, axis=-1, keepdims=True))
           - jnp.exp(jnp.sum(lq2 * lk2, axis=-1, keepdims=True)) + lam_init)
    lam_out_ref[...] = jnp.broadcast_to(lam, (SUBLANES, LANES))


def _prep(rel_bias, lam_vecs, blk, lam_init):
    return pl.pallas_call(
        functools.partial(_prep_kernel, blk=blk, lam_init=lam_init),
        in_specs=[
            pl.BlockSpec(memory_space=pltpu.SMEM),
            pl.BlockSpec(memory_space=pltpu.VMEM),
        ],
        out_specs=[pl.BlockSpec(memory_space=pltpu.VMEM)] * 3,
        out_shape=[
            jax.ShapeDtypeStruct((H_DIFF, 2, blk, blk), F32),
            jax.ShapeDtypeStruct((3, PAGE_SIZE, LANES), F32),
            jax.ShapeDtypeStruct((SUBLANES, LANES), F32),
        ],
        name="prep_bias_lambda",
    )(rel_bias, lam_vecs)


def _sb_prompt_kernel(q_ref, k_ref, v_ref, o_ref, acc_ref, c_ref, *, tq, tk):
    i = pl.program_id(2)
    n_diag = tq // tk
    r = lax.broadcasted_iota(I32, (2 * tk, tk), 0)
    c = lax.broadcasted_iota(I32, (2 * tk, tk), 1)
    tri2 = (jnp.where(r >= tk, r - tk, r) >= c).astype(BF16)

    def block(j, row0, diag):
        start = pl.multiple_of(j * tk, tk)
        rows = tq - row0
        if diag:
            rr = lax.broadcasted_iota(I32, (rows, tk), 0)
            cc = lax.broadcasted_iota(I32, (rows, tk), 1)
            strict = cc < rr
        for hh in range(2):
            lo = hh * DH_SB
            q = q_ref[0, row0:tq, lo:lo + DH_SB]
            k = k_ref[0, pl.ds(start, tk), lo:lo + DH_SB]
            v = v_ref[0, pl.ds(start, tk), lo:lo + DH_SB]
            z = _dot_nt(q, k)
            drop = _softplus(z)
            if diag:
                drop = jnp.where(strict, drop, 0.0)
            hi, lo_part = _split_trunc(drop)
            suffix = _dot(jnp.concatenate([hi, lo_part], axis=1), tri2)
            log_w = z - suffix - c_ref[hh, row0:tq]
            if diag:
                log_w = jnp.where(strict, log_w, NEG_INF)
            a = jnp.exp(log_w)
            acc_ref[hh, row0:tq] += _dot(a.astype(BF16), v)
            c_ref[hh, row0:tq] += suffix[:, 0:1]

    acc_ref[...] = jnp.zeros_like(acc_ref)
    c_ref[...] = jnp.zeros_like(c_ref)
    for dd in range(n_diag - 1, -1, -1):
        block(i * n_diag + dd, dd * tk, True)

    def body(kk, carry):
        block(i * n_diag - 1 - kk, 0, False)
        return carry

    lax.fori_loop(0, i * n_diag, body, 0)
    o_ref[0] = jnp.concatenate([acc_ref[0], acc_ref[1]], axis=1).astype(BF16)


def _sb_prompt(pb, tq, tk):
    b, t, _ = pb.shape
    qb, kb, vb = OFF_Q_SB // LANES, OFF_K_SB // LANES, OFF_V_SB // LANES
    return pl.pallas_call(
        functools.partial(_sb_prompt_kernel, tq=tq, tk=tk),
        grid=(b, H_SB // 2, t // tq),
        in_specs=[
            pl.BlockSpec((1, tq, LANES), lambda bi, hp, i: (bi, i, qb + hp)),
            pl.BlockSpec((1, t, LANES), lambda bi, hp, i: (bi, 0, kb + hp)),
            pl.BlockSpec((1, t, LANES), lambda bi, hp, i: (bi, 0, vb + hp)),
        ],
        out_specs=pl.BlockSpec((1, tq, LANES), lambda bi, hp, i: (bi, i, hp)),
        out_shape=jax.ShapeDtypeStruct((b, t, D_SB), BF16),
        scratch_shapes=[pltpu.VMEM((2, tq, DH_SB), F32), pltpu.VMEM((2, tq, 1), F32)],
        compiler_params=_params(("parallel", "parallel", "arbitrary")),
        name="sb_prompt",
    )(pb, pb, pb)


def _diff_prompt_kernel(lam_ref, rel_ref, q_ref, k_ref, v_ref, tz_ref, g_ref, o_ref,
                        q2_ref, m_ref, acc_ref, z_ref, *, blk, lam_init):
    hp = pl.program_id(1)
    i = pl.program_id(2)
    r = lax.broadcasted_iota(I32, (2 * blk, blk), 0)
    c = lax.broadcasted_iota(I32, (2 * blk, blk), 1)
    causal = c <= jnp.where(r >= blk, r - blk, r)
    lane = lax.broadcasted_iota(I32, (blk, DV_DIFF), 1)
    ones_col = (lane == 0).astype(BF16)
    scale = DQ_DIFF ** -0.5
    for hh in range(2):
        q = q_ref[0, :, hh * DV_DIFF:(hh + 1) * DV_DIFF]
        zero = jnp.zeros_like(q)
        q2_ref[hh, 0:blk, :] = jnp.where(lane < DQ_DIFF, q, zero)
        q2_ref[hh, blk:2 * blk, :] = jnp.where(lane >= DQ_DIFF, q, zero)
    m_ref[...] = jnp.full_like(m_ref, NEG_INF)
    acc_ref[...] = jnp.zeros_like(acc_ref)

    def logits(j, buf):
        start = pl.multiple_of(j * blk, blk)
        for hh in range(2):
            k = k_ref[0, pl.ds(start, blk), hh * DV_DIFF:(hh + 1) * DV_DIFF]
            z_ref[buf, hh] = _dot_nt(q2_ref[hh], k)

    def consume(j, buf, kind):
        start = pl.multiple_of(j * blk, blk)
        for hh in range(2):
            lo = hh * DV_DIFF
            v = jnp.concatenate([v_ref[0, pl.ds(start, blk), lo:lo + DV_DIFF], ones_col], axis=1)
            z = z_ref[buf, hh] * scale
            if kind == 2:
                z = z + rel_ref[N_BUCKETS - 1, hp * 2 + hh]
            else:
                bias = tz_ref[hh, kind]
                z = z + jnp.concatenate([bias, bias], axis=0)
            if kind == 0:
                z = jnp.where(causal, z, NEG_INF)
            m_prev = m_ref[hh]
            m_new = jnp.maximum(m_prev, jnp.max(z, axis=1, keepdims=True))
            alpha = jnp.exp(m_prev - m_new)
            p = jnp.exp(z - m_new)
            acc_ref[hh] = alpha * acc_ref[hh] + _dot(p.astype(BF16), v)
            m_ref[hh] = m_new

    logits(i, 0)
    logits(jnp.maximum(i - 1, 0), 1)
    consume(i, 0, 0)

    @pl.when(i >= 1)
    def _():
        logits(jnp.maximum(i - 2, 0), 0)
        consume(i - 1, 1, 1)

    n_far = jnp.maximum(i - 1, 0)

    def pair(p, carry):
        ja = i - 2 - 2 * p
        logits(ja - 1, 1)
        consume(ja, 0, 2)
        logits(jnp.maximum(ja - 2, 0), 0)
        consume(ja - 1, 1, 2)
        return carry

    lax.fori_loop(0, n_far // 2, pair, 0)

    @pl.when(n_far % 2 == 1)
    def _():
        consume(0, 0, 2)

    lam = lam_ref[0, 0]
    outs = []
    for hh in range(2):
        acc = acc_ref[hh]
        o = acc[:, 0:DV_DIFF] / acc[:, DV_DIFF:DV_DIFF + 1]
        o = o[0:blk] - lam * o[blk:2 * blk]
        outs.append(_rmsnorm(o, g_ref[...]) * (1.0 - lam_init))
    o_ref[0] = jnp.concatenate(outs, axis=1).astype(BF16)


def _diff_prompt(pb, tz, lam, rel_bias, g_diff, blk, lam_init):
    b, t, _ = pb.shape
    qb, kb, vb = OFF_Q_DF // LANES, OFF_K_DF // LANES, OFF_V_DF // LANES
    return pl.pallas_call(
        functools.partial(_diff_prompt_kernel, blk=blk, lam_init=lam_init),
        grid=(b, H_DIFF // 2, t // blk),
        in_specs=[
            pl.BlockSpec(memory_space=pltpu.SMEM),
            pl.BlockSpec(memory_space=pltpu.SMEM),
            pl.BlockSpec((1, blk, LANES), lambda bi, hp, i: (bi, i, qb + hp)),
            pl.BlockSpec((1, t, LANES), lambda bi, hp, i: (bi, 0, kb + hp)),
            pl.BlockSpec((1, t, LANES), lambda bi, hp, i: (bi, 0, vb + hp)),
            pl.BlockSpec((2, 2, blk, blk), lambda bi, hp, i: (hp, 0, 0, 0)),
            pl.BlockSpec((1, DV_DIFF), lambda bi, hp, i: (0, 0)),
        ],
        out_specs=pl.BlockSpec((1, blk, LANES), lambda bi, hp, i: (bi, i, hp)),
        out_shape=jax.ShapeDtypeStruct((b, t, D_DIFF), BF16),
        scratch_shapes=[
            pltpu.VMEM((2, 2 * blk, DV_DIFF), BF16),
            pltpu.VMEM((2, 2 * blk, 1), F32),
            pltpu.VMEM((2, 2 * blk, 2 * DV_DIFF), F32),
            pltpu.VMEM((2, 2, 2 * blk, blk), F32),
        ],
        compiler_params=_params(("parallel", "parallel", "arbitrary"), vmem=DIFF_VMEM_LIMIT),
        name="diff_prompt",
    )(lam, rel_bias, pb, pb, pb, tz, g_diff.reshape(1, DV_DIFF))


def _mem_attn_kernel(q_ref, k_ref, v_ref, o_ref, *, tq):
    rows = max(tq, SUBLANES)
    q_all = q_ref[0].astype(F32) * (DH_MEM ** -0.5)
    if rows > tq:
        q_all = jnp.concatenate([q_all, jnp.zeros((rows - tq, D_MEM), F32)], axis=0)
    q_all = q_all.astype(BF16)
    outs = []
    for h in range(H_MEM):
        lo = h * DH_MEM
        q = q_all[:, lo:lo + DH_MEM]
        k = k_ref[0, :, lo:lo + DH_MEM].astype(BF16)
        v = v_ref[0, :, lo:lo + DH_MEM].astype(BF16)
        z = _dot_nt(q, k)
        p = jnp.exp(z - jnp.max(z, axis=1, keepdims=True))
        outs.append(_dot(p.astype(BF16), v) / jnp.sum(p, axis=1, keepdims=True))
    o_ref[0] = jnp.concatenate(outs, axis=1)[0:tq]


def _mem_attn(q, q_block, k, k_block, v, v_block, tq, name):
    b, t, _ = q.shape
    m = k.shape[1]
    return pl.pallas_call(
        functools.partial(_mem_attn_kernel, tq=tq),
        grid=(b, t // tq),
        in_specs=[
            pl.BlockSpec((1, tq, D_MEM), lambda bi, i: (bi, i, q_block)),
            pl.BlockSpec((1, m, D_MEM), lambda bi, i: (bi, 0, k_block)),
            pl.BlockSpec((1, m, D_MEM), lambda bi, i: (bi, 0, v_block)),
        ],
        out_specs=pl.BlockSpec((1, tq, D_MEM), lambda bi, i: (bi, i, 0)),
        out_shape=jax.ShapeDtypeStruct((b, t, D_MEM), F32),
        compiler_params=_params(("parallel", "parallel")),
        name=name,
    )(q, k, v)


def _pad_rows(x, rows):
    return jnp.concatenate([x, jnp.zeros((rows - x.shape[0], x.shape[1]), x.dtype)], axis=0)


def _query_columns(q, n_heads, width):
    t = q.shape[0]
    rows = jnp.concatenate(
        [jnp.broadcast_to(q[i:i + 1], (SUBLANES, q.shape[1])) for i in range(t)], axis=0)
    r = lax.broadcasted_iota(I32, rows.shape, 0)
    c = lax.broadcasted_iota(I32, rows.shape, 1)
    rows = jnp.where(jnp.right_shift(c, int(math.log2(width))) == (r & (SUBLANES - 1)), rows, 0.0)
    return rows


def _suffix_sum_lanes(x):
    lane = lax.broadcasted_iota(I32, x.shape, 1)
    shift = 1
    while shift < PAGE_SIZE:
        x = x + jnp.where(lane < PAGE_SIZE - shift, pltpu.roll(x, PAGE_SIZE - shift, axis=1), 0.0)
        shift *= 2
    return x


def _head_rows_to_tokens(acc_t, n_tok, width):
    rows = _pad_lanes(acc_t, LANES).T[0:n_tok * SUBLANES]
    rr = lax.broadcasted_iota(I32, rows.shape, 0)
    cc = lax.broadcasted_iota(I32, rows.shape, 1)
    rows = jnp.where(jnp.right_shift(cc, int(math.log2(width))) == (rr & (SUBLANES - 1)), rows, 0.0)
    return rows


def _pad_lanes(x, lanes):
    return jnp.concatenate([x, jnp.zeros((x.shape[0], lanes - x.shape[1]), x.dtype)], axis=1)


def _sb_decode_kernel(pt_ref, q_ref, kn_ref, vn_ref, *refs, n_tok, n_pages):
    kt_refs = refs[:PAGES_PER_STEP]
    vt_refs = refs[PAGES_PER_STEP:2 * PAGES_PER_STEP]
    o_ref, qt_ref, acc_ref, c_ref = refs[2 * PAGES_PER_STEP:]
    j = pl.program_id(1)
    n_rows = n_tok * SUBLANES

    def attend(z_pages, vt, masks):
        drops, suffixes = [], []
        for z, mask in zip(z_pages, masks):
            drop = _softplus(z)
            if mask is not None:
                drop = jnp.where(mask, drop, 0.0)
            suffixes.append(_suffix_sum_lanes(drop))
        carry = c_ref[...]
        weights = []
        for z, suffix, mask in zip(z_pages, suffixes, masks):
            log_w = z - suffix - carry
            if mask is not None:
                log_w = jnp.where(mask, log_w, NEG_INF)
            weights.append(jnp.exp(log_w).astype(BF16))
            carry = carry + suffix[:, 0:1]
        c_ref[...] = carry
        acc_ref[...] += _dot_nt(vt, jnp.concatenate(weights, axis=1))

    @pl.when(j == 0)
    def _():
        q = q_ref[0] * (DH_SB ** -0.5)
        qt_ref[...] = _query_columns(q, H_SB, DH_SB).astype(BF16)
        acc_ref[...] = jnp.zeros_like(acc_ref)
        c_ref[...] = jnp.zeros_like(c_ref)
        kn = _pad_rows(kn_ref[0], PAGE_SIZE).astype(BF16)
        vn_t = _pad_rows(vn_ref[0], PAGE_SIZE).T.astype(BF16)
        row = lax.broadcasted_iota(I32, (n_rows, PAGE_SIZE), 0)
        key = lax.broadcasted_iota(I32, (n_rows, PAGE_SIZE), 1)
        attend([_dot_nt(qt_ref[...], kn)], vn_t, [key < jnp.right_shift(row, 3)])

    kt = jnp.concatenate([ref[0].astype(BF16) for ref in kt_refs], axis=1)
    vt = jnp.concatenate([ref[0].astype(BF16) for ref in vt_refs], axis=1)
    z = _dot(qt_ref[...], kt)
    attend([z[:, p * PAGE_SIZE:(p + 1) * PAGE_SIZE] for p in range(PAGES_PER_STEP)], vt,
           [None] * PAGES_PER_STEP)

    @pl.when(j == n_pages // PAGES_PER_STEP - 1)
    def _():
        rows = _head_rows_to_tokens(acc_ref[...], n_tok, DH_SB)
        o_ref[0] = jnp.sum(rows.reshape(n_tok, SUBLANES, D_SB), axis=1)


def _page_specs(width, n_pages):
    def spec(p):
        return pl.BlockSpec(
            (1, width, PAGE_SIZE),
            lambda bi, j, pt: (pt[bi, n_pages - 1 - (j * PAGES_PER_STEP + p)], 0, 0))
    return [spec(p) for p in range(PAGES_PER_STEP)]


def _sb_decode(q, k_new, v_new, cache_kt, cache_vt, page_table):
    b, t, _ = q.shape
    n_pages = page_table.shape[1]
    tok = pl.BlockSpec((1, t, D_SB), lambda bi, j, pt: (bi, 0, 0))
    grid_spec = pltpu.PrefetchScalarGridSpec(
        num_scalar_prefetch=1,
        grid=(b, n_pages // PAGES_PER_STEP),
        in_specs=[tok, tok, tok] + _page_specs(D_SB, n_pages) + _page_specs(D_SB, n_pages),
        out_specs=tok,
        scratch_shapes=[
            pltpu.VMEM((t * SUBLANES, D_SB), BF16),
            pltpu.VMEM((D_SB, t * SUBLANES), F32),
            pltpu.VMEM((t * SUBLANES, 1), F32),
        ],
    )
    return pl.pallas_call(
        functools.partial(_sb_decode_kernel, n_tok=t, n_pages=n_pages),
        grid_spec=grid_spec,
        out_shape=jax.ShapeDtypeStruct((b, t, D_SB), F32),
        compiler_params=_params(("parallel", "arbitrary")),
        name="sb_decode",
    )(page_table, q, k_new, v_new, *([cache_kt] * PAGES_PER_STEP), *([cache_vt] * PAGES_PER_STEP))


def _diff_decode_kernel(pt_ref, lam_ref, q_ref, kn_ref, vn_ref, bias_ref, g_ref, *refs,
                        n_tok, n_pages, lam_init):
    kt_refs = refs[:PAGES_PER_STEP]
    vt_refs = refs[PAGES_PER_STEP:2 * PAGES_PER_STEP]
    o_ref, qt_ref, acc_ref, m_ref, l_ref = refs[2 * PAGES_PER_STEP:]
    j = pl.program_id(1)
    half = n_tok * SUBLANES
    scale = DQ_DIFF ** -0.5

    def attend(z, vt):
        m_prev = m_ref[...]
        m_new = jnp.maximum(m_prev, jnp.max(z, axis=1, keepdims=True))
        alpha = jnp.exp(m_prev - m_new)
        p = jnp.exp(z - m_new)
        l_ref[...] = alpha * l_ref[...] + jnp.sum(p, axis=1, keepdims=True)
        acc_ref[...] = (acc_ref[...] * _col_to_row(alpha)[:, 0:2 * half]
                        + _dot_nt(vt, p.astype(BF16)))
        m_ref[...] = m_new

    @pl.when(j == 0)
    def _():
        q = _query_columns(q_ref[0], H_DIFF, 2 * DQ_DIFF)
        lane = lax.broadcasted_iota(I32, q.shape, 1)
        first = (jnp.right_shift(lane, 5) & 1) == 0
        q2 = jnp.concatenate([jnp.where(first, q, 0.0), jnp.where(first, 0.0, q)], axis=0)
        qt_ref[...] = q2.astype(BF16)
        acc_ref[...] = jnp.zeros_like(acc_ref)
        m_ref[...] = jnp.full_like(m_ref, NEG_INF)
        l_ref[...] = jnp.zeros_like(l_ref)
        kn = _pad_rows(kn_ref[0], PAGE_SIZE).astype(BF16)
        vn_t = _pad_rows(vn_ref[0], PAGE_SIZE).T.astype(BF16)
        row = lax.broadcasted_iota(I32, (2 * half, PAGE_SIZE), 0)
        key = lax.broadcasted_iota(I32, (2 * half, PAGE_SIZE), 1)
        z_new = _dot_nt(qt_ref[...], kn) * scale + bias_ref[1, 0:2 * half]
        attend(jnp.where(key <= (jnp.right_shift(row, 3) & (n_tok - 1)), z_new, NEG_INF), vn_t)

    kt = jnp.concatenate([ref[0].astype(BF16) for ref in kt_refs], axis=1)
    vt = jnp.concatenate([ref[0].astype(BF16) for ref in vt_refs], axis=1)
    far = bias_ref[2, 0:2 * half]
    near = jnp.where(j == 0, bias_ref[0, 0:2 * half], far)
    bias = jnp.concatenate([near] + [far] * (PAGES_PER_STEP - 1), axis=1)
    attend(_dot(qt_ref[...], kt) * scale + bias, vt)

    @pl.when(j == n_pages // PAGES_PER_STEP - 1)
    def _():
        rows = _pad_lanes(acc_ref[...], LANES).T[0:2 * half] / l_ref[...]
        o = rows[0:half] - lam_ref[0, 0] * rows[half:2 * half]
        rr = lax.broadcasted_iota(I32, o.shape, 0)
        cc = lax.broadcasted_iota(I32, o.shape, 1)
        o = jnp.where(jnp.right_shift(cc, 6) == (rr & (SUBLANES - 1)), o, 0.0)
        ms = jnp.sum(o * o, axis=1, keepdims=True) * (1.0 / DV_DIFF)
        o = (o * lax.rsqrt(ms + EPS)) * g_ref[...] * (1.0 - lam_init)
        o_ref[0] = jnp.sum(o.reshape(n_tok, SUBLANES, D_DIFF), axis=1)


def _diff_decode(q, k_new, v_new, cache_kt, cache_vt, page_table, lam, dec_bias, g_tiled, lam_init):
    b, t, _ = q.shape
    n_pages = page_table.shape[1]
    tok = pl.BlockSpec((1, t, D_DIFF), lambda bi, j, pt: (bi, 0, 0))
    grid_spec = pltpu.PrefetchScalarGridSpec(
        num_scalar_prefetch=1,
        grid=(b, n_pages // PAGES_PER_STEP),
        in_specs=[
            pl.BlockSpec(memory_space=pltpu.SMEM),
            tok, tok, tok,
            pl.BlockSpec((3, LANES, PAGE_SIZE), lambda bi, j, pt: (0, 0, 0)),
            pl.BlockSpec((1, D_DIFF), lambda bi, j, pt: (0, 0)),
        ] + _page_specs(D_QK_DIFF, n_pages) + _page_specs(D_DIFF, n_pages),
        out_specs=tok,
        scratch_shapes=[
            pltpu.VMEM((2 * t * SUBLANES, D_QK_DIFF), BF16),
            pltpu.VMEM((D_DIFF, 2 * t * SUBLANES), F32),
            pltpu.VMEM((2 * t * SUBLANES, 1), F32),
            pltpu.VMEM((2 * t * SUBLANES, 1), F32),
        ],
    )
    return pl.pallas_call(
        functools.partial(_diff_decode_kernel, n_tok=t, n_pages=n_pages, lam_init=lam_init),
        grid_spec=grid_spec,
        out_shape=jax.ShapeDtypeStruct((b, t, D_DIFF), F32),
        compiler_params=_params(("parallel", "arbitrary")),
        name="diff_decode",
    )(page_table, lam, q, k_new, v_new, dec_bias, g_tiled,
      *([cache_kt] * PAGES_PER_STEP), *([cache_vt] * PAGES_PER_STEP))


def _finish_kernel(x_ref, osb_ref, odf_ref, omem_ref, gmix_ref, wg_ref, bg_ref, wsb_ref, wdf_ref,
                   wmem_ref, wout_ref, gffn_ref, wr_ref, br_ref, xmid_ref, xn_ref, route_ref):
    x = x_ref[...]
    u = _rmsnorm(x, gmix_ref[...]).astype(BF16)
    gates = jax.nn.sigmoid(_dot(u, wg_ref[...]) + bg_ref[...])
    h = (gates[:, 0:D_MODEL] * _dot(osb_ref[...].astype(BF16), wsb_ref[...])
         + gates[:, D_MODEL:2 * D_MODEL] * _dot(odf_ref[...].astype(BF16), wdf_ref[...])
         + gates[:, 2 * D_MODEL:3 * D_MODEL] * _dot(omem_ref[...].astype(BF16), wmem_ref[...]))
    xm = x + _dot(h.astype(BF16), wout_ref[...])
    xmid_ref[...] = xm
    xn = _rmsnorm(xm, gffn_ref[...])
    xn_ref[...] = xn
    xh, xl = _split_bf16(xn)
    wh, wl = _split_bf16(wr_ref[...])
    lg = _dot(xh, wh) + _dot(xh, wl) + _dot(xl, wh) + br_ref[...]
    lane = lax.broadcasted_iota(I32, lg.shape, 1)
    is_group = lane < N_GROUPS
    gl = jnp.where(is_group, lg, -jnp.inf)
    gmax = jnp.max(gl, axis=1, keepdims=True)
    grp = jnp.min(jnp.where(gl == gmax, lane, LANES), axis=1, keepdims=True)
    p_grp = 1.0 / jnp.sum(jnp.where(is_group, jnp.exp(gl - gmax), 0.0), axis=1, keepdims=True)
    in_group = (lane >= N_GROUPS) & (lane < N_GROUPS + N_EXPERTS) & (
        jnp.right_shift(lane - N_GROUPS, 3) == grp)
    el = jnp.where(in_group, lg, -jnp.inf)
    v1 = jnp.max(el, axis=1, keepdims=True)
    i1 = jnp.min(jnp.where(el == v1, lane, LANES), axis=1, keepdims=True)
    el2 = jnp.where(lane == i1, -jnp.inf, el)
    v2 = jnp.max(el2, axis=1, keepdims=True)
    i2 = jnp.min(jnp.where(el2 == v2, lane, LANES), axis=1, keepdims=True)
    e = jnp.exp(v2 - v1)
    w1 = (1.0 / (1.0 + e)) * p_grp
    w2 = (e / (1.0 + e)) * p_grp
    rec = jnp.where(lane == R_EID1, (i1 - N_GROUPS).astype(F32), 0.0)
    rec = jnp.where(lane == R_EID2, (i2 - N_GROUPS).astype(F32), rec)
    rec = jnp.where(lane == R_W1, w1, rec)
    rec = jnp.where(lane == R_W2, w2, rec)
    route_ref[...] = rec


def _finish(x, o_sb, o_df, o_mem, g_mix, wg, bg, wsb, wdf, wmem, wout, g_ffn, wr, br, name):
    n, d = x.shape
    tm = min(TOK_TILE, n)

    def rows(width):
        return pl.BlockSpec((tm, width), lambda i: (i, 0))

    def whole(a):
        return pl.BlockSpec(a.shape, lambda i: (0, 0))

    args = (x, o_sb, o_df, o_mem, g_mix.reshape(1, d), wg, bg.reshape(1, -1), wsb, wdf, wmem, wout,
            g_ffn.reshape(1, d), wr, br)
    in_specs = [rows(d), rows(D_SB), rows(D_DIFF), rows(D_MEM)] + [whole(a) for a in args[4:]]
    return pl.pallas_call(
        _finish_kernel,
        grid=(n // tm,),
        in_specs=in_specs,
        out_specs=[rows(d), rows(d), rows(LANES)],
        out_shape=[
            jax.ShapeDtypeStruct((n, d), F32),
            jax.ShapeDtypeStruct((n, d), F32),
            jax.ShapeDtypeStruct((n, LANES), F32),
        ],
        compiler_params=_params(("parallel",)),
        name=name,
    )(*args)


def _one_hots(route):
    lane = lax.broadcasted_iota(I32, route.shape, 1)
    oh1 = lane == route[:, R_EID1:R_EID1 + 1].astype(I32)
    oh2 = lane == route[:, R_EID2:R_EID2 + 1].astype(I32)
    return oh1, oh2


def _moe_rank_kernel(route_ref, rank_ref, count_ref, carry_ref):
    i = pl.program_id(0)
    tm = route_ref.shape[0]

    @pl.when(i == 0)
    def _():
        carry_ref[...] = jnp.zeros_like(carry_ref)

    oh1, oh2 = _one_hots(route_ref[...])
    both = jnp.where(oh1 | oh2, 1.0, 0.0)
    r = lax.broadcasted_iota(I32, (tm, tm), 0)
    c = lax.broadcasted_iota(I32, (tm, tm), 1)
    earlier = (c < r).astype(BF16)
    before = _dot(earlier, both.astype(BF16)) + carry_ref[...]
    lane = lax.broadcasted_iota(I32, (tm, LANES), 1)
    rank1 = jnp.sum(jnp.where(oh1, before, 0.0), axis=1, keepdims=True)
    rank2 = jnp.sum(jnp.where(oh2, before, 0.0), axis=1, keepdims=True)
    rank_ref[...] = jnp.where(lane == 0, rank1, jnp.where(lane == 1, rank2, 0.0))
    carry_ref[...] += jnp.sum(both, axis=0, keepdims=True)
    count_ref[...] = jnp.broadcast_to(carry_ref[...], count_ref.shape)


def _moe_rank(route):
    n = route.shape[0]
    tm = min(TOK_TILE, n)
    return pl.pallas_call(
        _moe_rank_kernel,
        grid=(n // tm,),
        in_specs=[pl.BlockSpec((tm, LANES), lambda i: (i, 0))],
        out_specs=[pl.BlockSpec((tm, LANES), lambda i: (i, 0)),
                   pl.BlockSpec((SUBLANES, LANES), lambda i: (0, 0))],
        out_shape=[jax.ShapeDtypeStruct((n, LANES), F32),
                   jax.ShapeDtypeStruct((SUBLANES, LANES), F32)],
        scratch_shapes=[pltpu.VMEM((1, LANES), F32)],
        compiler_params=_params(("arbitrary",)),
        name="moe_rank",
    )(route)


def _moe_dest_kernel(route_ref, rank_ref, count_ref, dest_ref, blk_ref):
    n_blk = blk_ref.shape[0]
    blocks = jnp.right_shift(count_ref[...].astype(I32) + (MOE_ROWS - 1), MOE_ROWS_LOG2).astype(F32)
    r = lax.broadcasted_iota(I32, (LANES, LANES), 0)
    c = lax.broadcasted_iota(I32, (LANES, LANES), 1)
    upto = (r <= c).astype(BF16)
    block_end = _dot(blocks.astype(BF16), upto)
    row_start = (block_end - blocks)[0:1] * float(MOE_ROWS)
    oh1, oh2 = _one_hots(route_ref[...])
    rank = rank_ref[...]
    d1 = jnp.sum(jnp.where(oh1, row_start, 0.0), axis=1, keepdims=True) + rank[:, 0:1]
    d2 = jnp.sum(jnp.where(oh2, row_start, 0.0), axis=1, keepdims=True) + rank[:, 1:2]
    lane = lax.broadcasted_iota(I32, rank.shape, 1)
    dest_ref[...] = jnp.where(lane == 0, d1, jnp.where(lane == 1, d2, 0.0)).astype(I32)

    @pl.when(pl.program_id(0) == 0)
    def _():
        b_idx = lax.broadcasted_iota(I32, (n_blk, LANES), 0).astype(F32)
        lane_b = lax.broadcasted_iota(I32, (n_blk, LANES), 1)
        done = (block_end[0:1] <= b_idx) & (lane_b < N_EXPERTS)
        expert = jnp.minimum(jnp.sum(jnp.where(done, 1.0, 0.0), axis=1, keepdims=True),
                             float(N_EXPERTS - 1))
        used = block_end[0:1, N_EXPERTS - 1:N_EXPERTS]
        blk_ref[...] = jnp.where(lane_b == 0, expert, jnp.where(lane_b == 1, used, 0.0)).astype(I32)


def _moe_dest(route, rank, counts, n_blk):
    n = route.shape[0]
    tm = min(TOK_TILE, n)
    n_blk_pad = -(-n_blk // SUBLANES) * SUBLANES
    return pl.pallas_call(
        _moe_dest_kernel,
        grid=(n // tm,),
        in_specs=[pl.BlockSpec((tm, LANES), lambda i: (i, 0)),
                  pl.BlockSpec((tm, LANES), lambda i: (i, 0)),
                  pl.BlockSpec((SUBLANES, LANES), lambda i: (0, 0))],
        out_specs=[pl.BlockSpec((tm, LANES), lambda i: (i, 0)),
                   pl.BlockSpec((n_blk_pad, LANES), lambda i: (0, 0))],
        out_shape=[jax.ShapeDtypeStruct((n, LANES), I32),
                   jax.ShapeDtypeStruct((n_blk_pad, LANES), I32)],
        compiler_params=_params(("arbitrary",)),
        name="moe_dest",
    )(route, rank, counts)


def _dispatch_kernel(dest_ref, x_ref, init_ref, xs_ref, sem):
    del init_ref
    tm = x_ref.shape[0]

    def row_copy(t, d):
        return pltpu.make_async_copy(x_ref.at[pl.ds(t, 1)], xs_ref.at[pl.ds(d, 1)], sem)

    def start(t, carry):
        row_copy(t, dest_ref[0, 0, 2 * t]).start(priority=0)
        row_copy(t, dest_ref[0, 0, 2 * t + 1]).start(priority=1)
        return carry

    def wait(t, carry):
        row_copy(t, dest_ref[0, 0, 2 * t]).wait()
        row_copy(t, dest_ref[0, 0, 2 * t + 1]).wait()
        return carry

    lax.fori_loop(0, tm, start, 0, unroll=ROW_DMA_UNROLL)
    lax.fori_loop(0, tm, wait, 0, unroll=ROW_DMA_UNROLL)


def _dispatch(dest_tiles, xn, n_rows):
    n, d = xn.shape
    tm = dest_tiles.shape[2] // 2
    return pl.pallas_call(
        _dispatch_kernel,
        grid=(n // tm,),
        in_specs=[
            pl.BlockSpec((1, 1, 2 * tm), lambda i: (i, 0, 0), memory_space=pltpu.SMEM),
            pl.BlockSpec((tm, d), lambda i: (i, 0)),
            pl.BlockSpec(memory_space=pl.ANY),
        ],
        out_specs=pl.BlockSpec(memory_space=pl.ANY),
        out_shape=jax.ShapeDtypeStruct((n_rows, d), F32),
        scratch_shapes=[pltpu.SemaphoreType.DMA(())],
        input_output_aliases={2: 0},
        compiler_params=_params(("arbitrary",)),
        name="moe_dispatch",
    )(dest_tiles, xn, jnp.zeros((n_rows, d), F32))


def _expert_kernel(be_ref, used_ref, xs_ref, wg_ref, wu_ref, wd_ref, y_ref):
    b = pl.program_id(0)

    @pl.when(b < used_ref[0])
    def _():
        x = xs_ref[...].astype(BF16)
        h = jax.nn.silu(_dot(x, wg_ref[0])) * _dot(x, wu_ref[0])
        y_ref[...] = _dot(h.astype(BF16), wd_ref[0])

    @pl.when(b >= used_ref[0])
    def _():
        y_ref[...] = jnp.zeros_like(y_ref)


def _experts(blk_exp, used, xs, wg, wu, wd):
    n_rows, d = xs.shape
    n_blk = n_rows // MOE_ROWS
    grid_spec = pltpu.PrefetchScalarGridSpec(
        num_scalar_prefetch=2,
        grid=(n_blk,),
        in_specs=[
            pl.BlockSpec((MOE_ROWS, d), lambda b, be, used: (b, 0)),
            pl.BlockSpec((1, d, D_EXPERT), lambda b, be, used: (be[b], 0, 0)),
            pl.BlockSpec((1, d, D_EXPERT), lambda b, be, used: (be[b], 0, 0)),
            pl.BlockSpec((1, D_EXPERT, d), lambda b, be, used: (be[b], 0, 0)),
        ],
        out_specs=pl.BlockSpec((MOE_ROWS, d), lambda b, be, used: (b, 0)),
    )
    return pl.pallas_call(
        _expert_kernel,
        grid_spec=grid_spec,
        out_shape=jax.ShapeDtypeStruct((n_rows, d), F32),
        compiler_params=_params(("arbitrary",)),
        name="moe_experts",
    )(blk_exp, used, xs, wg, wu, wd)


def _combine_kernel(dest_ref, route_ref, xmid_ref, g_ref, yb_ref, out_ref, buf_ref, sem):
    tm = xmid_ref.shape[0]

    def row_copy(t, k, d):
        return pltpu.make_async_copy(yb_ref.at[pl.ds(d, 1)], buf_ref.at[k, pl.ds(t, 1)], sem)

    def start(t, carry):
        row_copy(t, 0, dest_ref[0, 0, 2 * t]).start(priority=0)
        row_copy(t, 1, dest_ref[0, 0, 2 * t + 1]).start(priority=1)
        return carry

    def wait(t, carry):
        row_copy(t, 0, dest_ref[0, 0, 2 * t]).wait()
        row_copy(t, 1, dest_ref[0, 0, 2 * t + 1]).wait()
        return carry

    lax.fori_loop(0, tm, start, 0, unroll=ROW_DMA_UNROLL)
    lax.fori_loop(0, tm, wait, 0, unroll=ROW_DMA_UNROLL)
    route = route_ref[...]
    y = buf_ref[0] * route[:, R_W1:R_W1 + 1] + buf_ref[1] * route[:, R_W2:R_W2 + 1]
    out_ref[...] = _rmsnorm(xmid_ref[...] + y, g_ref[...])


def _combine(dest_tiles, route, xmid, g_final, yb):
    n, d = xmid.shape
    tm = dest_tiles.shape[2] // 2
    return pl.pallas_call(
        _combine_kernel,
        grid=(n // tm,),
        in_specs=[
            pl.BlockSpec((1, 1, 2 * tm), lambda i: (i, 0, 0), memory_space=pltpu.SMEM),
            pl.BlockSpec((tm, LANES), lambda i: (i, 0)),
            pl.BlockSpec((tm, d), lambda i: (i, 0)),
            pl.BlockSpec((1, d), lambda i: (0, 0)),
            pl.BlockSpec(memory_space=pl.ANY),
        ],
        out_specs=pl.BlockSpec((tm, d), lambda i: (i, 0)),
        out_shape=jax.ShapeDtypeStruct((n, d), F32),
        scratch_shapes=[pltpu.VMEM((2, tm, d), F32), pltpu.SemaphoreType.DMA(())],
        compiler_params=_params(("arbitrary",)),
        name="moe_combine",
    )(dest_tiles, route, xmid, g_final.reshape(1, d), yb)


def _moe_and_final_norm(xmid, xn, route, wg, wu, wd, g_final):
    n, d = xmid.shape
    n_blk = -(-(2 * n + N_EXPERTS * (MOE_ROWS - 1)) // MOE_ROWS)
    rank, counts = _moe_rank(route)
    dest, blk = _moe_dest(route, rank, counts, n_blk)
    tm = min(ROW_TILE, n)
    dest_tiles = dest[:, 0:2].reshape(n // tm, 1, 2 * tm)
    xs = _dispatch(dest_tiles, xn, n_blk * MOE_ROWS)
    yb = _experts(blk[0:n_blk, 0], blk[0, 1:2], xs, wg, wu, wd)
    return _combine(dest_tiles, route, xmid, g_final, yb)


def _col_scale():
    s = jnp.ones((1, D_IN), F32)
    return s.at[:, OFF_Q_SB:OFF_Q_SB + D_SB].set(DH_SB ** -0.5)


def kernel(x_prompt, x_sample, cache_sb_k, cache_sb_v, cache_diff_k, cache_diff_v, cache_mem_k, cache_mem_v, page_table, mem_prompt, norm_mix_g, w_in, diff_lam_q1, diff_lam_k1, diff_lam_q2, diff_lam_k2, diff_norm_g, mem_norm_g, w_mem_kv, w_gate, b_gate, w_br_sb, w_br_diff, w_br_mem, w_out, norm_ffn_g, w_router_group, b_router_group, w_router_expert, b_router_expert, w_exp_gate, w_exp_up, w_exp_down, rel_bias, norm_final_g):
    depth = w_in.shape[0]
    assert depth == 1, "single-layer stack only"
    assert page_table.shape[1] % PAGES_PER_STEP == 0
    b, t, d = x_prompt.shape
    bs, ts, _ = x_sample.shape
    n_mem = mem_prompt.shape[1]
    sb_tq, diff_blk = min(SB_QUERY_BLOCK, t), min(DIFF_BLOCK, t)
    lam_init = 0.8 - 0.6 * math.exp(-0.3 * 0)

    w_in_b = w_in[0].astype(BF16)
    wg_b, wsb_b, wdf_b = w_gate[0].astype(BF16), w_br_sb[0].astype(BF16), w_br_diff[0].astype(BF16)
    wmem_b, wout_b = w_br_mem[0].astype(BF16), w_out[0].astype(BF16)
    weg_b, weu_b, wed_b = w_exp_gate[0].astype(BF16), w_exp_up[0].astype(BF16), w_exp_down[0].astype(BF16)
    pad = LANES - N_GROUPS - N_EXPERTS
    w_router = jnp.concatenate(
        [w_router_group[0], w_router_expert[0], jnp.zeros((d, pad), F32)], axis=1)
    b_router = jnp.concatenate(
        [b_router_group[0], b_router_expert[0], jnp.zeros((pad,), F32)]).reshape(1, LANES)
    lam_vecs = jnp.concatenate([diff_lam_q1, diff_lam_k1, diff_lam_q2, diff_lam_k2], axis=0)

    tz, dec_bias, lam_tile = _prep(rel_bias, lam_vecs, diff_blk, lam_init)
    lam = lam_tile[0:1, 0:1]

    def heads_last(a_t, n_heads, width):
        bb, _, tt = a_t.shape
        return jnp.transpose(a_t.reshape(bb, n_heads, width, tt), (0, 3, 1, 2))[None]

    def keys_last(cache, n_heads, width):
        pool, page = cache.shape[1], cache.shape[2]
        return jnp.transpose(cache[0], (0, 2, 3, 1)).reshape(pool, n_heads * width, page)

    k_sb, v_sb, k_df, v_df, pb = _norm_proj(
        x_prompt, norm_mix_g[0], w_in_b, _col_scale(),
        [(OFF_K_SB, D_SB), (OFF_V_SB, D_SB), (OFF_K_DF, D_QK_DIFF), (OFF_V_DF, D_DIFF)], True,
        "proj_prompt")
    mk, mv, mem_b = _norm_proj(
        mem_prompt, mem_norm_g[0], w_mem_kv[0].astype(BF16),
        jnp.ones((1, 2 * D_MEM), F32), [(0, D_MEM), (D_MEM, D_MEM)], True, "proj_memory")
    o_sb = _sb_prompt(pb, sb_tq, SB_KEY_BLOCK)
    o_df = _diff_prompt(pb, tz, lam, rel_bias, diff_norm_g[0], diff_blk, lam_init)
    o_mem = _mem_attn(pb, OFF_Q_MEM // D_MEM, mem_b, 0, mem_b, 1, MEM_QUERY_BLOCK, "mem_prompt")
    xmid, xn, route = _finish(
        x_prompt.reshape(b * t, d), o_sb.reshape(b * t, D_SB), o_df.reshape(b * t, D_DIFF),
        o_mem.reshape(b * t, D_MEM), norm_mix_g[0], wg_b, b_gate[0], wsb_b, wdf_b, wmem_b, wout_b,
        norm_ffn_g[0], w_router, b_router, "finish_prompt")
    y_prompt = _moe_and_final_norm(xmid, xn, route, weg_b, weu_b, wed_b, norm_final_g)

    q_sb_s, k_sb_s, v_sb_s, q_df_s, k_df_s, v_df_s, q_mem_s, _ = _norm_proj(
        x_sample.reshape(1, bs * ts, d), norm_mix_g[0], w_in_b, jnp.ones((1, D_IN), F32),
        [(OFF_Q_SB, D_SB), (OFF_K_SB, D_SB), (OFF_V_SB, D_SB), (OFF_Q_DF, D_QK_DIFF),
         (OFF_K_DF, D_QK_DIFF), (OFF_V_DF, D_DIFF), (OFF_Q_MEM, D_MEM)], False, "proj_sample")
    o_sb_s = _sb_decode(
        q_sb_s.reshape(bs, ts, D_SB), k_sb_s.reshape(bs, ts, D_SB), v_sb_s.reshape(bs, ts, D_SB),
        keys_last(cache_sb_k, H_SB, DH_SB), keys_last(cache_sb_v, H_SB, DH_SB), page_table)
    o_df_s = _diff_decode(
        q_df_s.reshape(bs, ts, D_QK_DIFF), k_df_s.reshape(bs, ts, D_QK_DIFF),
        v_df_s.reshape(bs, ts, D_DIFF), keys_last(cache_diff_k, H_DIFF, 2 * DQ_DIFF),
        keys_last(cache_diff_v, H_DIFF, DV_DIFF), page_table, lam, dec_bias,
        jnp.tile(diff_norm_g[0], H_DIFF).reshape(1, D_DIFF), lam_init)
    o_mem_s = _mem_attn(
        q_mem_s.reshape(bs, ts, D_MEM), 0, cache_mem_k[0].reshape(bs, n_mem, D_MEM), 0,
        cache_mem_v[0].reshape(bs, n_mem, D_MEM), 0, ts, "mem_sample")
    xmid_s, xn_s, route_s = _finish(
        x_sample.reshape(bs * ts, d), o_sb_s.reshape(bs * ts, D_SB), o_df_s.reshape(bs * ts, D_DIFF),
        o_mem_s.reshape(bs * ts, D_MEM), norm_mix_g[0], wg_b, b_gate[0], wsb_b, wdf_b, wmem_b, wout_b,
        norm_ffn_g[0], w_router, b_router, "finish_sample")
    y_sample = _moe_and_final_norm(xmid_s, xn_s, route_s, weg_b, weu_b, wed_b, norm_final_g)

    return (y_prompt.reshape(b, t, d), y_sample.reshape(bs, ts, d),
            heads_last(k_sb, H_SB, DH_SB), heads_last(v_sb, H_SB, DH_SB),
            heads_last(k_df, H_DIFF, 2 * DQ_DIFF), heads_last(v_df, H_DIFF, DV_DIFF),
            heads_last(mk, H_MEM, DH_MEM), heads_last(mv, H_MEM, DH_MEM),
            k_sb_s.reshape(1, bs, ts, H_SB, DH_SB), v_sb_s.reshape(1, bs, ts, H_SB, DH_SB),
            k_df_s.reshape(1, bs, ts, H_DIFF, 2 * DQ_DIFF), v_df_s.reshape(1, bs, ts, H_DIFF, DV_DIFF))
```

```python
import functools
import math

import jax
import jax.numpy as jnp
from jax import lax
from jax.experimental import pallas as pl
from jax.experimental.pallas import tpu as pltpu

F32 = jnp.float32
BF16 = jnp.bfloat16
I32 = jnp.int32

D_MODEL = 1024
PAGE_SIZE = 128
H_SB = 8
DH_SB = 64
H_DIFF = 4
DQ_DIFF = 32
DV_DIFF = 64
H_MEM = 4
DH_MEM = 64
D_SB = H_SB * DH_SB
D_DIFF = H_DIFF * DV_DIFF
D_MEM = H_MEM * DH_MEM
D_QK_DIFF = H_DIFF * 2 * DQ_DIFF
D_IN = 3 * D_SB + 2 * D_QK_DIFF + D_DIFF + D_MEM
N_BUCKETS = 32
MAX_EXACT = 16
MAX_DISTANCE = 128
N_GROUPS = 4
EXPERTS_PER_GROUP = 8
N_EXPERTS = N_GROUPS * EXPERTS_PER_GROUP
D_EXPERT = 512
EPS = 1e-6
NEG_INF = -1e30

LANES = 128
SUBLANES = 8
VMEM_LIMIT = 48 * 1024 * 1024
DIFF_VMEM_LIMIT = 56 * 1024 * 1024

OFF_Q_SB = 0
OFF_K_SB = D_SB
OFF_V_SB = 2 * D_SB
OFF_Q_DF = 3 * D_SB
OFF_K_DF = OFF_Q_DF + D_QK_DIFF
OFF_V_DF = OFF_K_DF + D_QK_DIFF
OFF_Q_MEM = OFF_V_DF + D_DIFF

MEM_QUERY_BLOCK = 256
SB_QUERY_BLOCK = 2048
SB_KEY_BLOCK = 256
DIFF_BLOCK = 512
MOE_ROWS = 256
MOE_ROWS_LOG2 = 8
TOK_TILE = 256
ROW_TILE = 128
ROW_DMA_UNROLL = 32
SB_PAGES_PER_STEP = 16
DIFF_PAGES_PER_STEP = 32

R_EID1, R_EID2, R_W1, R_W2 = 0, 1, 2, 3


def _params(sem, vmem=VMEM_LIMIT):
    return pltpu.CompilerParams(dimension_semantics=sem, vmem_limit_bytes=vmem)


def _rmsnorm(x, g):
    ms = jnp.mean(x * x, axis=-1, keepdims=True)
    return (x * lax.rsqrt(ms + EPS)) * g


def _dot(a, b):
    return jnp.dot(a, b, preferred_element_type=F32)


def _dot_nt(a, b):
    return lax.dot_general(a, b, (((1,), (1,)), ((), ())), preferred_element_type=F32)


def _split_bf16(x):
    hi = x.astype(BF16)
    lo = (x - hi.astype(F32)).astype(BF16)
    return hi, lo


def _softplus(z):
    neg_abs = pltpu.bitcast(pltpu.bitcast(z, jnp.uint32) | jnp.uint32(0x80000000), F32)
    return jnp.maximum(z, 0.0) + jnp.log(1.0 + jnp.exp(neg_abs))


def _col_to_row(v):
    n = v.shape[0]
    r = lax.broadcasted_iota(I32, (n, LANES), 0)
    c = lax.broadcasted_iota(I32, (n, LANES), 1)
    return jnp.sum(jnp.where(r == c, v, 0.0), axis=0, keepdims=True)


def _norm_proj_kernel(x_ref, g_ref, w_ref, s_ref, *out_refs, f32_cols, transposed):
    u = _rmsnorm(x_ref[0], g_ref[...]).astype(BF16)
    p = _dot(u, w_ref[...])
    for ref, (lo, width) in zip(out_refs[:-1], f32_cols):
        ref[0] = p[:, lo:lo + width].T if transposed else p[:, lo:lo + width]
    out_refs[-1][0] = (p * s_ref[...]).astype(BF16)


def _norm_proj(x, g, w_bf16, col_scale, f32_cols, transposed, name):
    b, t, d = x.shape
    n_out = w_bf16.shape[1]
    tm = min(TOK_TILE, t)
    if transposed:
        out_shape = [jax.ShapeDtypeStruct((b, width, t), F32) for _, width in f32_cols]
        out_specs = [pl.BlockSpec((1, width, tm), lambda bi, i: (bi, 0, i)) for _, width in f32_cols]
    else:
        out_shape = [jax.ShapeDtypeStruct((b, t, width), F32) for _, width in f32_cols]
        out_specs = [pl.BlockSpec((1, tm, width), lambda bi, i: (bi, i, 0)) for _, width in f32_cols]
    out_shape.append(jax.ShapeDtypeStruct((b, t, n_out), BF16))
    out_specs.append(pl.BlockSpec((1, tm, n_out), lambda bi, i: (bi, i, 0)))
    return pl.pallas_call(
        functools.partial(_norm_proj_kernel, f32_cols=tuple(f32_cols), transposed=transposed),
        grid=(b, t // tm),
        in_specs=[
            pl.BlockSpec((1, tm, d), lambda bi, i: (bi, i, 0)),
            pl.BlockSpec((1, d), lambda bi, i: (0, 0)),
            pl.BlockSpec((d, n_out), lambda bi, i: (0, 0)),
            pl.BlockSpec((1, n_out), lambda bi, i: (0, 0)),
        ],
        out_specs=out_specs,
        out_shape=out_shape,
        compiler_params=_params(("parallel", "parallel")),
        name=name,
    )(x, g.reshape(1, d), w_bf16, col_scale)


def _t5_bucket(delta):
    n = jnp.maximum(delta, 0)
    nf = jnp.maximum(n, 1).astype(F32)
    large = MAX_EXACT + (jnp.log(nf / MAX_EXACT) / math.log(MAX_DISTANCE / MAX_EXACT)
                         * (N_BUCKETS - MAX_EXACT)).astype(I32)
    large = jnp.minimum(large, N_BUCKETS - 1)
    return jnp.where(n < MAX_EXACT, n, large)


def _bias_of_bucket(bucket, rel_ref, head):
    out = jnp.zeros(bucket.shape, F32)
    for b in range(N_BUCKETS):
        out = jnp.where(bucket == b, rel_ref[b, head], out)
    return out


def _bias_by_head(bucket, head, rel_ref):
    acc = jnp.zeros(bucket.shape, F32)
    for h in range(H_DIFF):
        acc = jnp.where(head == h, _bias_of_bucket(bucket, rel_ref, h), acc)
    return acc


def _prep_kernel(rel_ref, lam_ref, tz_ref, dec_ref, lam_out_ref, *, blk, lam_init):
    r = lax.broadcasted_iota(I32, (blk, blk), 0)
    c = lax.broadcasted_iota(I32, (blk, blk), 1)
    for off in range(2):
        bucket = _t5_bucket(r - c + off * blk)
        for h in range(H_DIFF):
            tz_ref[h, off] = _bias_of_bucket(bucket, rel_ref, h)
    key = lax.broadcasted_iota(I32, (LANES, PAGE_SIZE), 1)
    row = lax.broadcasted_iota(I32, (LANES, PAGE_SIZE), 0)
    qi = jnp.right_shift(row, 3) & 3
    head = row & (SUBLANES - 1)
    dec_ref[0] = _bias_by_head(_t5_bucket(PAGE_SIZE + qi - key), head, rel_ref)
    dec_ref[1] = _bias_by_head(_t5_bucket(qi - key), head, rel_ref)
    dec_ref[2] = _bias_by_head(jnp.full((LANES, PAGE_SIZE), N_BUCKETS - 1, I32), head, rel_ref)
    lq1, lk1, lq2, lk2 = lam_ref[0:1, :], lam_ref[1:2, :], lam_ref[2:3, :], lam_ref[3:4, :]
    lam = (jnp.exp(jnp.sum(lq1 * lk1, axis=-1, keepdims=True))
           - jnp.exp(jnp.sum(lq2 * lk2, axis=-1, keepdims=True)) + lam_init)
    lam_out_ref[...] = jnp.broadcast_to(lam, (SUBLANES, LANES))


def _prep(rel_bias, lam_vecs, blk, lam_init):
    return pl.pallas_call(
        functools.partial(_prep_kernel, blk=blk, lam_init=lam_init),
        in_specs=[
            pl.BlockSpec(memory_space=pltpu.SMEM),
            pl.BlockSpec(memory_space=pltpu.VMEM),
        ],
        out_specs=[pl.BlockSpec(memory_space=pltpu.VMEM)] * 3,
        out_shape=[
            jax.ShapeDtypeStruct((H_DIFF, 2, blk, blk), F32),
            jax.ShapeDtypeStruct((3, PAGE_SIZE, LANES), F32),
            jax.ShapeDtypeStruct((SUBLANES, LANES), F32),
        ],
        name="prep_bias_lambda",
    )(rel_bias, lam_vecs)


def _sb_prompt_kernel(q_ref, k_ref, v_ref, o_ref, acc_ref, c_ref, *, tq, tk):
    i = pl.program_id(2)
    n_diag = tq // tk
    r = lax.broadcasted_iota(I32, (tk, tk), 0)
    c = lax.broadcasted_iota(I32, (tk, tk), 1)
    tri = (r >= c).astype(BF16)

    def block(j, row0, diag):
        start = pl.multiple_of(j * tk, tk)
        rows = tq - row0
        if diag:
            rr = lax.broadcasted_iota(I32, (rows, tk), 0)
            cc = lax.broadcasted_iota(I32, (rows, tk), 1)
            strict = cc < rr
        for hh in range(2):
            lo = hh * DH_SB
            q = q_ref[0, row0:tq, lo:lo + DH_SB]
            k = k_ref[0, pl.ds(start, tk), lo:lo + DH_SB]
            v = v_ref[0, pl.ds(start, tk), lo:lo + DH_SB]
            z = _dot_nt(q, k)
            drop = _softplus(z)
            if diag:
                drop = jnp.where(strict, drop, 0.0)
            suffix = _dot(drop.astype(BF16), tri)
            log_w = z - suffix - c_ref[hh, row0:tq]
            if diag:
                log_w = jnp.where(strict, log_w, NEG_INF)
            a = jnp.exp(log_w)
            acc_ref[hh, row0:tq] += _dot(a.astype(BF16), v)
            c_ref[hh, row0:tq] += suffix[:, 0:1]

    acc_ref[...] = jnp.zeros_like(acc_ref)
    c_ref[...] = jnp.zeros_like(c_ref)
    for dd in range(n_diag - 1, -1, -1):
        block(i * n_diag + dd, dd * tk, True)

    def body(kk, carry):
        block(i * n_diag - 1 - kk, 0, False)
        return carry

    lax.fori_loop(0, i * n_diag, body, 0)
    o_ref[0] = jnp.concatenate([acc_ref[0], acc_ref[1]], axis=1).astype(BF16)


def _sb_prompt(pb, tq, tk):
    b, t, _ = pb.shape
    qb, kb, vb = OFF_Q_SB // LANES, OFF_K_SB // LANES, OFF_V_SB // LANES
    return pl.pallas_call(
        functools.partial(_sb_prompt_kernel, tq=tq, tk=tk),
        grid=(b, H_SB // 2, t // tq),
        in_specs=[
            pl.BlockSpec((1, tq, LANES), lambda bi, hp, i: (bi, i, qb + hp)),
            pl.BlockSpec((1, t, LANES), lambda bi, hp, i: (bi, 0, kb + hp)),
            pl.BlockSpec((1, t, LANES), lambda bi, hp, i: (bi, 0, vb + hp)),
        ],
        out_specs=pl.BlockSpec((1, tq, LANES), lambda bi, hp, i: (bi, i, hp)),
        out_shape=jax.ShapeDtypeStruct((b, t, D_SB), BF16),
        scratch_shapes=[pltpu.VMEM((2, tq, DH_SB), F32), pltpu.VMEM((2, tq, 1), F32)],
        compiler_params=_params(("parallel", "parallel", "arbitrary")),
        name="sb_prompt",
    )(pb, pb, pb)


def _diff_prompt_kernel(lam_ref, rel_ref, q_ref, k_ref, v_ref, tz_ref, g_ref, o_ref,
                        q2_ref, m_ref, acc_ref, z_ref, *, blk, lam_init):
    hp = pl.program_id(1)
    i = pl.program_id(2)
    r = lax.broadcasted_iota(I32, (2 * blk, blk), 0)
    c = lax.broadcasted_iota(I32, (2 * blk, blk), 1)
    causal = c <= jnp.where(r >= blk, r - blk, r)
    lane = lax.broadcasted_iota(I32, (blk, DV_DIFF), 1)
    ones_col = (lane == 0).astype(BF16)
    scale = DQ_DIFF ** -0.5
    for hh in range(2):
        q = q_ref[0, :, hh * DV_DIFF:(hh + 1) * DV_DIFF]
        zero = jnp.zeros_like(q)
        q2_ref[hh, 0:blk, :] = jnp.where(lane < DQ_DIFF, q, zero)
        q2_ref[hh, blk:2 * blk, :] = jnp.where(lane >= DQ_DIFF, q, zero)
    m_ref[...] = jnp.full_like(m_ref, NEG_INF)
    acc_ref[...] = jnp.zeros_like(acc_ref)

    def logits(j, buf):
        start = pl.multiple_of(j * blk, blk)
        for hh in range(2):
            k = k_ref[0, pl.ds(start, blk), hh * DV_DIFF:(hh + 1) * DV_DIFF]
            z_ref[buf, hh] = _dot_nt(q2_ref[hh], k)

    def consume(j, buf, kind):
        start = pl.multiple_of(j * blk, blk)
        for hh in range(2):
            lo = hh * DV_DIFF
            v = jnp.concatenate([v_ref[0, pl.ds(start, blk), lo:lo + DV_DIFF], ones_col], axis=1)
            z = z_ref[buf, hh] * scale
            if kind == 2:
                z = z + rel_ref[N_BUCKETS - 1, hp * 2 + hh]
            else:
                bias = tz_ref[hh, kind]
                z = z + jnp.concatenate([bias, bias], axis=0)
            if kind == 0:
                z = jnp.where(causal, z, NEG_INF)
            m_prev = m_ref[hh]
            m_new = jnp.maximum(m_prev, jnp.max(z, axis=1, keepdims=True))
            alpha = jnp.exp(m_prev - m_new)
            p = jnp.exp(z - m_new)
            acc_ref[hh] = alpha * acc_ref[hh] + _dot(p.astype(BF16), v)
            m_ref[hh] = m_new

    logits(i, 0)
    logits(jnp.maximum(i - 1, 0), 1)
    consume(i, 0, 0)

    @pl.when(i >= 1)
    def _():
        logits(jnp.maximum(i - 2, 0), 0)
        consume(i - 1, 1, 1)

    n_far = jnp.maximum(i - 1, 0)

    def pair(p, carry):
        ja = i - 2 - 2 * p
        logits(ja - 1, 1)
        consume(ja, 0, 2)
        logits(jnp.maximum(ja - 2, 0), 0)
        consume(ja - 1, 1, 2)
        return carry

    lax.fori_loop(0, n_far // 2, pair, 0)

    @pl.when(n_far % 2 == 1)
    def _():
        consume(0, 0, 2)

    lam = lam_ref[0, 0]
    outs = []
    for hh in range(2):
        acc = acc_ref[hh]
        o = acc[:, 0:DV_DIFF] / acc[:, DV_DIFF:DV_DIFF + 1]
        o = o[0:blk] - lam * o[blk:2 * blk]
        outs.append(_rmsnorm(o, g_ref[...]) * (1.0 - lam_init))
    o_ref[0] = jnp.concatenate(outs, axis=1).astype(BF16)


def _diff_prompt(pb, tz, lam, rel_bias, g_diff, blk, lam_init):
    b, t, _ = pb.shape
    qb, kb, vb = OFF_Q_DF // LANES, OFF_K_DF // LANES, OFF_V_DF // LANES
    return pl.pallas_call(
        functools.partial(_diff_prompt_kernel, blk=blk, lam_init=lam_init),
        grid=(b, H_DIFF // 2, t // blk),
        in_specs=[
            pl.BlockSpec(memory_space=pltpu.SMEM),
            pl.BlockSpec(memory_space=pltpu.SMEM),
            pl.BlockSpec((1, blk, LANES), lambda bi, hp, i: (bi, i, qb + hp)),
            pl.BlockSpec((1, t, LANES), lambda bi, hp, i: (bi, 0, kb + hp)),
            pl.BlockSpec((1, t, LANES), lambda bi, hp, i: (bi, 0, vb + hp)),
            pl.BlockSpec((2, 2, blk, blk), lambda bi, hp, i: (hp, 0, 0, 0)),
            pl.BlockSpec((1, DV_DIFF), lambda bi, hp, i: (0, 0)),
        ],
        out_specs=pl.BlockSpec((1, blk, LANES), lambda bi, hp, i: (bi, i, hp)),
        out_shape=jax.ShapeDtypeStruct((b, t, D_DIFF), BF16),
        scratch_shapes=[
            pltpu.VMEM((2, 2 * blk, DV_DIFF), BF16),
            pltpu.VMEM((2, 2 * blk, 1), F32),
            pltpu.VMEM((2, 2 * blk, 2 * DV_DIFF), F32),
            pltpu.VMEM((2, 2, 2 * blk, blk), F32),
        ],
        compiler_params=_params(("parallel", "parallel", "arbitrary"), vmem=DIFF_VMEM_LIMIT),
        name="diff_prompt",
    )(lam, rel_bias, pb, pb, pb, tz, g_diff.reshape(1, DV_DIFF))


def _mem_attn_kernel(q_ref, k_ref, v_ref, o_ref, *, tq):
    rows = max(tq, SUBLANES)
    q_all = q_ref[0].astype(F32) * (DH_MEM ** -0.5)
    if rows > tq:
        q_all = jnp.concatenate([q_all, jnp.zeros((rows - tq, D_MEM), F32)], axis=0)
    q_all = q_all.astype(BF16)
    outs = []
    for h in range(H_MEM):
        lo = h * DH_MEM
        q = q_all[:, lo:lo + DH_MEM]
        k = k_ref[0, :, lo:lo + DH_MEM].astype(BF16)
        v = v_ref[0, :, lo:lo + DH_MEM].astype(BF16)
        z = _dot_nt(q, k)
        p = jnp.exp(z - jnp.max(z, axis=1, keepdims=True))
        outs.append(_dot(p.astype(BF16), v) / jnp.sum(p, axis=1, keepdims=True))
    o_ref[0] = jnp.concatenate(outs, axis=1)[0:tq]


def _mem_attn(q, q_block, k, k_block, v, v_block, tq, name):
    b, t, _ = q.shape
    m = k.shape[1]
    return pl.pallas_call(
        functools.partial(_mem_attn_kernel, tq=tq),
        grid=(b, t // tq),
        in_specs=[
            pl.BlockSpec((1, tq, D_MEM), lambda bi, i: (bi, i, q_block)),
            pl.BlockSpec((1, m, D_MEM), lambda bi, i: (bi, 0, k_block)),
            pl.BlockSpec((1, m, D_MEM), lambda bi, i: (bi, 0, v_block)),
        ],
        out_specs=pl.BlockSpec((1, tq, D_MEM), lambda bi, i: (bi, i, 0)),
        out_shape=jax.ShapeDtypeStruct((b, t, D_MEM), F32),
        compiler_params=_params(("parallel", "parallel")),
        name=name,
    )(q, k, v)


def _pad_rows(x, rows):
    return jnp.concatenate([x, jnp.zeros((rows - x.shape[0], x.shape[1]), x.dtype)], axis=0)


def _query_columns(q, n_heads, width):
    t = q.shape[0]
    rows = jnp.concatenate(
        [jnp.broadcast_to(q[i:i + 1], (SUBLANES, q.shape[1])) for i in range(t)], axis=0)
    r = lax.broadcasted_iota(I32, rows.shape, 0)
    c = lax.broadcasted_iota(I32, rows.shape, 1)
    rows = jnp.where(jnp.right_shift(c, int(math.log2(width))) == (r & (SUBLANES - 1)), rows, 0.0)
    return rows


def _suffix_sum_lanes(x):
    lane = lax.broadcasted_iota(I32, x.shape, 1)
    shift = 1
    while shift < PAGE_SIZE:
        x = x + jnp.where(lane < PAGE_SIZE - shift, pltpu.roll(x, PAGE_SIZE - shift, axis=1), 0.0)
        shift *= 2
    return x


def _head_rows_to_tokens(acc_t, n_tok, width):
    rows = _pad_lanes(acc_t, LANES).T[0:n_tok * SUBLANES]
    rr = lax.broadcasted_iota(I32, rows.shape, 0)
    cc = lax.broadcasted_iota(I32, rows.shape, 1)
    rows = jnp.where(jnp.right_shift(cc, int(math.log2(width))) == (rr & (SUBLANES - 1)), rows, 0.0)
    return rows


def _pad_lanes(x, lanes):
    return jnp.concatenate([x, jnp.zeros((x.shape[0], lanes - x.shape[1]), x.dtype)], axis=1)


def _sb_decode_kernel(pt_ref, q_ref, kn_ref, vn_ref, *refs, n_tok, n_pages, pps):
    kt_refs = refs[:pps]
    vt_refs = refs[pps:2 * pps]
    o_ref, qt_ref, acc_ref, c_ref = refs[2 * pps:]
    j = pl.program_id(1)
    n_rows = n_tok * SUBLANES

    def attend(z_pages, vt, masks):
        drops, suffixes = [], []
        for z, mask in zip(z_pages, masks):
            drop = _softplus(z)
            if mask is not None:
                drop = jnp.where(mask, drop, 0.0)
            suffixes.append(_suffix_sum_lanes(drop))
        carry = c_ref[...]
        weights = []
        for z, suffix, mask in zip(z_pages, suffixes, masks):
            log_w = z - suffix - carry
            if mask is not None:
                log_w = jnp.where(mask, log_w, NEG_INF)
            weights.append(jnp.exp(log_w).astype(BF16))
            carry = carry + suffix[:, 0:1]
        c_ref[...] = carry
        acc_ref[...] += _dot_nt(vt, jnp.concatenate(weights, axis=1))

    @pl.when(j == 0)
    def _():
        q = q_ref[0] * (DH_SB ** -0.5)
        qt_ref[...] = _query_columns(q, H_SB, DH_SB).astype(BF16)
        acc_ref[...] = jnp.zeros_like(acc_ref)
        c_ref[...] = jnp.zeros_like(c_ref)
        kn = _pad_rows(kn_ref[0], PAGE_SIZE).astype(BF16)
        vn_t = _pad_rows(vn_ref[0], PAGE_SIZE).T.astype(BF16)
        row = lax.broadcasted_iota(I32, (n_rows, PAGE_SIZE), 0)
        key = lax.broadcasted_iota(I32, (n_rows, PAGE_SIZE), 1)
        attend([_dot_nt(qt_ref[...], kn)], vn_t, [key < jnp.right_shift(row, 3)])

    kt = jnp.concatenate([ref[0].astype(BF16) for ref in kt_refs], axis=1)
    vt = jnp.concatenate([ref[0].astype(BF16) for ref in vt_refs], axis=1)
    z = _dot(qt_ref[...], kt)
    attend([z[:, p * PAGE_SIZE:(p + 1) * PAGE_SIZE] for p in range(pps)], vt,
           [None] * pps)

    @pl.when(j == n_pages // pps - 1)
    def _():
        rows = _head_rows_to_tokens(acc_ref[...], n_tok, DH_SB)
        o_ref[0] = jnp.sum(rows.reshape(n_tok, SUBLANES, D_SB), axis=1)


def _page_specs(width, n_pages, pps):
    def spec(p):
        return pl.BlockSpec(
            (1, width, PAGE_SIZE),
            lambda bi, j, pt: (pt[bi, n_pages - 1 - (j * pps + p)], 0, 0))
    return [spec(p) for p in range(pps)]


def _sb_decode(q, k_new, v_new, cache_kt, cache_vt, page_table):
    b, t, _ = q.shape
    n_pages = page_table.shape[1]
    pps = SB_PAGES_PER_STEP
    tok = pl.BlockSpec((1, t, D_SB), lambda bi, j, pt: (bi, 0, 0))
    grid_spec = pltpu.PrefetchScalarGridSpec(
        num_scalar_prefetch=1,
        grid=(b, n_pages // pps),
        in_specs=[tok, tok, tok] + _page_specs(D_SB, n_pages, pps) + _page_specs(D_SB, n_pages, pps),
        out_specs=tok,
        scratch_shapes=[
            pltpu.VMEM((t * SUBLANES, D_SB), BF16),
            pltpu.VMEM((D_SB, t * SUBLANES), F32),
            pltpu.VMEM((t * SUBLANES, 1), F32),
        ],
    )
    return pl.pallas_call(
        functools.partial(_sb_decode_kernel, n_tok=t, n_pages=n_pages, pps=pps),
        grid_spec=grid_spec,
        out_shape=jax.ShapeDtypeStruct((b, t, D_SB), F32),
        compiler_params=_params(("parallel", "arbitrary")),
        name="sb_decode",
    )(page_table, q, k_new, v_new, *([cache_kt] * pps), *([cache_vt] * pps))


def _diff_decode_kernel(pt_ref, lam_ref, q_ref, kn_ref, vn_ref, bias_ref, g_ref, *refs,
                        n_tok, n_pages, lam_init, pps):
    kt_refs = refs[:pps]
    vt_refs = refs[pps:2 * pps]
    o_ref, qt_ref, acc_ref, m_ref, l_ref = refs[2 * pps:]
    j = pl.program_id(1)
    half = n_tok * SUBLANES
    scale = DQ_DIFF ** -0.5

    def attend(z, vt):
        m_prev = m_ref[...]
        m_new = jnp.maximum(m_prev, jnp.max(z, axis=1, keepdims=True))
        alpha = jnp.exp(m_prev - m_new)
        p = jnp.exp(z - m_new)
        l_ref[...] = alpha * l_ref[...] + jnp.sum(p, axis=1, keepdims=True)
        acc_ref[...] = (acc_ref[...] * _col_to_row(alpha)[:, 0:2 * half]
                        + _dot_nt(vt, p.astype(BF16)))
        m_ref[...] = m_new

    @pl.when(j == 0)
    def _():
        q = _query_columns(q_ref[0], H_DIFF, 2 * DQ_DIFF)
        lane = lax.broadcasted_iota(I32, q.shape, 1)
        first = (jnp.right_shift(lane, 5) & 1) == 0
        q2 = jnp.concatenate([jnp.where(first, q, 0.0), jnp.where(first, 0.0, q)], axis=0)
        qt_ref[...] = q2.astype(BF16)
        acc_ref[...] = jnp.zeros_like(acc_ref)
        m_ref[...] = jnp.full_like(m_ref, NEG_INF)
        l_ref[...] = jnp.zeros_like(l_ref)
        kn = _pad_rows(kn_ref[0], PAGE_SIZE).astype(BF16)
        vn_t = _pad_rows(vn_ref[0], PAGE_SIZE).T.astype(BF16)
        row = lax.broadcasted_iota(I32, (2 * half, PAGE_SIZE), 0)
        key = lax.broadcasted_iota(I32, (2 * half, PAGE_SIZE), 1)
        z_new = _dot_nt(qt_ref[...], kn) * scale + bias_ref[1, 0:2 * half]
        attend(jnp.where(key <= (jnp.right_shift(row, 3) & (n_tok - 1)), z_new, NEG_INF), vn_t)

    kt = jnp.concatenate([ref[0].astype(BF16) for ref in kt_refs], axis=1)
    vt = jnp.concatenate([ref[0].astype(BF16) for ref in vt_refs], axis=1)
    far = bias_ref[2, 0:2 * half]
    near = jnp.where(j == 0, bias_ref[0, 0:2 * half], far)
    bias = jnp.concatenate([near] + [far] * (pps - 1), axis=1)
    attend(_dot(qt_ref[...], kt) * scale + bias, vt)

    @pl.when(j == n_pages // pps - 1)
    def _():
        rows = _pad_lanes(acc_ref[...], LANES).T[0:2 * half] / l_ref[...]
        o = rows[0:half] - lam_ref[0, 0] * rows[half:2 * half]
        rr = lax.broadcasted_iota(I32, o.shape, 0)
        cc = lax.broadcasted_iota(I32, o.shape, 1)
        o = jnp.where(jnp.right_shift(cc, 6) == (rr & (SUBLANES - 1)), o, 0.0)
        ms = jnp.sum(o * o, axis=1, keepdims=True) * (1.0 / DV_DIFF)
        o = (o * lax.rsqrt(ms + EPS)) * g_ref[...] * (1.0 - lam_init)
        o_ref[0] = jnp.sum(o.reshape(n_tok, SUBLANES, D_DIFF), axis=1)


def _diff_decode(q, k_new, v_new, cache_kt, cache_vt, page_table, lam, dec_bias, g_tiled, lam_init):
    b, t, _ = q.shape
    n_pages = page_table.shape[1]
    pps = DIFF_PAGES_PER_STEP
    tok = pl.BlockSpec((1, t, D_DIFF), lambda bi, j, pt: (bi, 0, 0))
    grid_spec = pltpu.PrefetchScalarGridSpec(
        num_scalar_prefetch=1,
        grid=(b, n_pages // pps),
        in_specs=[
            pl.BlockSpec(memory_space=pltpu.SMEM),
            tok, tok, tok,
            pl.BlockSpec((3, LANES, PAGE_SIZE), lambda bi, j, pt: (0, 0, 0)),
            pl.BlockSpec((1, D_DIFF), lambda bi, j, pt: (0, 0)),
        ] + _page_specs(D_QK_DIFF, n_pages, pps) + _page_specs(D_DIFF, n_pages, pps),
        out_specs=tok,
        scratch_shapes=[
            pltpu.VMEM((2 * t * SUBLANES, D_QK_DIFF), BF16),
            pltpu.VMEM((D_DIFF, 2 * t * SUBLANES), F32),
            pltpu.VMEM((2 * t * SUBLANES, 1), F32),
            pltpu.VMEM((2 * t * SUBLANES, 1), F32),
        ],
    )
    return pl.pallas_call(
        functools.partial(_diff_decode_kernel, n_tok=t, n_pages=n_pages, lam_init=lam_init, pps=pps),
        grid_spec=grid_spec,
        out_shape=jax.ShapeDtypeStruct((b, t, D_DIFF), F32),
        compiler_params=_params(("parallel", "arbitrary")),
        name="diff_decode",
    )(page_table, lam, q, k_new, v_new, dec_bias, g_tiled,
      *([cache_kt] * pps), *([cache_vt] * pps))


def _finish_kernel(x_ref, osb_ref, odf_ref, omem_ref, gmix_ref, wg_ref, bg_ref, wsb_ref, wdf_ref,
                   wmem_ref, wout_ref, gffn_ref, wr_ref, br_ref, xmid_ref, xn_ref, route_ref):
    x = x_ref[...]
    u = _rmsnorm(x, gmix_ref[...]).astype(BF16)
    gates = jax.nn.sigmoid(_dot(u, wg_ref[...]) + bg_ref[...])
    h = (gates[:, 0:D_MODEL] * _dot(osb_ref[...].astype(BF16), wsb_ref[...])
         + gates[:, D_MODEL:2 * D_MODEL] * _dot(odf_ref[...].astype(BF16), wdf_ref[...])
         + gates[:, 2 * D_MODEL:3 * D_MODEL] * _dot(omem_ref[...].astype(BF16), wmem_ref[...]))
    xm = x + _dot(h.astype(BF16), wout_ref[...])
    xmid_ref[...] = xm
    xn = _rmsnorm(xm, gffn_ref[...])
    xn_ref[...] = xn
    xh, xl = _split_bf16(xn)
    wh, wl = _split_bf16(wr_ref[...])
    lg = _dot(xh, wh) + _dot(xh, wl) + _dot(xl, wh) + br_ref[...]
    lane = lax.broadcasted_iota(I32, lg.shape, 1)
    is_group = lane < N_GROUPS
    gl = jnp.where(is_group, lg, -jnp.inf)
    gmax = jnp.max(gl, axis=1, keepdims=True)
    grp = jnp.min(jnp.where(gl == gmax, lane, LANES), axis=1, keepdims=True)
    p_grp = 1.0 / jnp.sum(jnp.where(is_group, jnp.exp(gl - gmax), 0.0), axis=1, keepdims=True)
    in_group = (lane >= N_GROUPS) & (lane < N_GROUPS + N_EXPERTS) & (
        jnp.right_shift(lane - N_GROUPS, 3) == grp)
    el = jnp.where(in_group, lg, -jnp.inf)
    v1 = jnp.max(el, axis=1, keepdims=True)
    i1 = jnp.min(jnp.where(el == v1, lane, LANES), axis=1, keepdims=True)
    el2 = jnp.where(lane == i1, -jnp.inf, el)
    v2 = jnp.max(el2, axis=1, keepdims=True)
    i2 = jnp.min(jnp.where(el2 == v2, lane, LANES), axis=1, keepdims=True)
    e = jnp.exp(v2 - v1)
    w1 = (1.0 / (1.0 + e)) * p_grp
    w2 = (e / (1.0 + e)) * p_grp
    rec = jnp.where(lane == R_EID1, (i1 - N_GROUPS).astype(F32), 0.0)
    rec = jnp.where(lane == R_EID2, (i2 - N_GROUPS).astype(F32), rec)
    rec = jnp.where(lane == R_W1, w1, rec)
    rec = jnp.where(lane == R_W2, w2, rec)
    route_ref[...] = rec


def _finish(x, o_sb, o_df, o_mem, g_mix, wg, bg, wsb, wdf, wmem, wout, g_ffn, wr, br, name):
    n, d = x.shape
    tm = min(TOK_TILE, n)

    def rows(width):
        return pl.BlockSpec((tm, width), lambda i: (i, 0))

    def whole(a):
        return pl.BlockSpec(a.shape, lambda i: (0, 0))

    args = (x, o_sb, o_df, o_mem, g_mix.reshape(1, d), wg, bg.reshape(1, -1), wsb, wdf, wmem, wout,
            g_ffn.reshape(1, d), wr, br)
    in_specs = [rows(d), rows(D_SB), rows(D_DIFF), rows(D_MEM)] + [whole(a) for a in args[4:]]
    return pl.pallas_call(
        _finish_kernel,
        grid=(n // tm,),
        in_specs=in_specs,
        out_specs=[rows(d), rows(d), rows(LANES)],
        out_shape=[
            jax.ShapeDtypeStruct((n, d), F32),
            jax.ShapeDtypeStruct((n, d), F32),
            jax.ShapeDtypeStruct((n, LANES), F32),
        ],
        compiler_params=_params(("parallel",)),
        name=name,
    )(*args)


def _one_hots(route):
    lane = lax.broadcasted_iota(I32, route.shape, 1)
    oh1 = lane == route[:, R_EID1:R_EID1 + 1].astype(I32)
    oh2 = lane == route[:, R_EID2:R_EID2 + 1].astype(I32)
    return oh1, oh2


def _moe_rank_kernel(route_ref, rank_ref, count_ref, carry_ref):
    i = pl.program_id(0)
    tm = route_ref.shape[0]

    @pl.when(i == 0)
    def _():
        carry_ref[...] = jnp.zeros_like(carry_ref)

    oh1, oh2 = _one_hots(route_ref[...])
    both = jnp.where(oh1 | oh2, 1.0, 0.0)
    r = lax.broadcasted_iota(I32, (tm, tm), 0)
    c = lax.broadcasted_iota(I32, (tm, tm), 1)
    earlier = (c < r).astype(BF16)
    before = _dot(earlier, both.astype(BF16)) + carry_ref[...]
    lane = lax.broadcasted_iota(I32, (tm, LANES), 1)
    rank1 = jnp.sum(jnp.where(oh1, before, 0.0), axis=1, keepdims=True)
    rank2 = jnp.sum(jnp.where(oh2, before, 0.0), axis=1, keepdims=True)
    rank_ref[...] = jnp.where(lane == 0, rank1, jnp.where(lane == 1, rank2, 0.0))
    carry_ref[...] += jnp.sum(both, axis=0, keepdims=True)
    count_ref[...] = jnp.broadcast_to(carry_ref[...], count_ref.shape)


def _moe_rank(route):
    n = route.shape[0]
    tm = min(TOK_TILE, n)
    return pl.pallas_call(
        _moe_rank_kernel,
        grid=(n // tm,),
        in_specs=[pl.BlockSpec((tm, LANES), lambda i: (i, 0))],
        out_specs=[pl.BlockSpec((tm, LANES), lambda i: (i, 0)),
                   pl.BlockSpec((SUBLANES, LANES), lambda i: (0, 0))],
        out_shape=[jax.ShapeDtypeStruct((n, LANES), F32),
                   jax.ShapeDtypeStruct((SUBLANES, LANES), F32)],
        scratch_shapes=[pltpu.VMEM((1, LANES), F32)],
        compiler_params=_params(("arbitrary",)),
        name="moe_rank",
    )(route)


def _moe_dest_kernel(route_ref, rank_ref, count_ref, dest_ref, blk_ref):
    n_blk = blk_ref.shape[0]
    blocks = jnp.right_shift(count_ref[...].astype(I32) + (MOE_ROWS - 1), MOE_ROWS_LOG2).astype(F32)
    r = lax.broadcasted_iota(I32, (LANES, LANES), 0)
    c = lax.broadcasted_iota(I32, (LANES, LANES), 1)
    upto = (r <= c).astype(BF16)
    block_end = _dot(blocks.astype(BF16), upto)
    row_start = (block_end - blocks)[0:1] * float(MOE_ROWS)
    oh1, oh2 = _one_hots(route_ref[...])
    rank = rank_ref[...]
    d1 = jnp.sum(jnp.where(oh1, row_start, 0.0), axis=1, keepdims=True) + rank[:, 0:1]
    d2 = jnp.sum(jnp.where(oh2, row_start, 0.0), axis=1, keepdims=True) + rank[:, 1:2]
    lane = lax.broadcasted_iota(I32, rank.shape, 1)
    dest_ref[...] = jnp.where(lane == 0, d1, jnp.where(lane == 1, d2, 0.0)).astype(I32)

    @pl.when(pl.program_id(0) == 0)
    def _():
        b_idx = lax.broadcasted_iota(I32, (n_blk, LANES), 0).astype(F32)
        lane_b = lax.broadcasted_iota(I32, (n_blk, LANES), 1)
        done = (block_end[0:1] <= b_idx) & (lane_b < N_EXPERTS)
        expert = jnp.minimum(jnp.sum(jnp.where(done, 1.0, 0.0), axis=1, keepdims=True),
                             float(N_EXPERTS - 1))
        used = block_end[0:1, N_EXPERTS - 1:N_EXPERTS]
        blk_ref[...] = jnp.where(lane_b == 0, expert, jnp.where(lane_b == 1, used, 0.0)).astype(I32)


def _moe_dest(route, rank, counts, n_blk):
    n = route.shape[0]
    tm = min(TOK_TILE, n)
    n_blk_pad = -(-n_blk // SUBLANES) * SUBLANES
    return pl.pallas_call(
        _moe_dest_kernel,
        grid=(n // tm,),
        in_specs=[pl.BlockSpec((tm, LANES), lambda i: (i, 0)),
                  pl.BlockSpec((tm, LANES), lambda i: (i, 0)),
                  pl.BlockSpec((SUBLANES, LANES), lambda i: (0, 0))],
        out_specs=[pl.BlockSpec((tm, LANES), lambda i: (i, 0)),
                   pl.BlockSpec((n_blk_pad, LANES), lambda i: (0, 0))],
        out_shape=[jax.ShapeDtypeStruct((n, LANES), I32),
                   jax.ShapeDtypeStruct((n_blk_pad, LANES), I32)],
        compiler_params=_params(("arbitrary",)),
        name="moe_dest",
    )(route, rank, counts)


def _dispatch_kernel(dest_ref, x_ref, init_ref, xs_ref, sem):
    del init_ref
    tm = x_ref.shape[0]

    def row_copy(t, d):
        return pltpu.make_async_copy(x_ref.at[pl.ds(t, 1)], xs_ref.at[pl.ds(d, 1)], sem)

    def start(t, carry):
        row_copy(t, dest_ref[0, 0, 2 * t]).start(priority=0)
        row_copy(t, dest_ref[0, 0, 2 * t + 1]).start(priority=1)
        return carry

    def wait(t, carry):
        row_copy(t, dest_ref[0, 0, 2 * t]).wait()
        row_copy(t, dest_ref[0, 0, 2 * t + 1]).wait()
        return carry

    lax.fori_loop(0, tm, start, 0, unroll=ROW_DMA_UNROLL)
    lax.fori_loop(0, tm, wait, 0, unroll=ROW_DMA_UNROLL)


def _dispatch(dest_tiles, xn, n_rows):
    n, d = xn.shape
    tm = dest_tiles.shape[2] // 2
    return pl.pallas_call(
        _dispatch_kernel,
        grid=(n // tm,),
        in_specs=[
            pl.BlockSpec((1, 1, 2 * tm), lambda i: (i, 0, 0), memory_space=pltpu.SMEM),
            pl.BlockSpec((tm, d), lambda i: (i, 0)),
            pl.BlockSpec(memory_space=pl.ANY),
        ],
        out_specs=pl.BlockSpec(memory_space=pl.ANY),
        out_shape=jax.ShapeDtypeStruct((n_rows, d), F32),
        scratch_shapes=[pltpu.SemaphoreType.DMA(())],
        input_output_aliases={2: 0},
        compiler_params=_params(("arbitrary",)),
        name="moe_dispatch",
    )(dest_tiles, xn, jnp.zeros((n_rows, d), F32))


def _expert_kernel(be_ref, used_ref, xs_ref, wg_ref, wu_ref, wd_ref, y_ref, wg_b, wu_b, wd_b):
    b = pl.program_id(0)

    @pl.when(b < used_ref[0])
    def _():
        @pl.when((b == 0) | (be_ref[b] != be_ref[jnp.maximum(b - 1, 0)]))
        def _():
            wg_b[...] = wg_ref[0].astype(BF16)
            wu_b[...] = wu_ref[0].astype(BF16)
            wd_b[...] = wd_ref[0].astype(BF16)

        x = xs_ref[...].astype(BF16)
        h = jax.nn.silu(_dot(x, wg_b[...])) * _dot(x, wu_b[...])
        y_ref[...] = _dot(h.astype(BF16), wd_b[...])

    @pl.when(b >= used_ref[0])
    def _():
        y_ref[...] = jnp.zeros_like(y_ref)


def _experts(blk_exp, used, xs, wg, wu, wd):
    n_rows, d = xs.shape
    n_blk = n_rows // MOE_ROWS
    grid_spec = pltpu.PrefetchScalarGridSpec(
        num_scalar_prefetch=2,
        grid=(n_blk,),
        in_specs=[
            pl.BlockSpec((MOE_ROWS, d), lambda b, be, used: (b, 0)),
            pl.BlockSpec((1, d, D_EXPERT), lambda b, be, used: (be[b], 0, 0)),
            pl.BlockSpec((1, d, D_EXPERT), lambda b, be, used: (be[b], 0, 0)),
            pl.BlockSpec((1, D_EXPERT, d), lambda b, be, used: (be[b], 0, 0)),
        ],
        out_specs=pl.BlockSpec((MOE_ROWS, d), lambda b, be, used: (b, 0)),
        scratch_shapes=[pltpu.VMEM((d, D_EXPERT), BF16), pltpu.VMEM((d, D_EXPERT), BF16),
                        pltpu.VMEM((D_EXPERT, d), BF16)],
    )
    return pl.pallas_call(
        _expert_kernel,
        grid_spec=grid_spec,
        out_shape=jax.ShapeDtypeStruct((n_rows, d), F32),
        compiler_params=_params(("arbitrary",)),
        name="moe_experts",
    )(blk_exp, used, xs, wg, wu, wd)


def _combine_kernel(dest_ref, route_ref, xmid_ref, g_ref, yb_ref, out_ref, buf_ref, sem):
    tm = xmid_ref.shape[0]

    def row_copy(t, k, d):
        return pltpu.make_async_copy(yb_ref.at[pl.ds(d, 1)], buf_ref.at[k, pl.ds(t, 1)], sem)

    def start(t, carry):
        row_copy(t, 0, dest_ref[0, 0, 2 * t]).start(priority=0)
        row_copy(t, 1, dest_ref[0, 0, 2 * t + 1]).start(priority=1)
        return carry

    def wait(t, carry):
        row_copy(t, 0, dest_ref[0, 0, 2 * t]).wait()
        row_copy(t, 1, dest_ref[0, 0, 2 * t + 1]).wait()
        return carry

    lax.fori_loop(0, tm, start, 0, unroll=ROW_DMA_UNROLL)
    lax.fori_loop(0, tm, wait, 0, unroll=ROW_DMA_UNROLL)
    route = route_ref[...]
    y = buf_ref[0] * route[:, R_W1:R_W1 + 1] + buf_ref[1] * route[:, R_W2:R_W2 + 1]
    out_ref[...] = _rmsnorm(xmid_ref[...] + y, g_ref[...])


def _combine(dest_tiles, route, xmid, g_final, yb):
    n, d = xmid.shape
    tm = dest_tiles.shape[2] // 2
    return pl.pallas_call(
        _combine_kernel,
        grid=(n // tm,),
        in_specs=[
            pl.BlockSpec((1, 1, 2 * tm), lambda i: (i, 0, 0), memory_space=pltpu.SMEM),
            pl.BlockSpec((tm, LANES), lambda i: (i, 0)),
            pl.BlockSpec((tm, d), lambda i: (i, 0)),
            pl.BlockSpec((1, d), lambda i: (0, 0)),
            pl.BlockSpec(memory_space=pl.ANY),
        ],
        out_specs=pl.BlockSpec((tm, d), lambda i: (i, 0)),
        out_shape=jax.ShapeDtypeStruct((n, d), F32),
        scratch_shapes=[pltpu.VMEM((2, tm, d), F32), pltpu.SemaphoreType.DMA(())],
        compiler_params=_params(("arbitrary",)),
        name="moe_combine",
    )(dest_tiles, route, xmid, g_final.reshape(1, d), yb)


def _moe_and_final_norm(xmid, xn, route, wg, wu, wd, g_final):
    n, d = xmid.shape
    n_blk = -(-(2 * n + N_EXPERTS * (MOE_ROWS - 1)) // MOE_ROWS)
    rank, counts = _moe_rank(route)
    dest, blk = _moe_dest(route, rank, counts, n_blk)
    tm = min(ROW_TILE, n)
    dest_tiles = dest[:, 0:2].reshape(n // tm, 1, 2 * tm)
    xs = _dispatch(dest_tiles, xn, n_blk * MOE_ROWS)
    yb = _experts(blk[0:n_blk, 0], blk[0, 1:2], xs, wg, wu, wd)
    return _combine(dest_tiles, route, xmid, g_final, yb)


def _col_scale():
    s = jnp.ones((1, D_IN), F32)
    return s.at[:, OFF_Q_SB:OFF_Q_SB + D_SB].set(DH_SB ** -0.5)


def kernel(x_prompt, x_sample, cache_sb_k, cache_sb_v, cache_diff_k, cache_diff_v, cache_mem_k, cache_mem_v, page_table, mem_prompt, norm_mix_g, w_in, diff_lam_q1, diff_lam_k1, diff_lam_q2, diff_lam_k2, diff_norm_g, mem_norm_g, w_mem_kv, w_gate, b_gate, w_br_sb, w_br_diff, w_br_mem, w_out, norm_ffn_g, w_router_group, b_router_group, w_router_expert, b_router_expert, w_exp_gate, w_exp_up, w_exp_down, rel_bias, norm_final_g):
    depth = w_in.shape[0]
    assert depth == 1, "single-layer stack only"
    assert page_table.shape[1] % SB_PAGES_PER_STEP == 0 and page_table.shape[1] % DIFF_PAGES_PER_STEP == 0
    b, t, d = x_prompt.shape
    bs, ts, _ = x_sample.shape
    n_mem = mem_prompt.shape[1]
    sb_tq, diff_blk = min(SB_QUERY_BLOCK, t), min(DIFF_BLOCK, t)
    lam_init = 0.8 - 0.6 * math.exp(-0.3 * 0)

    w_in_b = w_in[0].astype(BF16)
    wg_b, wsb_b, wdf_b = w_gate[0].astype(BF16), w_br_sb[0].astype(BF16), w_br_diff[0].astype(BF16)
    wmem_b, wout_b = w_br_mem[0].astype(BF16), w_out[0].astype(BF16)
    weg_b, weu_b, wed_b = w_exp_gate[0], w_exp_up[0], w_exp_down[0]
    pad = LANES - N_GROUPS - N_EXPERTS
    w_router = jnp.concatenate(
        [w_router_group[0], w_router_expert[0], jnp.zeros((d, pad), F32)], axis=1)
    b_router = jnp.concatenate(
        [b_router_group[0], b_router_expert[0], jnp.zeros((pad,), F32)]).reshape(1, LANES)
    lam_vecs = jnp.concatenate([diff_lam_q1, diff_lam_k1, diff_lam_q2, diff_lam_k2], axis=0)

    tz, dec_bias, lam_tile = _prep(rel_bias, lam_vecs, diff_blk, lam_init)
    lam = lam_tile[0:1, 0:1]

    def heads_last(a_t, n_heads, width):
        bb, _, tt = a_t.shape
        return jnp.transpose(a_t.reshape(bb, n_heads, width, tt), (0, 3, 1, 2))[None]

    def keys_last(cache, n_heads, width):
        pool, page = cache.shape[1], cache.shape[2]
        return jnp.transpose(cache[0], (0, 2, 3, 1)).reshape(pool, n_heads * width, page)

    k_sb, v_sb, k_df, v_df, pb = _norm_proj(
        x_prompt, norm_mix_g[0], w_in_b, _col_scale(),
        [(OFF_K_SB, D_SB), (OFF_V_SB, D_SB), (OFF_K_DF, D_QK_DIFF), (OFF_V_DF, D_DIFF)], True,
        "proj_prompt")
    mk, mv, mem_b = _norm_proj(
        mem_prompt, mem_norm_g[0], w_mem_kv[0].astype(BF16),
        jnp.ones((1, 2 * D_MEM), F32), [(0, D_MEM), (D_MEM, D_MEM)], True, "proj_memory")
    o_sb = _sb_prompt(pb, sb_tq, SB_KEY_BLOCK)
    o_df = _diff_prompt(pb, tz, lam, rel_bias, diff_norm_g[0], diff_blk, lam_init)
    o_mem = _mem_attn(pb, OFF_Q_MEM // D_MEM, mem_b, 0, mem_b, 1, MEM_QUERY_BLOCK, "mem_prompt")
    xmid, xn, route = _finish(
        x_prompt.reshape(b * t, d), o_sb.reshape(b * t, D_SB), o_df.reshape(b * t, D_DIFF),
        o_mem.reshape(b * t, D_MEM), norm_mix_g[0], wg_b, b_gate[0], wsb_b, wdf_b, wmem_b, wout_b,
        norm_ffn_g[0], w_router, b_router, "finish_prompt")
    y_prompt = _moe_and_final_norm(xmid, xn, route, weg_b, weu_b, wed_b, norm_final_g)

    q_sb_s, k_sb_s, v_sb_s, q_df_s, k_df_s, v_df_s, q_mem_s, _ = _norm_proj(
        x_sample.reshape(1, bs * ts, d), norm_mix_g[0], w_in_b, jnp.ones((1, D_IN), F32),
        [(OFF_Q_SB, D_SB), (OFF_K_SB, D_SB), (OFF_V_SB, D_SB), (OFF_Q_DF, D_QK_DIFF),
         (OFF_K_DF, D_QK_DIFF), (OFF_V_DF, D_DIFF), (OFF_Q_MEM, D_MEM)], False, "proj_sample")
    o_sb_s = _sb_decode(
        q_sb_s.reshape(bs, ts, D_SB), k_sb_s.reshape(bs, ts, D_SB), v_sb_s.reshape(bs, ts, D_SB),
        keys_last(cache_sb_k, H_SB, DH_SB), keys_last(cache_sb_v, H_SB, DH_SB), page_table)
    o_df_s = _diff_decode(
        q_df_s.reshape(bs, ts, D_QK_DIFF), k_df_s.reshape(bs, ts, D_QK_DIFF),
        v_df_s.reshape(bs, ts, D_DIFF), keys_last(cache_diff_k, H_DIFF, 2 * DQ_DIFF),
        keys_last(cache_diff_v, H_DIFF, DV_DIFF), page_table, lam, dec_bias,
        jnp.tile(diff_norm_g[0], H_DIFF).reshape(1, D_DIFF), lam_init)
    o_mem_s = _mem_attn(
        q_mem_s.reshape(bs, ts, D_MEM), 0, cache_mem_k[0].reshape(bs, n_mem, D_MEM), 0,
        cache_mem_v[0].reshape(bs, n_mem, D_MEM), 0, ts, "mem_sample")
    xmid_s, xn_s, route_s = _finish(
        x_sample.reshape(bs * ts, d), o_sb_s.reshape(bs * ts, D_SB), o_df_s.reshape(bs * ts, D_DIFF),
        o_mem_s.reshape(bs * ts, D_MEM), norm_mix_g[0], wg_b, b_gate[0], wsb_b, wdf_b, wmem_b, wout_b,
        norm_ffn_g[0], w_router, b_router, "finish_sample")
    y_sample = _moe_and_final_norm(xmid_s, xn_s, route_s, weg_b, weu_b, wed_b, norm_final_g)

    return (y_prompt.reshape(b, t, d), y_sample.reshape(bs, ts, d),
            heads_last(k_sb, H_SB, DH_SB), heads_last(v_sb, H_SB, DH_SB),
            heads_last(k_df, H_DIFF, 2 * DQ_DIFF), heads_last(v_df, H_DIFF, DV_DIFF),
            heads_last(mk, H_MEM, DH_MEM), heads_last(mv, H_MEM, DH_MEM),
            k_sb_s.reshape(1, bs, ts, H_SB, DH_SB), v_sb_s.reshape(1, bs, ts, H_SB, DH_SB),
            k_df_s.reshape(1, bs, ts, H_DIFF, 2 * DQ_DIFF), v_df_s.reshape(1, bs, ts, H_DIFF, DV_DIFF))
```

```python
import functools
import math

import jax
import jax.numpy as jnp
from jax import lax
from jax.experimental import pallas as pl
from jax.experimental.pallas import tpu as pltpu

F32 = jnp.float32
BF16 = jnp.bfloat16
I32 = jnp.int32

D_MODEL = 1024
PAGE_SIZE = 128
H_SB = 8
DH_SB = 64
H_DIFF = 4
DQ_DIFF = 32
DV_DIFF = 64
H_MEM = 4
DH_MEM = 64
D_SB = H_SB * DH_SB
D_DIFF = H_DIFF * DV_DIFF
D_MEM = H_MEM * DH_MEM
D_QK_DIFF = H_DIFF * 2 * DQ_DIFF
D_IN = 3 * D_SB + 2 * D_QK_DIFF + D_DIFF + D_MEM
N_BUCKETS = 32
MAX_EXACT = 16
MAX_DISTANCE = 128
N_GROUPS = 4
EXPERTS_PER_GROUP = 8
N_EXPERTS = N_GROUPS * EXPERTS_PER_GROUP
D_EXPERT = 512
EPS = 1e-6
NEG_INF = -1e30

LANES = 128
SUBLANES = 8
VMEM_LIMIT = 48 * 1024 * 1024
DIFF_VMEM_LIMIT = 56 * 1024 * 1024

OFF_Q_SB = 0
OFF_K_SB = D_SB
OFF_V_SB = 2 * D_SB
OFF_Q_DF = 3 * D_SB
OFF_K_DF = OFF_Q_DF + D_QK_DIFF
OFF_V_DF = OFF_K_DF + D_QK_DIFF
OFF_Q_MEM = OFF_V_DF + D_DIFF

MEM_QUERY_BLOCK = 256
SB_QUERY_BLOCK = 2048
SB_KEY_BLOCK = 256
DIFF_BLOCK = 512
MOE_ROWS = 256
MOE_ROWS_LOG2 = 8
TOK_TILE = 256
ROW_TILE = 128
ROW_DMA_UNROLL = 32
SB_PAGES_PER_STEP = 32
DIFF_PAGES_PER_STEP = 64

R_EID1, R_EID2, R_W1, R_W2, R_RANK1, R_RANK2 = 0, 1, 2, 3, 4, 5


def _params(sem, vmem=VMEM_LIMIT):
    return pltpu.CompilerParams(dimension_semantics=sem, vmem_limit_bytes=vmem)


def _rmsnorm(x, g):
    ms = jnp.mean(x * x, axis=-1, keepdims=True)
    return (x * lax.rsqrt(ms + EPS)) * g


def _dot(a, b):
    return jnp.dot(a, b, preferred_element_type=F32)


def _dot_nt(a, b):
    return lax.dot_general(a, b, (((1,), (1,)), ((), ())), preferred_element_type=F32)


def _split_bf16(x):
    hi = x.astype(BF16)
    lo = (x - hi.astype(F32)).astype(BF16)
    return hi, lo


def _softplus(z):
    neg_abs = pltpu.bitcast(pltpu.bitcast(z, jnp.uint32) | jnp.uint32(0x80000000), F32)
    return jnp.maximum(z, 0.0) + jnp.log(1.0 + jnp.exp(neg_abs))


def _col_to_row(v):
    n = v.shape[0]
    r = lax.broadcasted_iota(I32, (n, LANES), 0)
    c = lax.broadcasted_iota(I32, (n, LANES), 1)
    return jnp.sum(jnp.where(r == c, v, 0.0), axis=0, keepdims=True)


def _norm_proj_kernel(x_ref, g_ref, w_ref, s_ref, *out_refs, f32_cols, transposed):
    u = _rmsnorm(x_ref[0], g_ref[...]).astype(BF16)
    p = _dot(u, w_ref[...])
    for ref, (lo, width) in zip(out_refs[:-1], f32_cols):
        ref[0] = p[:, lo:lo + width].T if transposed else p[:, lo:lo + width]
    out_refs[-1][0] = (p * s_ref[...]).astype(BF16)


def _norm_proj(x, g, w_bf16, col_scale, f32_cols, transposed, name):
    b, t, d = x.shape
    n_out = w_bf16.shape[1]
    tm = min(TOK_TILE, t)
    if transposed:
        out_shape = [jax.ShapeDtypeStruct((b, width, t), F32) for _, width in f32_cols]
        out_specs = [pl.BlockSpec((1, width, tm), lambda bi, i: (bi, 0, i)) for _, width in f32_cols]
    else:
        out_shape = [jax.ShapeDtypeStruct((b, t, width), F32) for _, width in f32_cols]
        out_specs = [pl.BlockSpec((1, tm, width), lambda bi, i: (bi, i, 0)) for _, width in f32_cols]
    out_shape.append(jax.ShapeDtypeStruct((b, t, n_out), BF16))
    out_specs.append(pl.BlockSpec((1, tm, n_out), lambda bi, i: (bi, i, 0)))
    return pl.pallas_call(
        functools.partial(_norm_proj_kernel, f32_cols=tuple(f32_cols), transposed=transposed),
        grid=(b, t // tm),
        in_specs=[
            pl.BlockSpec((1, tm, d), lambda bi, i: (bi, i, 0)),
            pl.BlockSpec((1, d), lambda bi, i: (0, 0)),
            pl.BlockSpec((d, n_out), lambda bi, i: (0, 0)),
            pl.BlockSpec((1, n_out), lambda bi, i: (0, 0)),
        ],
        out_specs=out_specs,
        out_shape=out_shape,
        compiler_params=_params(("parallel", "parallel")),
        name=name,
    )(x, g.reshape(1, d), w_bf16, col_scale)


def _t5_bucket(delta):
    n = jnp.maximum(delta, 0)
    nf = jnp.maximum(n, 1).astype(F32)
    large = MAX_EXACT + (jnp.log(nf / MAX_EXACT) / math.log(MAX_DISTANCE / MAX_EXACT)
                         * (N_BUCKETS - MAX_EXACT)).astype(I32)
    large = jnp.minimum(large, N_BUCKETS - 1)
    return jnp.where(n < MAX_EXACT, n, large)


def _bias_of_bucket(bucket, rel_ref, head):
    out = jnp.zeros(bucket.shape, F32)
    for b in range(N_BUCKETS):
        out = jnp.where(bucket == b, rel_ref[b, head], out)
    return out


def _bias_by_head(bucket, head, rel_ref):
    acc = jnp.zeros(bucket.shape, F32)
    for h in range(H_DIFF):
        acc = jnp.where(head == h, _bias_of_bucket(bucket, rel_ref, h), acc)
    return acc


def _prep_kernel(rel_ref, lam_ref, tz_ref, dec_ref, lam_out_ref, *, blk, lam_init):
    r = lax.broadcasted_iota(I32, (blk, blk), 0)
    c = lax.broadcasted_iota(I32, (blk, blk), 1)
    for off in range(2):
        bucket = _t5_bucket(r - c + off * blk)
        for h in range(H_DIFF):
            tz_ref[h, off] = _bias_of_bucket(bucket, rel_ref, h)
    key = lax.broadcasted_iota(I32, (LANES, PAGE_SIZE), 1)
    row = lax.broadcasted_iota(I32, (LANES, PAGE_SIZE), 0)
    qi = jnp.right_shift(row, 3) & 3
    head = row & (SUBLANES - 1)
    dec_ref[0] = _bias_by_head(_t5_bucket(PAGE_SIZE + qi - key), head, rel_ref)
    dec_ref[1] = _bias_by_head(_t5_bucket(qi - key), head, rel_ref)
    dec_ref[2] = _bias_by_head(jnp.full((LANES, PAGE_SIZE), N_BUCKETS - 1, I32), head, rel_ref)
    lq1, lk1, lq2, lk2 = lam_ref[0:1, :], lam_ref[1:2, :], lam_ref[2:3, :], lam_ref[3:4, :]
    lam = (jnp.exp(jnp.sum(lq1 * lk1, axis=-1, keepdims=True))
           - jnp.exp(jnp.sum(lq2 * lk2, axis=-1, keepdims=True)) + lam_init)
    lam_out_ref[...] = jnp.broadcast_to(lam, (SUBLANES, LANES))


def _prep(rel_bias, lam_vecs, blk, lam_init):
    return pl.pallas_call(
        functools.partial(_prep_kernel, blk=blk, lam_init=lam_init),
        in_specs=[
            pl.BlockSpec(memory_space=pltpu.SMEM),
            pl.BlockSpec(memory_space=pltpu.VMEM),
        ],
        out_specs=[pl.BlockSpec(memory_space=pltpu.VMEM)] * 3,
        out_shape=[
            jax.ShapeDtypeStruct((H_DIFF, 2, blk, blk), F32),
            jax.ShapeDtypeStruct((3, PAGE_SIZE, LANES), F32),
            jax.ShapeDtypeStruct((SUBLANES, LANES), F32),
        ],
        name="prep_bias_lambda",
    )(rel_bias, lam_vecs)


def _sb_prompt_kernel(q_ref, k_ref, v_ref, o_ref, acc_ref, c_ref, *, tq, tk):
    i = pl.program_id(2)
    n_diag = tq // tk
    r = lax.broadcasted_iota(I32, (tk, tk), 0)
    c = lax.broadcasted_iota(I32, (tk, tk), 1)
    tri = (r >= c).astype(BF16)

    def block(j, row0, diag):
        start = pl.multiple_of(j * tk, tk)
        rows = tq - row0
        if diag:
            rr = lax.broadcasted_iota(I32, (rows, tk), 0)
            cc = lax.broadcasted_iota(I32, (rows, tk), 1)
            strict = cc < rr
        for hh in range(2):
            lo = hh * DH_SB
            q = q_ref[0, row0:tq, lo:lo + DH_SB]
            k = k_ref[0, pl.ds(start, tk), lo:lo + DH_SB]
            v = v_ref[0, pl.ds(start, tk), lo:lo + DH_SB]
            z = _dot_nt(q, k)
            drop = _softplus(z)
            if diag:
                drop = jnp.where(strict, drop, 0.0)
            suffix = _dot(drop.astype(BF16), tri)
            log_w = z - suffix - c_ref[hh, row0:tq]
            if diag:
                log_w = jnp.where(strict, log_w, NEG_INF)
            a = jnp.exp(log_w)
            acc_ref[hh, row0:tq] += _dot(a.astype(BF16), v)
            c_ref[hh, row0:tq] += suffix[:, 0:1]

    acc_ref[...] = jnp.zeros_like(acc_ref)
    c_ref[...] = jnp.zeros_like(c_ref)
    for dd in range(n_diag - 1, -1, -1):
        block(i * n_diag + dd, dd * tk, True)

    def body(kk, carry):
        block(i * n_diag - 1 - kk, 0, False)
        return carry

    lax.fori_loop(0, i * n_diag, body, 0)
    o_ref[0] = jnp.concatenate([acc_ref[0], acc_ref[1]], axis=1).astype(BF16)


def _sb_prompt(pb, tq, tk):
    b, t, _ = pb.shape
    qb, kb, vb = OFF_Q_SB // LANES, OFF_K_SB // LANES, OFF_V_SB // LANES
    return pl.pallas_call(
        functools.partial(_sb_prompt_kernel, tq=tq, tk=tk),
        grid=(b, H_SB // 2, t // tq),
        in_specs=[
            pl.BlockSpec((1, tq, LANES), lambda bi, hp, i: (bi, i, qb + hp)),
            pl.BlockSpec((1, t, LANES), lambda bi, hp, i: (bi, 0, kb + hp)),
            pl.BlockSpec((1, t, LANES), lambda bi, hp, i: (bi, 0, vb + hp)),
        ],
        out_specs=pl.BlockSpec((1, tq, LANES), lambda bi, hp, i: (bi, i, hp)),
        out_shape=jax.ShapeDtypeStruct((b, t, D_SB), BF16),
        scratch_shapes=[pltpu.VMEM((2, tq, DH_SB), F32), pltpu.VMEM((2, tq, 1), F32)],
        compiler_params=_params(("parallel", "parallel", "arbitrary")),
        name="sb_prompt",
    )(pb, pb, pb)


def _diff_prompt_kernel(lam_ref, rel_ref, q_ref, k_ref, v_ref, tz_ref, g_ref, o_ref,
                        q2_ref, m_ref, acc_ref, z_ref, *, blk, lam_init):
    hp = pl.program_id(1)
    i = pl.program_id(2)
    r = lax.broadcasted_iota(I32, (2 * blk, blk), 0)
    c = lax.broadcasted_iota(I32, (2 * blk, blk), 1)
    causal = c <= jnp.where(r >= blk, r - blk, r)
    lane = lax.broadcasted_iota(I32, (blk, DV_DIFF), 1)
    ones_col = (lane == 0).astype(BF16)
    scale = DQ_DIFF ** -0.5
    for hh in range(2):
        q = q_ref[0, :, hh * DV_DIFF:(hh + 1) * DV_DIFF]
        zero = jnp.zeros_like(q)
        q2_ref[hh, 0:blk, :] = jnp.where(lane < DQ_DIFF, q, zero)
        q2_ref[hh, blk:2 * blk, :] = jnp.where(lane >= DQ_DIFF, q, zero)
    m_ref[...] = jnp.full_like(m_ref, NEG_INF)
    acc_ref[...] = jnp.zeros_like(acc_ref)

    def logits(j, buf):
        start = pl.multiple_of(j * blk, blk)
        for hh in range(2):
            k = k_ref[0, pl.ds(start, blk), hh * DV_DIFF:(hh + 1) * DV_DIFF]
            z_ref[buf, hh] = _dot_nt(q2_ref[hh], k)

    def consume(j, buf, kind):
        start = pl.multiple_of(j * blk, blk)
        for hh in range(2):
            lo = hh * DV_DIFF
            v = jnp.concatenate([v_ref[0, pl.ds(start, blk), lo:lo + DV_DIFF], ones_col], axis=1)
            z = z_ref[buf, hh] * scale
            if kind == 2:
                z = z + rel_ref[N_BUCKETS - 1, hp * 2 + hh]
            else:
                bias = tz_ref[hh, kind]
                z = z + jnp.concatenate([bias, bias], axis=0)
            if kind == 0:
                z = jnp.where(causal, z, NEG_INF)
            m_prev = m_ref[hh]
            m_new = jnp.maximum(m_prev, jnp.max(z, axis=1, keepdims=True))
            alpha = jnp.exp(m_prev - m_new)
            p = jnp.exp(z - m_new)
            acc_ref[hh] = alpha * acc_ref[hh] + _dot(p.astype(BF16), v)
            m_ref[hh] = m_new

    logits(i, 0)
    logits(jnp.maximum(i - 1, 0), 1)
    consume(i, 0, 0)

    @pl.when(i >= 1)
    def _():
        logits(jnp.maximum(i - 2, 0), 0)
        consume(i - 1, 1, 1)

    n_far = jnp.maximum(i - 1, 0)

    def pair(p, carry):
        ja = i - 2 - 2 * p
        logits(ja - 1, 1)
        consume(ja, 0, 2)
        logits(jnp.maximum(ja - 2, 0), 0)
        consume(ja - 1, 1, 2)
        return carry

    lax.fori_loop(0, n_far // 2, pair, 0)

    @pl.when(n_far % 2 == 1)
    def _():
        consume(0, 0, 2)

    lam = lam_ref[0, 0]
    outs = []
    for hh in range(2):
        acc = acc_ref[hh]
        o = acc[:, 0:DV_DIFF] / acc[:, DV_DIFF:DV_DIFF + 1]
        o = o[0:blk] - lam * o[blk:2 * blk]
        outs.append(_rmsnorm(o, g_ref[...]) * (1.0 - lam_init))
    o_ref[0] = jnp.concatenate(outs, axis=1).astype(BF16)


def _diff_prompt(pb, tz, lam, rel_bias, g_diff, blk, lam_init):
    b, t, _ = pb.shape
    qb, kb, vb = OFF_Q_DF // LANES, OFF_K_DF // LANES, OFF_V_DF // LANES
    return pl.pallas_call(
        functools.partial(_diff_prompt_kernel, blk=blk, lam_init=lam_init),
        grid=(b, H_DIFF // 2, t // blk),
        in_specs=[
            pl.BlockSpec(memory_space=pltpu.SMEM),
            pl.BlockSpec(memory_space=pltpu.SMEM),
            pl.BlockSpec((1, blk, LANES), lambda bi, hp, i: (bi, i, qb + hp)),
            pl.BlockSpec((1, t, LANES), lambda bi, hp, i: (bi, 0, kb + hp)),
            pl.BlockSpec((1, t, LANES), lambda bi, hp, i: (bi, 0, vb + hp)),
            pl.BlockSpec((2, 2, blk, blk), lambda bi, hp, i: (hp, 0, 0, 0)),
            pl.BlockSpec((1, DV_DIFF), lambda bi, hp, i: (0, 0)),
        ],
        out_specs=pl.BlockSpec((1, blk, LANES), lambda bi, hp, i: (bi, i, hp)),
        out_shape=jax.ShapeDtypeStruct((b, t, D_DIFF), BF16),
        scratch_shapes=[
            pltpu.VMEM((2, 2 * blk, DV_DIFF), BF16),
            pltpu.VMEM((2, 2 * blk, 1), F32),
            pltpu.VMEM((2, 2 * blk, 2 * DV_DIFF), F32),
            pltpu.VMEM((2, 2, 2 * blk, blk), F32),
        ],
        compiler_params=_params(("parallel", "parallel", "arbitrary"), vmem=DIFF_VMEM_LIMIT),
        name="diff_prompt",
    )(lam, rel_bias, pb, pb, pb, tz, g_diff.reshape(1, DV_DIFF))


def _mem_attn_kernel(q_ref, k_ref, v_ref, o_ref, *, tq):
    rows = max(tq, SUBLANES)
    q_all = q_ref[0].astype(F32) * (DH_MEM ** -0.5)
    if rows > tq:
        q_all = jnp.concatenate([q_all, jnp.zeros((rows - tq, D_MEM), F32)], axis=0)
    q_all = q_all.astype(BF16)
    outs = []
    for h in range(H_MEM):
        lo = h * DH_MEM
        q = q_all[:, lo:lo + DH_MEM]
        k = k_ref[0, :, lo:lo + DH_MEM].astype(BF16)
        v = v_ref[0, :, lo:lo + DH_MEM].astype(BF16)
        z = _dot_nt(q, k)
        p = jnp.exp(z - jnp.max(z, axis=1, keepdims=True))
        outs.append(_dot(p.astype(BF16), v) / jnp.sum(p, axis=1, keepdims=True))
    o_ref[0] = jnp.concatenate(outs, axis=1)[0:tq]


def _mem_attn(q, q_block, k, k_block, v, v_block, tq, name):
    b, t, _ = q.shape
    m = k.shape[1]
    return pl.pallas_call(
        functools.partial(_mem_attn_kernel, tq=tq),
        grid=(b, t // tq),
        in_specs=[
            pl.BlockSpec((1, tq, D_MEM), lambda bi, i: (bi, i, q_block)),
            pl.BlockSpec((1, m, D_MEM), lambda bi, i: (bi, 0, k_block)),
            pl.BlockSpec((1, m, D_MEM), lambda bi, i: (bi, 0, v_block)),
        ],
        out_specs=pl.BlockSpec((1, tq, D_MEM), lambda bi, i: (bi, i, 0)),
        out_shape=jax.ShapeDtypeStruct((b, t, D_MEM), F32),
        compiler_params=_params(("parallel", "parallel")),
        name=name,
    )(q, k, v)


def _pad_rows(x, rows):
    return jnp.concatenate([x, jnp.zeros((rows - x.shape[0], x.shape[1]), x.dtype)], axis=0)


def _query_columns(q, n_heads, width):
    t = q.shape[0]
    rows = jnp.concatenate(
        [jnp.broadcast_to(q[i:i + 1], (SUBLANES, q.shape[1])) for i in range(t)], axis=0)
    r = lax.broadcasted_iota(I32, rows.shape, 0)
    c = lax.broadcasted_iota(I32, rows.shape, 1)
    rows = jnp.where(jnp.right_shift(c, int(math.log2(width))) == (r & (SUBLANES - 1)), rows, 0.0)
    return rows


def _suffix_sum_lanes(x):
    lane = lax.broadcasted_iota(I32, x.shape, 1)
    shift = 1
    while shift < PAGE_SIZE:
        x = x + jnp.where(lane < PAGE_SIZE - shift, pltpu.roll(x, PAGE_SIZE - shift, axis=1), 0.0)
        shift *= 2
    return x


def _head_rows_to_tokens(acc_t, n_tok, width):
    rows = _pad_lanes(acc_t, LANES).T[0:n_tok * SUBLANES]
    rr = lax.broadcasted_iota(I32, rows.shape, 0)
    cc = lax.broadcasted_iota(I32, rows.shape, 1)
    rows = jnp.where(jnp.right_shift(cc, int(math.log2(width))) == (rr & (SUBLANES - 1)), rows, 0.0)
    return rows


def _pad_lanes(x, lanes):
    return jnp.concatenate([x, jnp.zeros((x.shape[0], lanes - x.shape[1]), x.dtype)], axis=1)


def _sb_decode_kernel(pt_ref, q_ref, kn_ref, vn_ref, *refs, n_tok, n_pages, pps):
    kt_refs = refs[:pps]
    vt_refs = refs[pps:2 * pps]
    o_ref, qt_ref, acc_ref, c_ref = refs[2 * pps:]
    j = pl.program_id(1)
    n_rows = n_tok * SUBLANES

    def attend(z_pages, vt, masks):
        drops, suffixes = [], []
        for z, mask in zip(z_pages, masks):
            drop = _softplus(z)
            if mask is not None:
                drop = jnp.where(mask, drop, 0.0)
            suffixes.append(_suffix_sum_lanes(drop))
        carry = c_ref[...]
        weights = []
        for z, suffix, mask in zip(z_pages, suffixes, masks):
            log_w = z - suffix - carry
            if mask is not None:
                log_w = jnp.where(mask, log_w, NEG_INF)
            weights.append(jnp.exp(log_w).astype(BF16))
            carry = carry + suffix[:, 0:1]
        c_ref[...] = carry
        acc_ref[...] += _dot_nt(vt, jnp.concatenate(weights, axis=1))

    @pl.when(j == 0)
    def _():
        q = q_ref[0] * (DH_SB ** -0.5)
        qt_ref[...] = _query_columns(q, H_SB, DH_SB).astype(BF16)
        acc_ref[...] = jnp.zeros_like(acc_ref)
        c_ref[...] = jnp.zeros_like(c_ref)
        kn = _pad_rows(kn_ref[0], PAGE_SIZE).astype(BF16)
        vn_t = _pad_rows(vn_ref[0], PAGE_SIZE).T.astype(BF16)
        row = lax.broadcasted_iota(I32, (n_rows, PAGE_SIZE), 0)
        key = lax.broadcasted_iota(I32, (n_rows, PAGE_SIZE), 1)
        attend([_dot_nt(qt_ref[...], kn)], vn_t, [key < jnp.right_shift(row, 3)])

    kt = jnp.concatenate([ref[0].astype(BF16) for ref in kt_refs], axis=1)
    vt = jnp.concatenate([ref[0].astype(BF16) for ref in vt_refs], axis=1)
    z = _dot(qt_ref[...], kt)
    attend([z[:, p * PAGE_SIZE:(p + 1) * PAGE_SIZE] for p in range(pps)], vt,
           [None] * pps)

    @pl.when(j == n_pages // pps - 1)
    def _():
        rows = _head_rows_to_tokens(acc_ref[...], n_tok, DH_SB)
        o_ref[0] = jnp.sum(rows.reshape(n_tok, SUBLANES, D_SB), axis=1)


def _page_specs(width, n_pages, pps):
    def spec(p):
        return pl.BlockSpec(
            (1, width, PAGE_SIZE),
            lambda bi, j, pt: (pt[bi, n_pages - 1 - (j * pps + p)], 0, 0))
    return [spec(p) for p in range(pps)]


def _sb_decode(q, k_new, v_new, cache_kt, cache_vt, page_table):
    b, t, _ = q.shape
    n_pages = page_table.shape[1]
    pps = SB_PAGES_PER_STEP
    tok = pl.BlockSpec((1, t, D_SB), lambda bi, j, pt: (bi, 0, 0))
    grid_spec = pltpu.PrefetchScalarGridSpec(
        num_scalar_prefetch=1,
        grid=(b, n_pages // pps),
        in_specs=[tok, tok, tok] + _page_specs(D_SB, n_pages, pps) + _page_specs(D_SB, n_pages, pps),
        out_specs=tok,
        scratch_shapes=[
            pltpu.VMEM((t * SUBLANES, D_SB), BF16),
            pltpu.VMEM((D_SB, t * SUBLANES), F32),
            pltpu.VMEM((t * SUBLANES, 1), F32),
        ],
    )
    return pl.pallas_call(
        functools.partial(_sb_decode_kernel, n_tok=t, n_pages=n_pages, pps=pps),
        grid_spec=grid_spec,
        out_shape=jax.ShapeDtypeStruct((b, t, D_SB), F32),
        compiler_params=_params(("parallel", "arbitrary")),
        name="sb_decode",
    )(page_table, q, k_new, v_new, *([cache_kt] * pps), *([cache_vt] * pps))


def _diff_decode_kernel(pt_ref, lam_ref, q_ref, kn_ref, vn_ref, bias_ref, g_ref, *refs,
                        n_tok, n_pages, lam_init, pps):
    kt_refs = refs[:pps]
    vt_refs = refs[pps:2 * pps]
    o_ref, qt_ref, acc_ref, m_ref, l_ref = refs[2 * pps:]
    j = pl.program_id(1)
    half = n_tok * SUBLANES
    scale = DQ_DIFF ** -0.5

    def attend(z, vt):
        m_prev = m_ref[...]
        m_new = jnp.maximum(m_prev, jnp.max(z, axis=1, keepdims=True))
        alpha = jnp.exp(m_prev - m_new)
        p = jnp.exp(z - m_new)
        l_ref[...] = alpha * l_ref[...] + jnp.sum(p, axis=1, keepdims=True)
        acc_ref[...] = (acc_ref[...] * _col_to_row(alpha)[:, 0:2 * half]
                        + _dot_nt(vt, p.astype(BF16)))
        m_ref[...] = m_new

    @pl.when(j == 0)
    def _():
        q = _query_columns(q_ref[0], H_DIFF, 2 * DQ_DIFF)
        lane = lax.broadcasted_iota(I32, q.shape, 1)
        first = (jnp.right_shift(lane, 5) & 1) == 0
        q2 = jnp.concatenate([jnp.where(first, q, 0.0), jnp.where(first, 0.0, q)], axis=0)
        qt_ref[...] = q2.astype(BF16)
        acc_ref[...] = jnp.zeros_like(acc_ref)
        m_ref[...] = jnp.full_like(m_ref, NEG_INF)
        l_ref[...] = jnp.zeros_like(l_ref)
        kn = _pad_rows(kn_ref[0], PAGE_SIZE).astype(BF16)
        vn_t = _pad_rows(vn_ref[0], PAGE_SIZE).T.astype(BF16)
        row = lax.broadcasted_iota(I32, (2 * half, PAGE_SIZE), 0)
        key = lax.broadcasted_iota(I32, (2 * half, PAGE_SIZE), 1)
        z_new = _dot_nt(qt_ref[...], kn) * scale + bias_ref[1, 0:2 * half]
        attend(jnp.where(key <= (jnp.right_shift(row, 3) & (n_tok - 1)), z_new, NEG_INF), vn_t)

    kt = jnp.concatenate([ref[0].astype(BF16) for ref in kt_refs], axis=1)
    vt = jnp.concatenate([ref[0].astype(BF16) for ref in vt_refs], axis=1)
    far = bias_ref[2, 0:2 * half]
    near = jnp.where(j == 0, bias_ref[0, 0:2 * half], far)
    bias = jnp.concatenate([near] + [far] * (pps - 1), axis=1)
    attend(_dot(qt_ref[...], kt) * scale + bias, vt)

    @pl.when(j == n_pages // pps - 1)
    def _():
        rows = _pad_lanes(acc_ref[...], LANES).T[0:2 * half] / l_ref[...]
        o = rows[0:half] - lam_ref[0, 0] * rows[half:2 * half]
        rr = lax.broadcasted_iota(I32, o.shape, 0)
        cc = lax.broadcasted_iota(I32, o.shape, 1)
        o = jnp.where(jnp.right_shift(cc, 6) == (rr & (SUBLANES - 1)), o, 0.0)
        ms = jnp.sum(o * o, axis=1, keepdims=True) * (1.0 / DV_DIFF)
        o = (o * lax.rsqrt(ms + EPS)) * g_ref[...] * (1.0 - lam_init)
        o_ref[0] = jnp.sum(o.reshape(n_tok, SUBLANES, D_DIFF), axis=1)


def _diff_decode(q, k_new, v_new, cache_kt, cache_vt, page_table, lam, dec_bias, g_tiled, lam_init):
    b, t, _ = q.shape
    n_pages = page_table.shape[1]
    pps = DIFF_PAGES_PER_STEP
    tok = pl.BlockSpec((1, t, D_DIFF), lambda bi, j, pt: (bi, 0, 0))
    grid_spec = pltpu.PrefetchScalarGridSpec(
        num_scalar_prefetch=1,
        grid=(b, n_pages // pps),
        in_specs=[
            pl.BlockSpec(memory_space=pltpu.SMEM),
            tok, tok, tok,
            pl.BlockSpec((3, LANES, PAGE_SIZE), lambda bi, j, pt: (0, 0, 0)),
            pl.BlockSpec((1, D_DIFF), lambda bi, j, pt: (0, 0)),
        ] + _page_specs(D_QK_DIFF, n_pages, pps) + _page_specs(D_DIFF, n_pages, pps),
        out_specs=tok,
        scratch_shapes=[
            pltpu.VMEM((2 * t * SUBLANES, D_QK_DIFF), BF16),
            pltpu.VMEM((D_DIFF, 2 * t * SUBLANES), F32),
            pltpu.VMEM((2 * t * SUBLANES, 1), F32),
            pltpu.VMEM((2 * t * SUBLANES, 1), F32),
        ],
    )
    return pl.pallas_call(
        functools.partial(_diff_decode_kernel, n_tok=t, n_pages=n_pages, lam_init=lam_init, pps=pps),
        grid_spec=grid_spec,
        out_shape=jax.ShapeDtypeStruct((b, t, D_DIFF), F32),
        compiler_params=_params(("parallel", "arbitrary")),
        name="diff_decode",
    )(page_table, lam, q, k_new, v_new, dec_bias, g_tiled,
      *([cache_kt] * pps), *([cache_vt] * pps))


def _finish_kernel(x_ref, osb_ref, odf_ref, omem_ref, gmix_ref, wg_ref, bg_ref, wsb_ref, wdf_ref,
                   wmem_ref, wout_ref, gffn_ref, wr_ref, br_ref, xmid_ref, xn_ref, route_ref, count_ref,
                   carry_ref):
    x = x_ref[...]
    u = _rmsnorm(x, gmix_ref[...]).astype(BF16)
    gates = jax.nn.sigmoid(_dot(u, wg_ref[...]) + bg_ref[...])
    h = (gates[:, 0:D_MODEL] * _dot(osb_ref[...].astype(BF16), wsb_ref[...])
         + gates[:, D_MODEL:2 * D_MODEL] * _dot(odf_ref[...].astype(BF16), wdf_ref[...])
         + gates[:, 2 * D_MODEL:3 * D_MODEL] * _dot(omem_ref[...].astype(BF16), wmem_ref[...]))
    xm = x + _dot(h.astype(BF16), wout_ref[...])
    xmid_ref[...] = xm
    xn = _rmsnorm(xm, gffn_ref[...])
    xn_ref[...] = xn
    xh, xl = _split_bf16(xn)
    wh, wl = _split_bf16(wr_ref[...])
    lg = _dot(xh, wh) + _dot(xh, wl) + _dot(xl, wh) + br_ref[...]
    lane = lax.broadcasted_iota(I32, lg.shape, 1)
    is_group = lane < N_GROUPS
    gl = jnp.where(is_group, lg, -jnp.inf)
    gmax = jnp.max(gl, axis=1, keepdims=True)
    grp = jnp.min(jnp.where(gl == gmax, lane, LANES), axis=1, keepdims=True)
    p_grp = 1.0 / jnp.sum(jnp.where(is_group, jnp.exp(gl - gmax), 0.0), axis=1, keepdims=True)
    in_group = (lane >= N_GROUPS) & (lane < N_GROUPS + N_EXPERTS) & (
        jnp.right_shift(lane - N_GROUPS, 3) == grp)
    el = jnp.where(in_group, lg, -jnp.inf)
    v1 = jnp.max(el, axis=1, keepdims=True)
    i1 = jnp.min(jnp.where(el == v1, lane, LANES), axis=1, keepdims=True)
    el2 = jnp.where(lane == i1, -jnp.inf, el)
    v2 = jnp.max(el2, axis=1, keepdims=True)
    i2 = jnp.min(jnp.where(el2 == v2, lane, LANES), axis=1, keepdims=True)
    e = jnp.exp(v2 - v1)
    w1 = (1.0 / (1.0 + e)) * p_grp
    w2 = (e / (1.0 + e)) * p_grp
    @pl.when(pl.program_id(0) == 0)
    def _():
        carry_ref[...] = jnp.zeros_like(carry_ref)

    tm = lg.shape[0]
    oh1 = lane == i1 - N_GROUPS
    oh2 = lane == i2 - N_GROUPS
    both = jnp.where(oh1 | oh2, 1.0, 0.0)
    r = lax.broadcasted_iota(I32, (tm, tm), 0)
    c = lax.broadcasted_iota(I32, (tm, tm), 1)
    before = _dot((c < r).astype(BF16), both.astype(BF16)) + carry_ref[...]
    rank1 = jnp.sum(jnp.where(oh1, before, 0.0), axis=1, keepdims=True)
    rank2 = jnp.sum(jnp.where(oh2, before, 0.0), axis=1, keepdims=True)
    carry_ref[...] += jnp.sum(both, axis=0, keepdims=True)
    count_ref[...] = jnp.broadcast_to(carry_ref[...], count_ref.shape)

    rec = jnp.where(lane == R_EID1, (i1 - N_GROUPS).astype(F32), 0.0)
    rec = jnp.where(lane == R_EID2, (i2 - N_GROUPS).astype(F32), rec)
    rec = jnp.where(lane == R_W1, w1, rec)
    rec = jnp.where(lane == R_W2, w2, rec)
    rec = jnp.where(lane == R_RANK1, rank1, rec)
    rec = jnp.where(lane == R_RANK2, rank2, rec)
    route_ref[...] = rec


def _finish(x, o_sb, o_df, o_mem, g_mix, wg, bg, wsb, wdf, wmem, wout, g_ffn, wr, br, name):
    n, d = x.shape
    tm = min(TOK_TILE, n)

    def rows(width):
        return pl.BlockSpec((tm, width), lambda i: (i, 0))

    def whole(a):
        return pl.BlockSpec(a.shape, lambda i: (0, 0))

    args = (x, o_sb, o_df, o_mem, g_mix.reshape(1, d), wg, bg.reshape(1, -1), wsb, wdf, wmem, wout,
            g_ffn.reshape(1, d), wr, br)
    in_specs = [rows(d), rows(D_SB), rows(D_DIFF), rows(D_MEM)] + [whole(a) for a in args[4:]]
    return pl.pallas_call(
        _finish_kernel,
        grid=(n // tm,),
        in_specs=in_specs,
        out_specs=[rows(d), rows(d), rows(LANES), pl.BlockSpec((SUBLANES, LANES), lambda i: (0, 0))],
        out_shape=[
            jax.ShapeDtypeStruct((n, d), F32),
            jax.ShapeDtypeStruct((n, d), F32),
            jax.ShapeDtypeStruct((n, LANES), F32),
            jax.ShapeDtypeStruct((SUBLANES, LANES), F32),
        ],
        scratch_shapes=[pltpu.VMEM((1, LANES), F32)],
        compiler_params=_params(("arbitrary",)),
        name=name,
    )(*args)


def _one_hots(route):
    lane = lax.broadcasted_iota(I32, route.shape, 1)
    oh1 = lane == route[:, R_EID1:R_EID1 + 1].astype(I32)
    oh2 = lane == route[:, R_EID2:R_EID2 + 1].astype(I32)
    return oh1, oh2


def _moe_dest_kernel(route_ref, count_ref, dest_ref, blk_ref):
    n_blk = blk_ref.shape[0]
    blocks = jnp.right_shift(count_ref[...].astype(I32) + (MOE_ROWS - 1), MOE_ROWS_LOG2).astype(F32)
    r = lax.broadcasted_iota(I32, (LANES, LANES), 0)
    c = lax.broadcasted_iota(I32, (LANES, LANES), 1)
    upto = (r <= c).astype(BF16)
    block_end = _dot(blocks.astype(BF16), upto)
    row_start = (block_end - blocks)[0:1] * float(MOE_ROWS)
    route = route_ref[...]
    oh1, oh2 = _one_hots(route)
    d1 = jnp.sum(jnp.where(oh1, row_start, 0.0), axis=1, keepdims=True) + route[:, R_RANK1:R_RANK1 + 1]
    d2 = jnp.sum(jnp.where(oh2, row_start, 0.0), axis=1, keepdims=True) + route[:, R_RANK2:R_RANK2 + 1]
    lane = lax.broadcasted_iota(I32, route.shape, 1)
    dest_ref[...] = jnp.where(lane == 0, d1, jnp.where(lane == 1, d2, 0.0)).astype(I32)

    @pl.when(pl.program_id(0) == 0)
    def _():
        b_idx = lax.broadcasted_iota(I32, (n_blk, LANES), 0).astype(F32)
        lane_b = lax.broadcasted_iota(I32, (n_blk, LANES), 1)
        done = (block_end[0:1] <= b_idx) & (lane_b < N_EXPERTS)
        expert = jnp.minimum(jnp.sum(jnp.where(done, 1.0, 0.0), axis=1, keepdims=True),
                             float(N_EXPERTS - 1))
        used = block_end[0:1, N_EXPERTS - 1:N_EXPERTS]
        blk_ref[...] = jnp.where(lane_b == 0, expert, jnp.where(lane_b == 1, used, 0.0)).astype(I32)


def _moe_dest(route, counts, n_blk):
    n = route.shape[0]
    tm = min(TOK_TILE, n)
    n_blk_pad = -(-n_blk // SUBLANES) * SUBLANES
    return pl.pallas_call(
        _moe_dest_kernel,
        grid=(n // tm,),
        in_specs=[pl.BlockSpec((tm, LANES), lambda i: (i, 0)),
                  pl.BlockSpec((SUBLANES, LANES), lambda i: (0, 0))],
        out_specs=[pl.BlockSpec((tm, LANES), lambda i: (i, 0)),
                   pl.BlockSpec((n_blk_pad, LANES), lambda i: (0, 0))],
        out_shape=[jax.ShapeDtypeStruct((n, LANES), I32),
                   jax.ShapeDtypeStruct((n_blk_pad, LANES), I32)],
        compiler_params=_params(("arbitrary",)),
        name="moe_dest",
    )(route, counts)


def _dispatch_kernel(dest_ref, x_ref, init_ref, xs_ref, sem):
    del init_ref
    tm = x_ref.shape[0]

    def row_copy(t, d):
        return pltpu.make_async_copy(x_ref.at[pl.ds(t, 1)], xs_ref.at[pl.ds(d, 1)], sem)

    def start(t, carry):
        row_copy(t, dest_ref[0, 0, 2 * t]).start(priority=0)
        row_copy(t, dest_ref[0, 0, 2 * t + 1]).start(priority=1)
        return carry

    def wait(t, carry):
        row_copy(t, dest_ref[0, 0, 2 * t]).wait()
        row_copy(t, dest_ref[0, 0, 2 * t + 1]).wait()
        return carry

    lax.fori_loop(0, tm, start, 0, unroll=ROW_DMA_UNROLL)
    lax.fori_loop(0, tm, wait, 0, unroll=ROW_DMA_UNROLL)


def _dispatch(dest_tiles, xn, n_rows):
    n, d = xn.shape
    tm = dest_tiles.shape[2] // 2
    return pl.pallas_call(
        _dispatch_kernel,
        grid=(n // tm,),
        in_specs=[
            pl.BlockSpec((1, 1, 2 * tm), lambda i: (i, 0, 0), memory_space=pltpu.SMEM),
            pl.BlockSpec((tm, d), lambda i: (i, 0)),
            pl.BlockSpec(memory_space=pl.ANY),
        ],
        out_specs=pl.BlockSpec(memory_space=pl.ANY),
        out_shape=jax.ShapeDtypeStruct((n_rows, d), F32),
        scratch_shapes=[pltpu.SemaphoreType.DMA(())],
        input_output_aliases={2: 0},
        compiler_params=_params(("arbitrary",)),
        name="moe_dispatch",
    )(dest_tiles, xn, jnp.zeros((n_rows, d), F32))


def _expert_kernel(be_ref, used_ref, xs_ref, wg_ref, wu_ref, wd_ref, y_ref, wg_b, wu_b, wd_b):
    b = pl.program_id(0)

    @pl.when(b < used_ref[0])
    def _():
        @pl.when((b == 0) | (be_ref[b] != be_ref[jnp.maximum(b - 1, 0)]))
        def _():
            wg_b[...] = wg_ref[0].astype(BF16)
            wu_b[...] = wu_ref[0].astype(BF16)
            wd_b[...] = wd_ref[0].astype(BF16)

        x = xs_ref[...].astype(BF16)
        h = jax.nn.silu(_dot(x, wg_b[...])) * _dot(x, wu_b[...])
        y_ref[...] = _dot(h.astype(BF16), wd_b[...])

    @pl.when(b >= used_ref[0])
    def _():
        y_ref[...] = jnp.zeros_like(y_ref)


def _experts(blk_exp, used, xs, wg, wu, wd):
    n_rows, d = xs.shape
    n_blk = n_rows // MOE_ROWS
    grid_spec = pltpu.PrefetchScalarGridSpec(
        num_scalar_prefetch=2,
        grid=(n_blk,),
        in_specs=[
            pl.BlockSpec((MOE_ROWS, d), lambda b, be, used: (b, 0)),
            pl.BlockSpec((1, d, D_EXPERT), lambda b, be, used: (be[b], 0, 0)),
            pl.BlockSpec((1, d, D_EXPERT), lambda b, be, used: (be[b], 0, 0)),
            pl.BlockSpec((1, D_EXPERT, d), lambda b, be, used: (be[b], 0, 0)),
        ],
        out_specs=pl.BlockSpec((MOE_ROWS, d), lambda b, be, used: (b, 0)),
        scratch_shapes=[pltpu.VMEM((d, D_EXPERT), BF16), pltpu.VMEM((d, D_EXPERT), BF16),
                        pltpu.VMEM((D_EXPERT, d), BF16)],
    )
    return pl.pallas_call(
        _expert_kernel,
        grid_spec=grid_spec,
        out_shape=jax.ShapeDtypeStruct((n_rows, d), F32),
        compiler_params=_params(("arbitrary",)),
        name="moe_experts",
    )(blk_exp, used, xs, wg, wu, wd)


def _combine_kernel(dest_ref, route_ref, xmid_ref, g_ref, yb_ref, out_ref, buf_ref, sem):
    tm = xmid_ref.shape[0]

    def row_copy(t, k, d):
        return pltpu.make_async_copy(yb_ref.at[pl.ds(d, 1)], buf_ref.at[k, pl.ds(t, 1)], sem)

    def start(t, carry):
        row_copy(t, 0, dest_ref[0, 0, 2 * t]).start(priority=0)
        row_copy(t, 1, dest_ref[0, 0, 2 * t + 1]).start(priority=1)
        return carry

    def wait(t, carry):
        row_copy(t, 0, dest_ref[0, 0, 2 * t]).wait()
        row_copy(t, 1, dest_ref[0, 0, 2 * t + 1]).wait()
        return carry

    lax.fori_loop(0, tm, start, 0, unroll=ROW_DMA_UNROLL)
    lax.fori_loop(0, tm, wait, 0, unroll=ROW_DMA_UNROLL)
    route = route_ref[...]
    y = buf_ref[0] * route[:, R_W1:R_W1 + 1] + buf_ref[1] * route[:, R_W2:R_W2 + 1]
    out_ref[...] = _rmsnorm(xmid_ref[...] + y, g_ref[...])


def _combine(dest_tiles, route, xmid, g_final, yb):
    n, d = xmid.shape
    tm = dest_tiles.shape[2] // 2
    return pl.pallas_call(
        _combine_kernel,
        grid=(n // tm,),
        in_specs=[
            pl.BlockSpec((1, 1, 2 * tm), lambda i: (i, 0, 0), memory_space=pltpu.SMEM),
            pl.BlockSpec((tm, LANES), lambda i: (i, 0)),
            pl.BlockSpec((tm, d), lambda i: (i, 0)),
            pl.BlockSpec((1, d), lambda i: (0, 0)),
            pl.BlockSpec(memory_space=pl.ANY),
        ],
        out_specs=pl.BlockSpec((tm, d), lambda i: (i, 0)),
        out_shape=jax.ShapeDtypeStruct((n, d), F32),
        scratch_shapes=[pltpu.VMEM((2, tm, d), F32), pltpu.SemaphoreType.DMA(())],
        compiler_params=_params(("arbitrary",)),
        name="moe_combine",
    )(dest_tiles, route, xmid, g_final.reshape(1, d), yb)


def _moe_and_final_norm(xmid, xn, route, counts, wg, wu, wd, g_final):
    n, d = xmid.shape
    n_blk = -(-(2 * n + N_EXPERTS * (MOE_ROWS - 1)) // MOE_ROWS)
    dest, blk = _moe_dest(route, counts, n_blk)
    tm = min(ROW_TILE, n)
    dest_tiles = dest[:, 0:2].reshape(n // tm, 1, 2 * tm)
    xs = _dispatch(dest_tiles, xn, n_blk * MOE_ROWS)
    yb = _experts(blk[0:n_blk, 0], blk[0, 1:2], xs, wg, wu, wd)
    return _combine(dest_tiles, route, xmid, g_final, yb)


def _col_scale():
    s = jnp.ones((1, D_IN), F32)
    return s.at[:, OFF_Q_SB:OFF_Q_SB + D_SB].set(DH_SB ** -0.5)


def kernel(x_prompt, x_sample, cache_sb_k, cache_sb_v, cache_diff_k, cache_diff_v, cache_mem_k, cache_mem_v, page_table, mem_prompt, norm_mix_g, w_in, diff_lam_q1, diff_lam_k1, diff_lam_q2, diff_lam_k2, diff_norm_g, mem_norm_g, w_mem_kv, w_gate, b_gate, w_br_sb, w_br_diff, w_br_mem, w_out, norm_ffn_g, w_router_group, b_router_group, w_router_expert, b_router_expert, w_exp_gate, w_exp_up, w_exp_down, rel_bias, norm_final_g):
    depth = w_in.shape[0]
    assert depth == 1, "single-layer stack only"
    assert page_table.shape[1] % SB_PAGES_PER_STEP == 0 and page_table.shape[1] % DIFF_PAGES_PER_STEP == 0
    b, t, d = x_prompt.shape
    bs, ts, _ = x_sample.shape
    n_mem = mem_prompt.shape[1]
    sb_tq, diff_blk = min(SB_QUERY_BLOCK, t), min(DIFF_BLOCK, t)
    lam_init = 0.8 - 0.6 * math.exp(-0.3 * 0)

    w_in_b = w_in[0].astype(BF16)
    wg_b, wsb_b, wdf_b = w_gate[0].astype(BF16), w_br_sb[0].astype(BF16), w_br_diff[0].astype(BF16)
    wmem_b, wout_b = w_br_mem[0].astype(BF16), w_out[0].astype(BF16)
    weg_b, weu_b, wed_b = w_exp_gate[0], w_exp_up[0], w_exp_down[0]
    pad = LANES - N_GROUPS - N_EXPERTS
    w_router = jnp.concatenate(
        [w_router_group[0], w_router_expert[0], jnp.zeros((d, pad), F32)], axis=1)
    b_router = jnp.concatenate(
        [b_router_group[0], b_router_expert[0], jnp.zeros((pad,), F32)]).reshape(1, LANES)
    lam_vecs = jnp.concatenate([diff_lam_q1, diff_lam_k1, diff_lam_q2, diff_lam_k2], axis=0)

    tz, dec_bias, lam_tile = _prep(rel_bias, lam_vecs, diff_blk, lam_init)
    lam = lam_tile[0:1, 0:1]

    def heads_last(a_t, n_heads, width):
        bb, _, tt = a_t.shape
        return jnp.transpose(a_t.reshape(bb, n_heads, width, tt), (0, 3, 1, 2))[None]

    def keys_last(cache, n_heads, width):
        pool, page = cache.shape[1], cache.shape[2]
        return jnp.transpose(cache[0], (0, 2, 3, 1)).reshape(pool, n_heads * width, page)

    k_sb, v_sb, k_df, v_df, pb = _norm_proj(
        x_prompt, norm_mix_g[0], w_in_b, _col_scale(),
        [(OFF_K_SB, D_SB), (OFF_V_SB, D_SB), (OFF_K_DF, D_QK_DIFF), (OFF_V_DF, D_DIFF)], True,
        "proj_prompt")
    mk, mv, mem_b = _norm_proj(
        mem_prompt, mem_norm_g[0], w_mem_kv[0].astype(BF16),
        jnp.ones((1, 2 * D_MEM), F32), [(0, D_MEM), (D_MEM, D_MEM)], True, "proj_memory")
    o_sb = _sb_prompt(pb, sb_tq, SB_KEY_BLOCK)
    o_df = _diff_prompt(pb, tz, lam, rel_bias, diff_norm_g[0], diff_blk, lam_init)
    o_mem = _mem_attn(pb, OFF_Q_MEM // D_MEM, mem_b, 0, mem_b, 1, MEM_QUERY_BLOCK, "mem_prompt")
    xmid, xn, route, counts = _finish(
        x_prompt.reshape(b * t, d), o_sb.reshape(b * t, D_SB), o_df.reshape(b * t, D_DIFF),
        o_mem.reshape(b * t, D_MEM), norm_mix_g[0], wg_b, b_gate[0], wsb_b, wdf_b, wmem_b, wout_b,
        norm_ffn_g[0], w_router, b_router, "finish_prompt")
    y_prompt = _moe_and_final_norm(xmid, xn, route, counts, weg_b, weu_b, wed_b, norm_final_g)

    q_sb_s, k_sb_s, v_sb_s, q_df_s, k_df_s, v_df_s, q_mem_s, _ = _norm_proj(
        x_sample.reshape(1, bs * ts, d), norm_mix_g[0], w_in_b, jnp.ones((1, D_IN), F32),
        [(OFF_Q_SB, D_SB), (OFF_K_SB, D_SB), (OFF_V_SB, D_SB), (OFF_Q_DF, D_QK_DIFF),
         (OFF_K_DF, D_QK_DIFF), (OFF_V_DF, D_DIFF), (OFF_Q_MEM, D_MEM)], False, "proj_sample")
    o_sb_s = _sb_decode(
        q_sb_s.reshape(bs, ts, D_SB), k_sb_s.reshape(bs, ts, D_SB), v_sb_s.reshape(bs, ts, D_SB),
        keys_last(cache_sb_k, H_SB, DH_SB), keys_last(cache_sb_v, H_SB, DH_SB), page_table)
    o_df_s = _diff_decode(
        q_df_s.reshape(bs, ts, D_QK_DIFF), k_df_s.reshape(bs, ts, D_QK_DIFF),
        v_df_s.reshape(bs, ts, D_DIFF), keys_last(cache_diff_k, H_DIFF, 2 * DQ_DIFF),
        keys_last(cache_diff_v, H_DIFF, DV_DIFF), page_table, lam, dec_bias,
        jnp.tile(diff_norm_g[0], H_DIFF).reshape(1, D_DIFF), lam_init)
    o_mem_s = _mem_attn(
        q_mem_s.reshape(bs, ts, D_MEM), 0, cache_mem_k[0].reshape(bs, n_mem, D_MEM), 0,
        cache_mem_v[0].reshape(bs, n_mem, D_MEM), 0, ts, "mem_sample")
    xmid_s, xn_s, route_s, counts_s = _finish(
        x_sample.reshape(bs * ts, d), o_sb_s.reshape(bs * ts, D_SB), o_df_s.reshape(bs * ts, D_DIFF),
        o_mem_s.reshape(bs * ts, D_MEM), norm_mix_g[0], wg_b, b_gate[0], wsb_b, wdf_b, wmem_b, wout_b,
        norm_ffn_g[0], w_router, b_router, "finish_sample")
    y_sample = _moe_and_final_norm(xmid_s, xn_s, route_s, counts_s, weg_b, weu_b, wed_b, norm_final_g)

    return (y_prompt.reshape(b, t, d), y_sample.reshape(bs, ts, d),
            heads_last(k_sb, H_SB, DH_SB), heads_last(v_sb, H_SB, DH_SB),
            heads_last(k_df, H_DIFF, 2 * DQ_DIFF), heads_last(v_df, H_DIFF, DV_DIFF),
            heads_last(mk, H_MEM, DH_MEM), heads_last(mv, H_MEM, DH_MEM),
            k_sb_s.reshape(1, bs, ts, H_SB, DH_SB), v_sb_s.reshape(1, bs, ts, H_SB, DH_SB),
            k_df_s.reshape(1, bs, ts, H_DIFF, 2 * DQ_DIFF), v_df_s.reshape(1, bs, ts, H_DIFF, DV_DIFF))
```

```python
import functools
import math

import jax
import jax.numpy as jnp
from jax import lax
from jax.experimental import pallas as pl
from jax.experimental.pallas import tpu as pltpu

F32 = jnp.float32
BF16 = jnp.bfloat16
I32 = jnp.int32

D_MODEL = 1024
PAGE_SIZE = 128
H_SB = 8
DH_SB = 64
H_DIFF = 4
DQ_DIFF = 32
DV_DIFF = 64
H_MEM = 4
DH_MEM = 64
D_SB = H_SB * DH_SB
D_DIFF = H_DIFF * DV_DIFF
D_MEM = H_MEM * DH_MEM
D_QK_DIFF = H_DIFF * 2 * DQ_DIFF
D_IN = 3 * D_SB + 2 * D_QK_DIFF + D_DIFF + D_MEM
N_BUCKETS = 32
MAX_EXACT = 16
MAX_DISTANCE = 128
N_GROUPS = 4
EXPERTS_PER_GROUP = 8
N_EXPERTS = N_GROUPS * EXPERTS_PER_GROUP
D_EXPERT = 512
EPS = 1e-6
NEG_INF = -1e30

LANES = 128
SUBLANES = 8
VMEM_LIMIT = 48 * 1024 * 1024
DIFF_VMEM_LIMIT = 56 * 1024 * 1024

OFF_Q_SB = 0
OFF_K_SB = D_SB
OFF_V_SB = 2 * D_SB
OFF_Q_DF = 3 * D_SB
OFF_K_DF = OFF_Q_DF + D_QK_DIFF
OFF_V_DF = OFF_K_DF + D_QK_DIFF
OFF_Q_MEM = OFF_V_DF + D_DIFF

MEM_QUERY_BLOCK = 1024
SB_QUERY_BLOCK = 2048
SB_KEY_BLOCK = 256
DIFF_BLOCK = 512
MOE_ROWS = 256
MOE_ROWS_LOG2 = 8
TOK_TILE = 256
FINISH_TILE = 512
DEST_TILE = 1024
ROW_TILE = 128
ROW_DMA_UNROLL = 32
SB_PAGES_PER_STEP = 32
DIFF_PAGES_PER_STEP = 64

R_EID1, R_EID2, R_W1, R_W2, R_RANK1, R_RANK2 = 0, 1, 2, 3, 4, 5


def _params(sem, vmem=VMEM_LIMIT):
    return pltpu.CompilerParams(dimension_semantics=sem, vmem_limit_bytes=vmem)


def _rmsnorm(x, g):
    ms = jnp.mean(x * x, axis=-1, keepdims=True)
    return (x * lax.rsqrt(ms + EPS)) * g


def _dot(a, b):
    return jnp.dot(a, b, preferred_element_type=F32)


def _dot_nt(a, b):
    return lax.dot_general(a, b, (((1,), (1,)), ((), ())), preferred_element_type=F32)


def _split_bf16(x):
    hi = x.astype(BF16)
    lo = (x - hi.astype(F32)).astype(BF16)
    return hi, lo


def _softplus(z):
    neg_abs = pltpu.bitcast(pltpu.bitcast(z, jnp.uint32) | jnp.uint32(0x80000000), F32)
    return jnp.maximum(z, 0.0) + jnp.log(1.0 + jnp.exp(neg_abs))


def _col_to_row(v):
    n = v.shape[0]
    r = lax.broadcasted_iota(I32, (n, LANES), 0)
    c = lax.broadcasted_iota(I32, (n, LANES), 1)
    return jnp.sum(jnp.where(r == c, v, 0.0), axis=0, keepdims=True)


def _norm_proj_kernel(x_ref, g_ref, w_ref, s_ref, *out_refs, f32_cols, transposed):
    u = _rmsnorm(x_ref[0], g_ref[...]).astype(BF16)
    p = _dot(u, w_ref[...])
    for ref, (lo, width) in zip(out_refs[:-1], f32_cols):
        ref[0] = p[:, lo:lo + width].T if transposed else p[:, lo:lo + width]
    out_refs[-1][0] = (p * s_ref[...]).astype(BF16)


def _norm_proj(x, g, w_bf16, col_scale, f32_cols, transposed, name):
    b, t, d = x.shape
    n_out = w_bf16.shape[1]
    tm = min(TOK_TILE, t)
    if transposed:
        out_shape = [jax.ShapeDtypeStruct((b, width, t), F32) for _, width in f32_cols]
        out_specs = [pl.BlockSpec((1, width, tm), lambda bi, i: (bi, 0, i)) for _, width in f32_cols]
    else:
        out_shape = [jax.ShapeDtypeStruct((b, t, width), F32) for _, width in f32_cols]
        out_specs = [pl.BlockSpec((1, tm, width), lambda bi, i: (bi, i, 0)) for _, width in f32_cols]
    out_shape.append(jax.ShapeDtypeStruct((b, t, n_out), BF16))
    out_specs.append(pl.BlockSpec((1, tm, n_out), lambda bi, i: (bi, i, 0)))
    return pl.pallas_call(
        functools.partial(_norm_proj_kernel, f32_cols=tuple(f32_cols), transposed=transposed),
        grid=(b, t // tm),
        in_specs=[
            pl.BlockSpec((1, tm, d), lambda bi, i: (bi, i, 0)),
            pl.BlockSpec((1, d), lambda bi, i: (0, 0)),
            pl.BlockSpec((d, n_out), lambda bi, i: (0, 0)),
            pl.BlockSpec((1, n_out), lambda bi, i: (0, 0)),
        ],
        out_specs=out_specs,
        out_shape=out_shape,
        compiler_params=_params(("parallel", "parallel")),
        name=name,
    )(x, g.reshape(1, d), w_bf16, col_scale)


def _t5_bucket(delta):
    n = jnp.maximum(delta, 0)
    nf = jnp.maximum(n, 1).astype(F32)
    large = MAX_EXACT + (jnp.log(nf / MAX_EXACT) / math.log(MAX_DISTANCE / MAX_EXACT)
                         * (N_BUCKETS - MAX_EXACT)).astype(I32)
    large = jnp.minimum(large, N_BUCKETS - 1)
    return jnp.where(n < MAX_EXACT, n, large)


def _bias_of_bucket(bucket, rel_ref, head):
    out = jnp.zeros(bucket.shape, F32)
    for b in range(N_BUCKETS):
        out = jnp.where(bucket == b, rel_ref[b, head], out)
    return out


def _bias_by_head(bucket, head, rel_ref):
    acc = jnp.zeros(bucket.shape, F32)
    for h in range(H_DIFF):
        acc = jnp.where(head == h, _bias_of_bucket(bucket, rel_ref, h), acc)
    return acc


def _prep_kernel(rel_ref, lam_ref, tz_ref, dec_ref, lam_out_ref, *, blk, lam_init):
    r = lax.broadcasted_iota(I32, (blk, blk), 0)
    c = lax.broadcasted_iota(I32, (blk, blk), 1)
    for off in range(2):
        bucket = _t5_bucket(r - c + off * blk)
        for h in range(H_DIFF):
            tz_ref[h, off] = _bias_of_bucket(bucket, rel_ref, h)
    key = lax.broadcasted_iota(I32, (LANES, PAGE_SIZE), 1)
    row = lax.broadcasted_iota(I32, (LANES, PAGE_SIZE), 0)
    qi = jnp.right_shift(row, 3) & 3
    head = row & (SUBLANES - 1)
    dec_ref[0] = _bias_by_head(_t5_bucket(PAGE_SIZE + qi - key), head, rel_ref)
    dec_ref[1] = _bias_by_head(_t5_bucket(qi - key), head, rel_ref)
    dec_ref[2] = _bias_by_head(jnp.full((LANES, PAGE_SIZE), N_BUCKETS - 1, I32), head, rel_ref)
    lq1, lk1, lq2, lk2 = lam_ref[0:1, :], lam_ref[1:2, :], lam_ref[2:3, :], lam_ref[3:4, :]
    lam = (jnp.exp(jnp.sum(lq1 * lk1, axis=-1, keepdims=True))
           - jnp.exp(jnp.sum(lq2 * lk2, axis=-1, keepdims=True)) + lam_init)
    lam_out_ref[...] = jnp.broadcast_to(lam, (SUBLANES, LANES))


def _prep(rel_bias, lam_vecs, blk, lam_init):
    return pl.pallas_call(
        functools.partial(_prep_kernel, blk=blk, lam_init=lam_init),
        in_specs=[
            pl.BlockSpec(memory_space=pltpu.SMEM),
            pl.BlockSpec(memory_space=pltpu.VMEM),
        ],
        out_specs=[pl.BlockSpec(memory_space=pltpu.VMEM)] * 3,
        out_shape=[
            jax.ShapeDtypeStruct((H_DIFF, 2, blk, blk), F32),
            jax.ShapeDtypeStruct((3, PAGE_SIZE, LANES), F32),
            jax.ShapeDtypeStruct((SUBLANES, LANES), F32),
        ],
        name="prep_bias_lambda",
    )(rel_bias, lam_vecs)


def _sb_prompt_kernel(q_ref, k_ref, v_ref, o_ref, acc_ref, c_ref, *, tq, tk):
    i = pl.program_id(2)
    n_diag = tq // tk
    r = lax.broadcasted_iota(I32, (tk, tk), 0)
    c = lax.broadcasted_iota(I32, (tk, tk), 1)
    tri = (r >= c).astype(BF16)

    def block(j, row0, diag):
        start = pl.multiple_of(j * tk, tk)
        rows = tq - row0
        if diag:
            rr = lax.broadcasted_iota(I32, (rows, tk), 0)
            cc = lax.broadcasted_iota(I32, (rows, tk), 1)
            strict = cc < rr
        for hh in range(2):
            lo = hh * DH_SB
            q = q_ref[0, row0:tq, lo:lo + DH_SB]
            k = k_ref[0, pl.ds(start, tk), lo:lo + DH_SB]
            v = v_ref[0, pl.ds(start, tk), lo:lo + DH_SB]
            z = _dot_nt(q, k)
            drop = _softplus(z)
            if diag:
                drop = jnp.where(strict, drop, 0.0)
            suffix = _dot(drop.astype(BF16), tri)
            log_w = z - suffix - c_ref[hh, row0:tq]
            if diag:
                log_w = jnp.where(strict, log_w, NEG_INF)
            a = jnp.exp(log_w)
            acc_ref[hh, row0:tq] += _dot(a.astype(BF16), v)
            c_ref[hh, row0:tq] += suffix[:, 0:1]

    acc_ref[...] = jnp.zeros_like(acc_ref)
    c_ref[...] = jnp.zeros_like(c_ref)
    for dd in range(n_diag - 1, -1, -1):
        block(i * n_diag + dd, dd * tk, True)

    def body(kk, carry):
        block(i * n_diag - 1 - kk, 0, False)
        return carry

    lax.fori_loop(0, i * n_diag, body, 0)
    o_ref[0] = jnp.concatenate([acc_ref[0], acc_ref[1]], axis=1).astype(BF16)


def _sb_prompt(pb, tq, tk):
    b, t, _ = pb.shape
    qb, kb, vb = OFF_Q_SB // LANES, OFF_K_SB // LANES, OFF_V_SB // LANES
    return pl.pallas_call(
        functools.partial(_sb_prompt_kernel, tq=tq, tk=tk),
        grid=(b, H_SB // 2, t // tq),
        in_specs=[
            pl.BlockSpec((1, tq, LANES), lambda bi, hp, i: (bi, i, qb + hp)),
            pl.BlockSpec((1, t, LANES), lambda bi, hp, i: (bi, 0, kb + hp)),
            pl.BlockSpec((1, t, LANES), lambda bi, hp, i: (bi, 0, vb + hp)),
        ],
        out_specs=pl.BlockSpec((1, tq, LANES), lambda bi, hp, i: (bi, i, hp)),
        out_shape=jax.ShapeDtypeStruct((b, t, D_SB), BF16),
        scratch_shapes=[pltpu.VMEM((2, tq, DH_SB), F32), pltpu.VMEM((2, tq, 1), F32)],
        compiler_params=_params(("parallel", "parallel", "arbitrary")),
        name="sb_prompt",
    )(pb, pb, pb)


def _diff_prompt_kernel(lam_ref, rel_ref, q_ref, k_ref, v_ref, tz_ref, g_ref, o_ref,
                        q2_ref, m_ref, acc_ref, z_ref, *, blk, lam_init):
    hp = pl.program_id(1)
    i = pl.program_id(2)
    r = lax.broadcasted_iota(I32, (2 * blk, blk), 0)
    c = lax.broadcasted_iota(I32, (2 * blk, blk), 1)
    causal = c <= jnp.where(r >= blk, r - blk, r)
    lane = lax.broadcasted_iota(I32, (blk, DV_DIFF), 1)
    ones_col = (lane == 0).astype(BF16)
    scale = DQ_DIFF ** -0.5
    for hh in range(2):
        q = q_ref[0, :, hh * DV_DIFF:(hh + 1) * DV_DIFF]
        zero = jnp.zeros_like(q)
        q2_ref[hh, 0:blk, :] = jnp.where(lane < DQ_DIFF, q, zero)
        q2_ref[hh, blk:2 * blk, :] = jnp.where(lane >= DQ_DIFF, q, zero)
    m_ref[...] = jnp.full_like(m_ref, NEG_INF)
    acc_ref[...] = jnp.zeros_like(acc_ref)

    def logits(j, buf):
        start = pl.multiple_of(j * blk, blk)
        for hh in range(2):
            k = k_ref[0, pl.ds(start, blk), hh * DV_DIFF:(hh + 1) * DV_DIFF]
            z_ref[buf, hh] = _dot_nt(q2_ref[hh], k)

    def consume(j, buf, kind):
        start = pl.multiple_of(j * blk, blk)
        for hh in range(2):
            lo = hh * DV_DIFF
            v = jnp.concatenate([v_ref[0, pl.ds(start, blk), lo:lo + DV_DIFF], ones_col], axis=1)
            z = z_ref[buf, hh] * scale
            if kind == 2:
                z = z + rel_ref[N_BUCKETS - 1, hp * 2 + hh]
            else:
                bias = tz_ref[hh, kind]
                z = z + jnp.concatenate([bias, bias], axis=0)
            if kind == 0:
                z = jnp.where(causal, z, NEG_INF)
            m_prev = m_ref[hh]
            m_new = jnp.maximum(m_prev, jnp.max(z, axis=1, keepdims=True))
            alpha = jnp.exp(m_prev - m_new)
            p = jnp.exp(z - m_new)
            acc_ref[hh] = alpha * acc_ref[hh] + _dot(p.astype(BF16), v)
            m_ref[hh] = m_new

    logits(i, 0)
    logits(jnp.maximum(i - 1, 0), 1)
    consume(i, 0, 0)

    @pl.when(i >= 1)
    def _():
        logits(jnp.maximum(i - 2, 0), 0)
        consume(i - 1, 1, 1)

    n_far = jnp.maximum(i - 1, 0)

    def pair(p, carry):
        ja = i - 2 - 2 * p
        logits(ja - 1, 1)
        consume(ja, 0, 2)
        logits(jnp.maximum(ja - 2, 0), 0)
        consume(ja - 1, 1, 2)
        return carry

    lax.fori_loop(0, n_far // 2, pair, 0)

    @pl.when(n_far % 2 == 1)
    def _():
        consume(0, 0, 2)

    lam = lam_ref[0, 0]
    outs = []
    for hh in range(2):
        acc = acc_ref[hh]
        o = acc[:, 0:DV_DIFF] / acc[:, DV_DIFF:DV_DIFF + 1]
        o = o[0:blk] - lam * o[blk:2 * blk]
        outs.append(_rmsnorm(o, g_ref[...]) * (1.0 - lam_init))
    o_ref[0] = jnp.concatenate(outs, axis=1).astype(BF16)


def _diff_prompt(pb, tz, lam, rel_bias, g_diff, blk, lam_init):
    b, t, _ = pb.shape
    qb, kb, vb = OFF_Q_DF // LANES, OFF_K_DF // LANES, OFF_V_DF // LANES
    return pl.pallas_call(
        functools.partial(_diff_prompt_kernel, blk=blk, lam_init=lam_init),
        grid=(b, H_DIFF // 2, t // blk),
        in_specs=[
            pl.BlockSpec(memory_space=pltpu.SMEM),
            pl.BlockSpec(memory_space=pltpu.SMEM),
            pl.BlockSpec((1, blk, LANES), lambda bi, hp, i: (bi, i, qb + hp)),
            pl.BlockSpec((1, t, LANES), lambda bi, hp, i: (bi, 0, kb + hp)),
            pl.BlockSpec((1, t, LANES), lambda bi, hp, i: (bi, 0, vb + hp)),
            pl.BlockSpec((2, 2, blk, blk), lambda bi, hp, i: (hp, 0, 0, 0)),
            pl.BlockSpec((1, DV_DIFF), lambda bi, hp, i: (0, 0)),
        ],
        out_specs=pl.BlockSpec((1, blk, LANES), lambda bi, hp, i: (bi, i, hp)),
        out_shape=jax.ShapeDtypeStruct((b, t, D_DIFF), BF16),
        scratch_shapes=[
            pltpu.VMEM((2, 2 * blk, DV_DIFF), BF16),
            pltpu.VMEM((2, 2 * blk, 1), F32),
            pltpu.VMEM((2, 2 * blk, 2 * DV_DIFF), F32),
            pltpu.VMEM((2, 2, 2 * blk, blk), F32),
        ],
        compiler_params=_params(("parallel", "parallel", "arbitrary"), vmem=DIFF_VMEM_LIMIT),
        name="diff_prompt",
    )(lam, rel_bias, pb, pb, pb, tz, g_diff.reshape(1, DV_DIFF))


def _mem_attn_kernel(q_ref, k_ref, v_ref, o_ref, *, tq):
    rows = max(tq, SUBLANES)
    q_all = q_ref[0].astype(F32) * (DH_MEM ** -0.5)
    if rows > tq:
        q_all = jnp.concatenate([q_all, jnp.zeros((rows - tq, D_MEM), F32)], axis=0)
    q_all = q_all.astype(BF16)
    outs = []
    for h in range(H_MEM):
        lo = h * DH_MEM
        q = q_all[:, lo:lo + DH_MEM]
        k = k_ref[0, :, lo:lo + DH_MEM].astype(BF16)
        v = v_ref[0, :, lo:lo + DH_MEM].astype(BF16)
        z = _dot_nt(q, k)
        p = jnp.exp(z - jnp.max(z, axis=1, keepdims=True))
        outs.append(_dot(p.astype(BF16), v) / jnp.sum(p, axis=1, keepdims=True))
    o_ref[0] = jnp.concatenate(outs, axis=1)[0:tq]


def _mem_attn(q, q_block, k, k_block, v, v_block, tq, name):
    b, t, _ = q.shape
    m = k.shape[1]
    return pl.pallas_call(
        functools.partial(_mem_attn_kernel, tq=tq),
        grid=(b, t // tq),
        in_specs=[
            pl.BlockSpec((1, tq, D_MEM), lambda bi, i: (bi, i, q_block)),
            pl.BlockSpec((1, m, D_MEM), lambda bi, i: (bi, 0, k_block)),
            pl.BlockSpec((1, m, D_MEM), lambda bi, i: (bi, 0, v_block)),
        ],
        out_specs=pl.BlockSpec((1, tq, D_MEM), lambda bi, i: (bi, i, 0)),
        out_shape=jax.ShapeDtypeStruct((b, t, D_MEM), F32),
        compiler_params=_params(("parallel", "parallel")),
        name=name,
    )(q, k, v)


def _pad_rows(x, rows):
    return jnp.concatenate([x, jnp.zeros((rows - x.shape[0], x.shape[1]), x.dtype)], axis=0)


def _query_columns(q, n_heads, width):
    t = q.shape[0]
    rows = jnp.concatenate(
        [jnp.broadcast_to(q[i:i + 1], (SUBLANES, q.shape[1])) for i in range(t)], axis=0)
    r = lax.broadcasted_iota(I32, rows.shape, 0)
    c = lax.broadcasted_iota(I32, rows.shape, 1)
    rows = jnp.where(jnp.right_shift(c, int(math.log2(width))) == (r & (SUBLANES - 1)), rows, 0.0)
    return rows


def _suffix_sum_lanes(x):
    lane = lax.broadcasted_iota(I32, x.shape, 1)
    shift = 1
    while shift < PAGE_SIZE:
        x = x + jnp.where(lane < PAGE_SIZE - shift, pltpu.roll(x, PAGE_SIZE - shift, axis=1), 0.0)
        shift *= 2
    return x


def _head_rows_to_tokens(acc_t, n_tok, width):
    rows = _pad_lanes(acc_t, LANES).T[0:n_tok * SUBLANES]
    rr = lax.broadcasted_iota(I32, rows.shape, 0)
    cc = lax.broadcasted_iota(I32, rows.shape, 1)
    rows = jnp.where(jnp.right_shift(cc, int(math.log2(width))) == (rr & (SUBLANES - 1)), rows, 0.0)
    return rows


def _pad_lanes(x, lanes):
    return jnp.concatenate([x, jnp.zeros((x.shape[0], lanes - x.shape[1]), x.dtype)], axis=1)


def _sb_decode_kernel(pt_ref, q_ref, kn_ref, vn_ref, *refs, n_tok, n_pages, pps):
    kt_refs = refs[:pps]
    vt_refs = refs[pps:2 * pps]
    o_ref, qt_ref, acc_ref, c_ref = refs[2 * pps:]
    j = pl.program_id(1)
    n_rows = n_tok * SUBLANES

    def attend(z_pages, vt, masks):
        drops, suffixes = [], []
        for z, mask in zip(z_pages, masks):
            drop = _softplus(z)
            if mask is not None:
                drop = jnp.where(mask, drop, 0.0)
            suffixes.append(_suffix_sum_lanes(drop))
        carry = c_ref[...]
        weights = []
        for z, suffix, mask in zip(z_pages, suffixes, masks):
            log_w = z - suffix - carry
            if mask is not None:
                log_w = jnp.where(mask, log_w, NEG_INF)
            weights.append(jnp.exp(log_w).astype(BF16))
            carry = carry + suffix[:, 0:1]
        c_ref[...] = carry
        acc_ref[...] += _dot_nt(vt, jnp.concatenate(weights, axis=1))

    @pl.when(j == 0)
    def _():
        q = q_ref[0] * (DH_SB ** -0.5)
        qt_ref[...] = _query_columns(q, H_SB, DH_SB).astype(BF16)
        acc_ref[...] = jnp.zeros_like(acc_ref)
        c_ref[...] = jnp.zeros_like(c_ref)
        kn = _pad_rows(kn_ref[0], PAGE_SIZE).astype(BF16)
        vn_t = _pad_rows(vn_ref[0], PAGE_SIZE).T.astype(BF16)
        row = lax.broadcasted_iota(I32, (n_rows, PAGE_SIZE), 0)
        key = lax.broadcasted_iota(I32, (n_rows, PAGE_SIZE), 1)
        attend([_dot_nt(qt_ref[...], kn)], vn_t, [key < jnp.right_shift(row, 3)])

    kt = jnp.concatenate([ref[0].astype(BF16) for ref in kt_refs], axis=1)
    vt = jnp.concatenate([ref[0].astype(BF16) for ref in vt_refs], axis=1)
    z = _dot(qt_ref[...], kt)
    attend([z[:, p * PAGE_SIZE:(p + 1) * PAGE_SIZE] for p in range(pps)], vt,
           [None] * pps)

    @pl.when(j == n_pages // pps - 1)
    def _():
        rows = _head_rows_to_tokens(acc_ref[...], n_tok, DH_SB)
        o_ref[0] = jnp.sum(rows.reshape(n_tok, SUBLANES, D_SB), axis=1)


def _page_specs(width, n_pages, pps):
    def spec(p):
        return pl.BlockSpec(
            (1, width, PAGE_SIZE),
            lambda bi, j, pt: (pt[bi, n_pages - 1 - (j * pps + p)], 0, 0))
    return [spec(p) for p in range(pps)]


def _sb_decode(q, k_new, v_new, cache_kt, cache_vt, page_table):
    b, t, _ = q.shape
    n_pages = page_table.shape[1]
    pps = SB_PAGES_PER_STEP
    tok = pl.BlockSpec((1, t, D_SB), lambda bi, j, pt: (bi, 0, 0))
    grid_spec = pltpu.PrefetchScalarGridSpec(
        num_scalar_prefetch=1,
        grid=(b, n_pages // pps),
        in_specs=[tok, tok, tok] + _page_specs(D_SB, n_pages, pps) + _page_specs(D_SB, n_pages, pps),
        out_specs=tok,
        scratch_shapes=[
            pltpu.VMEM((t * SUBLANES, D_SB), BF16),
            pltpu.VMEM((D_SB, t * SUBLANES), F32),
            pltpu.VMEM((t * SUBLANES, 1), F32),
        ],
    )
    return pl.pallas_call(
        functools.partial(_sb_decode_kernel, n_tok=t, n_pages=n_pages, pps=pps),
        grid_spec=grid_spec,
        out_shape=jax.ShapeDtypeStruct((b, t, D_SB), F32),
        compiler_params=_params(("parallel", "arbitrary")),
        name="sb_decode",
    )(page_table, q, k_new, v_new, *([cache_kt] * pps), *([cache_vt] * pps))


def _diff_decode_kernel(pt_ref, lam_ref, q_ref, kn_ref, vn_ref, bias_ref, g_ref, *refs,
                        n_tok, n_pages, lam_init, pps):
    kt_refs = refs[:pps]
    vt_refs = refs[pps:2 * pps]
    o_ref, qt_ref, acc_ref, m_ref, l_ref = refs[2 * pps:]
    j = pl.program_id(1)
    half = n_tok * SUBLANES
    scale = DQ_DIFF ** -0.5

    def attend(z, vt):
        m_prev = m_ref[...]
        m_new = jnp.maximum(m_prev, jnp.max(z, axis=1, keepdims=True))
        alpha = jnp.exp(m_prev - m_new)
        p = jnp.exp(z - m_new)
        l_ref[...] = alpha * l_ref[...] + jnp.sum(p, axis=1, keepdims=True)
        acc_ref[...] = (acc_ref[...] * _col_to_row(alpha)[:, 0:2 * half]
                        + _dot_nt(vt, p.astype(BF16)))
        m_ref[...] = m_new

    @pl.when(j == 0)
    def _():
        q = _query_columns(q_ref[0], H_DIFF, 2 * DQ_DIFF)
        lane = lax.broadcasted_iota(I32, q.shape, 1)
        first = (jnp.right_shift(lane, 5) & 1) == 0
        q2 = jnp.concatenate([jnp.where(first, q, 0.0), jnp.where(first, 0.0, q)], axis=0)
        qt_ref[...] = q2.astype(BF16)
        acc_ref[...] = jnp.zeros_like(acc_ref)
        m_ref[...] = jnp.full_like(m_ref, NEG_INF)
        l_ref[...] = jnp.zeros_like(l_ref)
        kn = _pad_rows(kn_ref[0], PAGE_SIZE).astype(BF16)
        vn_t = _pad_rows(vn_ref[0], PAGE_SIZE).T.astype(BF16)
        row = lax.broadcasted_iota(I32, (2 * half, PAGE_SIZE), 0)
        key = lax.broadcasted_iota(I32, (2 * half, PAGE_SIZE), 1)
        z_new = _dot_nt(qt_ref[...], kn) * scale + bias_ref[1, 0:2 * half]
        attend(jnp.where(key <= (jnp.right_shift(row, 3) & (n_tok - 1)), z_new, NEG_INF), vn_t)

    kt = jnp.concatenate([ref[0].astype(BF16) for ref in kt_refs], axis=1)
    vt = jnp.concatenate([ref[0].astype(BF16) for ref in vt_refs], axis=1)
    far = bias_ref[2, 0:2 * half]
    near = jnp.where(j == 0, bias_ref[0, 0:2 * half], far)
    bias = jnp.concatenate([near] + [far] * (pps - 1), axis=1)
    attend(_dot(qt_ref[...], kt) * scale + bias, vt)

    @pl.when(j == n_pages // pps - 1)
    def _():
        rows = _pad_lanes(acc_ref[...], LANES).T[0:2 * half] / l_ref[...]
        o = rows[0:half] - lam_ref[0, 0] * rows[half:2 * half]
        rr = lax.broadcasted_iota(I32, o.shape, 0)
        cc = lax.broadcasted_iota(I32, o.shape, 1)
        o = jnp.where(jnp.right_shift(cc, 6) == (rr & (SUBLANES - 1)), o, 0.0)
        ms = jnp.sum(o * o, axis=1, keepdims=True) * (1.0 / DV_DIFF)
        o = (o * lax.rsqrt(ms + EPS)) * g_ref[...] * (1.0 - lam_init)
        o_ref[0] = jnp.sum(o.reshape(n_tok, SUBLANES, D_DIFF), axis=1)


def _diff_decode(q, k_new, v_new, cache_kt, cache_vt, page_table, lam, dec_bias, g_tiled, lam_init):
    b, t, _ = q.shape
    n_pages = page_table.shape[1]
    pps = DIFF_PAGES_PER_STEP
    tok = pl.BlockSpec((1, t, D_DIFF), lambda bi, j, pt: (bi, 0, 0))
    grid_spec = pltpu.PrefetchScalarGridSpec(
        num_scalar_prefetch=1,
        grid=(b, n_pages // pps),
        in_specs=[
            pl.BlockSpec(memory_space=pltpu.SMEM),
            tok, tok, tok,
            pl.BlockSpec((3, LANES, PAGE_SIZE), lambda bi, j, pt: (0, 0, 0)),
            pl.BlockSpec((1, D_DIFF), lambda bi, j, pt: (0, 0)),
        ] + _page_specs(D_QK_DIFF, n_pages, pps) + _page_specs(D_DIFF, n_pages, pps),
        out_specs=tok,
        scratch_shapes=[
            pltpu.VMEM((2 * t * SUBLANES, D_QK_DIFF), BF16),
            pltpu.VMEM((D_DIFF, 2 * t * SUBLANES), F32),
            pltpu.VMEM((2 * t * SUBLANES, 1), F32),
            pltpu.VMEM((2 * t * SUBLANES, 1), F32),
        ],
    )
    return pl.pallas_call(
        functools.partial(_diff_decode_kernel, n_tok=t, n_pages=n_pages, lam_init=lam_init, pps=pps),
        grid_spec=grid_spec,
        out_shape=jax.ShapeDtypeStruct((b, t, D_DIFF), F32),
        compiler_params=_params(("parallel", "arbitrary")),
        name="diff_decode",
    )(page_table, lam, q, k_new, v_new, dec_bias, g_tiled,
      *([cache_kt] * pps), *([cache_vt] * pps))


def _finish_kernel(x_ref, osb_ref, odf_ref, omem_ref, gmix_ref, wg_ref, bg_ref, wsb_ref, wdf_ref,
                   wmem_ref, wout_ref, gffn_ref, wr_ref, br_ref, xmid_ref, xn_ref, route_ref, count_ref,
                   carry_ref):
    x = x_ref[...]
    u = _rmsnorm(x, gmix_ref[...]).astype(BF16)
    gates = jax.nn.sigmoid(_dot(u, wg_ref[...]) + bg_ref[...])
    h = (gates[:, 0:D_MODEL] * _dot(osb_ref[...].astype(BF16), wsb_ref[...])
         + gates[:, D_MODEL:2 * D_MODEL] * _dot(odf_ref[...].astype(BF16), wdf_ref[...])
         + gates[:, 2 * D_MODEL:3 * D_MODEL] * _dot(omem_ref[...].astype(BF16), wmem_ref[...]))
    xm = x + _dot(h.astype(BF16), wout_ref[...])
    xmid_ref[...] = xm
    xn = _rmsnorm(xm, gffn_ref[...])
    xn_ref[...] = xn
    xh, xl = _split_bf16(xn)
    wh, wl = _split_bf16(wr_ref[...])
    lg = _dot(xh, wh) + _dot(xh, wl) + _dot(xl, wh) + br_ref[...]
    lane = lax.broadcasted_iota(I32, lg.shape, 1)
    is_group = lane < N_GROUPS
    gl = jnp.where(is_group, lg, -jnp.inf)
    gmax = jnp.max(gl, axis=1, keepdims=True)
    grp = jnp.min(jnp.where(gl == gmax, lane, LANES), axis=1, keepdims=True)
    p_grp = 1.0 / jnp.sum(jnp.where(is_group, jnp.exp(gl - gmax), 0.0), axis=1, keepdims=True)
    in_group = (lane >= N_GROUPS) & (lane < N_GROUPS + N_EXPERTS) & (
        jnp.right_shift(lane - N_GROUPS, 3) == grp)
    el = jnp.where(in_group, lg, -jnp.inf)
    v1 = jnp.max(el, axis=1, keepdims=True)
    i1 = jnp.min(jnp.where(el == v1, lane, LANES), axis=1, keepdims=True)
    el2 = jnp.where(lane == i1, -jnp.inf, el)
    v2 = jnp.max(el2, axis=1, keepdims=True)
    i2 = jnp.min(jnp.where(el2 == v2, lane, LANES), axis=1, keepdims=True)
    e = jnp.exp(v2 - v1)
    w1 = (1.0 / (1.0 + e)) * p_grp
    w2 = (e / (1.0 + e)) * p_grp
    @pl.when(pl.program_id(0) == 0)
    def _():
        carry_ref[...] = jnp.zeros_like(carry_ref)

    tm = lg.shape[0]
    oh1 = lane == i1 - N_GROUPS
    oh2 = lane == i2 - N_GROUPS
    both = jnp.where(oh1 | oh2, 1.0, 0.0)
    r = lax.broadcasted_iota(I32, (tm, tm), 0)
    c = lax.broadcasted_iota(I32, (tm, tm), 1)
    before = _dot((c < r).astype(BF16), both.astype(BF16)) + carry_ref[...]
    rank1 = jnp.sum(jnp.where(oh1, before, 0.0), axis=1, keepdims=True)
    rank2 = jnp.sum(jnp.where(oh2, before, 0.0), axis=1, keepdims=True)
    carry_ref[...] += jnp.sum(both, axis=0, keepdims=True)
    count_ref[...] = jnp.broadcast_to(carry_ref[...], count_ref.shape)

    rec = jnp.where(lane == R_EID1, (i1 - N_GROUPS).astype(F32), 0.0)
    rec = jnp.where(lane == R_EID2, (i2 - N_GROUPS).astype(F32), rec)
    rec = jnp.where(lane == R_W1, w1, rec)
    rec = jnp.where(lane == R_W2, w2, rec)
    rec = jnp.where(lane == R_RANK1, rank1, rec)
    rec = jnp.where(lane == R_RANK2, rank2, rec)
    route_ref[...] = rec


def _finish(x, o_sb, o_df, o_mem, g_mix, wg, bg, wsb, wdf, wmem, wout, g_ffn, wr, br, name):
    n, d = x.shape
    tm = min(FINISH_TILE, n)

    def rows(width):
        return pl.BlockSpec((tm, width), lambda i: (i, 0))

    def whole(a):
        return pl.BlockSpec(a.shape, lambda i: (0, 0))

    args = (x, o_sb, o_df, o_mem, g_mix.reshape(1, d), wg, bg.reshape(1, -1), wsb, wdf, wmem, wout,
            g_ffn.reshape(1, d), wr, br)
    in_specs = [rows(d), rows(D_SB), rows(D_DIFF), rows(D_MEM)] + [whole(a) for a in args[4:]]
    return pl.pallas_call(
        _finish_kernel,
        grid=(n // tm,),
        in_specs=in_specs,
        out_specs=[rows(d), rows(d), rows(LANES), pl.BlockSpec((SUBLANES, LANES), lambda i: (0, 0))],
        out_shape=[
            jax.ShapeDtypeStruct((n, d), F32),
            jax.ShapeDtypeStruct((n, d), F32),
            jax.ShapeDtypeStruct((n, LANES), F32),
            jax.ShapeDtypeStruct((SUBLANES, LANES), F32),
        ],
        scratch_shapes=[pltpu.VMEM((1, LANES), F32)],
        compiler_params=_params(("arbitrary",)),
        name=name,
    )(*args)


def _one_hots(route):
    lane = lax.broadcasted_iota(I32, route.shape, 1)
    oh1 = lane == route[:, R_EID1:R_EID1 + 1].astype(I32)
    oh2 = lane == route[:, R_EID2:R_EID2 + 1].astype(I32)
    return oh1, oh2


def _moe_dest_kernel(route_ref, count_ref, dest_ref, blk_ref):
    n_blk = blk_ref.shape[0]
    blocks = jnp.right_shift(count_ref[...].astype(I32) + (MOE_ROWS - 1), MOE_ROWS_LOG2).astype(F32)
    r = lax.broadcasted_iota(I32, (LANES, LANES), 0)
    c = lax.broadcasted_iota(I32, (LANES, LANES), 1)
    upto = (r <= c).astype(BF16)
    block_end = _dot(blocks.astype(BF16), upto)
    row_start = (block_end - blocks)[0:1] * float(MOE_ROWS)
    route = route_ref[...]
    oh1, oh2 = _one_hots(route)
    d1 = jnp.sum(jnp.where(oh1, row_start, 0.0), axis=1, keepdims=True) + route[:, R_RANK1:R_RANK1 + 1]
    d2 = jnp.sum(jnp.where(oh2, row_start, 0.0), axis=1, keepdims=True) + route[:, R_RANK2:R_RANK2 + 1]
    lane = lax.broadcasted_iota(I32, route.shape, 1)
    dest_ref[...] = jnp.where(lane == 0, d1, jnp.where(lane == 1, d2, 0.0)).astype(I32)

    @pl.when(pl.program_id(0) == 0)
    def _():
        b_idx = lax.broadcasted_iota(I32, (n_blk, LANES), 0).astype(F32)
        lane_b = lax.broadcasted_iota(I32, (n_blk, LANES), 1)
        done = (block_end[0:1] <= b_idx) & (lane_b < N_EXPERTS)
        expert = jnp.minimum(jnp.sum(jnp.where(done, 1.0, 0.0), axis=1, keepdims=True),
                             float(N_EXPERTS - 1))
        used = block_end[0:1, N_EXPERTS - 1:N_EXPERTS]
        blk_ref[...] = jnp.where(lane_b == 0, expert, jnp.where(lane_b == 1, used, 0.0)).astype(I32)


def _moe_dest(route, counts, n_blk):
    n = route.shape[0]
    tm = min(DEST_TILE, n)
    n_blk_pad = -(-n_blk // SUBLANES) * SUBLANES
    return pl.pallas_call(
        _moe_dest_kernel,
        grid=(n // tm,),
        in_specs=[pl.BlockSpec((tm, LANES), lambda i: (i, 0)),
                  pl.BlockSpec((SUBLANES, LANES), lambda i: (0, 0))],
        out_specs=[pl.BlockSpec((tm, LANES), lambda i: (i, 0)),
                   pl.BlockSpec((n_blk_pad, LANES), lambda i: (0, 0))],
        out_shape=[jax.ShapeDtypeStruct((n, LANES), I32),
                   jax.ShapeDtypeStruct((n_blk_pad, LANES), I32)],
        compiler_params=_params(("arbitrary",)),
        name="moe_dest",
    )(route, counts)


def _dispatch_kernel(dest_ref, x_ref, init_ref, xs_ref, sem):
    del init_ref
    tm = x_ref.shape[0]

    def row_copy(t, d):
        return pltpu.make_async_copy(x_ref.at[pl.ds(t, 1)], xs_ref.at[pl.ds(d, 1)], sem)

    def start(t, carry):
        row_copy(t, dest_ref[0, 0, 2 * t]).start(priority=0)
        row_copy(t, dest_ref[0, 0, 2 * t + 1]).start(priority=1)
        return carry

    def wait(t, carry):
        row_copy(t, dest_ref[0, 0, 2 * t]).wait()
        row_copy(t, dest_ref[0, 0, 2 * t + 1]).wait()
        return carry

    lax.fori_loop(0, tm, start, 0, unroll=ROW_DMA_UNROLL)
    lax.fori_loop(0, tm, wait, 0, unroll=ROW_DMA_UNROLL)


def _dispatch(dest_tiles, xn, n_rows):
    n, d = xn.shape
    tm = dest_tiles.shape[2] // 2
    return pl.pallas_call(
        _dispatch_kernel,
        grid=(n // tm,),
        in_specs=[
            pl.BlockSpec((1, 1, 2 * tm), lambda i: (i, 0, 0), memory_space=pltpu.SMEM),
            pl.BlockSpec((tm, d), lambda i: (i, 0)),
            pl.BlockSpec(memory_space=pl.ANY),
        ],
        out_specs=pl.BlockSpec(memory_space=pl.ANY),
        out_shape=jax.ShapeDtypeStruct((n_rows, d), F32),
        scratch_shapes=[pltpu.SemaphoreType.DMA(())],
        input_output_aliases={2: 0},
        compiler_params=_params(("arbitrary",)),
        name="moe_dispatch",
    )(dest_tiles, xn, jnp.zeros((n_rows, d), F32))


def _expert_kernel(be_ref, used_ref, xs_ref, wg_ref, wu_ref, wd_ref, y_ref, wg_b, wu_b, wd_b):
    b = pl.program_id(0)

    @pl.when(b < used_ref[0])
    def _():
        @pl.when((b == 0) | (be_ref[b] != be_ref[jnp.maximum(b - 1, 0)]))
        def _():
            wg_b[...] = wg_ref[0].astype(BF16)
            wu_b[...] = wu_ref[0].astype(BF16)
            wd_b[...] = wd_ref[0].astype(BF16)

        x = xs_ref[...].astype(BF16)
        h = jax.nn.silu(_dot(x, wg_b[...])) * _dot(x, wu_b[...])
        y_ref[...] = _dot(h.astype(BF16), wd_b[...])

    @pl.when(b >= used_ref[0])
    def _():
        y_ref[...] = jnp.zeros_like(y_ref)


def _experts(blk_exp, used, xs, wg, wu, wd):
    n_rows, d = xs.shape
    n_blk = n_rows // MOE_ROWS
    grid_spec = pltpu.PrefetchScalarGridSpec(
        num_scalar_prefetch=2,
        grid=(n_blk,),
        in_specs=[
            pl.BlockSpec((MOE_ROWS, d), lambda b, be, used: (b, 0)),
            pl.BlockSpec((1, d, D_EXPERT), lambda b, be, used: (be[b], 0, 0)),
            pl.BlockSpec((1, d, D_EXPERT), lambda b, be, used: (be[b], 0, 0)),
            pl.BlockSpec((1, D_EXPERT, d), lambda b, be, used: (be[b], 0, 0)),
        ],
        out_specs=pl.BlockSpec((MOE_ROWS, d), lambda b, be, used: (b, 0)),
        scratch_shapes=[pltpu.VMEM((d, D_EXPERT), BF16), pltpu.VMEM((d, D_EXPERT), BF16),
                        pltpu.VMEM((D_EXPERT, d), BF16)],
    )
    return pl.pallas_call(
        _expert_kernel,
        grid_spec=grid_spec,
        out_shape=jax.ShapeDtypeStruct((n_rows, d), F32),
        compiler_params=_params(("arbitrary",)),
        name="moe_experts",
    )(blk_exp, used, xs, wg, wu, wd)


def _combine_kernel(dest_ref, next_ref, route_ref, xmid_ref, g_ref, yb_ref, out_ref, buf_ref, sems):
    i = pl.program_id(0)
    tm = xmid_ref.shape[0]
    slot = i % 2

    def row_copy(s, t, k, d):
        return pltpu.make_async_copy(yb_ref.at[pl.ds(d, 1)], buf_ref.at[s, k, pl.ds(t, 1)], sems.at[s])

    def request(idx_ref, s):
        def start(t, carry):
            row_copy(s, t, 0, idx_ref[0, 0, 2 * t]).start(priority=0)
            row_copy(s, t, 1, idx_ref[0, 0, 2 * t + 1]).start(priority=1)
            return carry

        lax.fori_loop(0, tm, start, 0, unroll=ROW_DMA_UNROLL)

    @pl.when(i == 0)
    def _():
        request(dest_ref, 0)

    @pl.when(i + 1 < pl.num_programs(0))
    def _():
        request(next_ref, 1 - slot)

    def wait(t, carry):
        row_copy(slot, t, 0, 0).wait()
        row_copy(slot, t, 1, 0).wait()
        return carry

    lax.fori_loop(0, tm, wait, 0, unroll=ROW_DMA_UNROLL)
    route = route_ref[...]
    y = buf_ref[slot, 0] * route[:, R_W1:R_W1 + 1] + buf_ref[slot, 1] * route[:, R_W2:R_W2 + 1]
    out_ref[...] = _rmsnorm(xmid_ref[...] + y, g_ref[...])


def _combine(dest_tiles, route, xmid, g_final, yb):
    n, d = xmid.shape
    tm = dest_tiles.shape[2] // 2
    last = n // tm - 1
    return pl.pallas_call(
        _combine_kernel,
        grid=(n // tm,),
        in_specs=[
            pl.BlockSpec((1, 1, 2 * tm), lambda i: (i, 0, 0), memory_space=pltpu.SMEM),
            pl.BlockSpec((1, 1, 2 * tm), lambda i: (jnp.minimum(i + 1, last), 0, 0), memory_space=pltpu.SMEM),
            pl.BlockSpec((tm, LANES), lambda i: (i, 0)),
            pl.BlockSpec((tm, d), lambda i: (i, 0)),
            pl.BlockSpec((1, d), lambda i: (0, 0)),
            pl.BlockSpec(memory_space=pl.ANY),
        ],
        out_specs=pl.BlockSpec((tm, d), lambda i: (i, 0)),
        out_shape=jax.ShapeDtypeStruct((n, d), F32),
        scratch_shapes=[pltpu.VMEM((2, 2, tm, d), F32), pltpu.SemaphoreType.DMA((2,))],
        compiler_params=_params(("arbitrary",)),
        name="moe_combine",
    )(dest_tiles, dest_tiles, route, xmid, g_final.reshape(1, d), yb)


def _moe_and_final_norm(xmid, xn, route, counts, wg, wu, wd, g_final):
    n, d = xmid.shape
    n_blk = -(-(2 * n + N_EXPERTS * (MOE_ROWS - 1)) // MOE_ROWS)
    dest, blk = _moe_dest(route, counts, n_blk)
    tm = min(ROW_TILE, n)
    dest_tiles = dest[:, 0:2].reshape(n // tm, 1, 2 * tm)
    xs = _dispatch(dest_tiles, xn, n_blk * MOE_ROWS)
    yb = _experts(blk[0:n_blk, 0], blk[0, 1:2], xs, wg, wu, wd)
    return _combine(dest_tiles, route, xmid, g_final, yb)


def _col_scale():
    s = jnp.ones((1, D_IN), F32)
    return s.at[:, OFF_Q_SB:OFF_Q_SB + D_SB].set(DH_SB ** -0.5)


def kernel(x_prompt, x_sample, cache_sb_k, cache_sb_v, cache_diff_k, cache_diff_v, cache_mem_k, cache_mem_v, page_table, mem_prompt, norm_mix_g, w_in, diff_lam_q1, diff_lam_k1, diff_lam_q2, diff_lam_k2, diff_norm_g, mem_norm_g, w_mem_kv, w_gate, b_gate, w_br_sb, w_br_diff, w_br_mem, w_out, norm_ffn_g, w_router_group, b_router_group, w_router_expert, b_router_expert, w_exp_gate, w_exp_up, w_exp_down, rel_bias, norm_final_g):
    depth = w_in.shape[0]
    assert depth == 1, "single-layer stack only"
    assert page_table.shape[1] % SB_PAGES_PER_STEP == 0 and page_table.shape[1] % DIFF_PAGES_PER_STEP == 0
    b, t, d = x_prompt.shape
    bs, ts, _ = x_sample.shape
    n_mem = mem_prompt.shape[1]
    sb_tq, diff_blk = min(SB_QUERY_BLOCK, t), min(DIFF_BLOCK, t)
    lam_init = 0.8 - 0.6 * math.exp(-0.3 * 0)

    w_in_b = w_in[0].astype(BF16)
    wg_b, wsb_b, wdf_b = w_gate[0].astype(BF16), w_br_sb[0].astype(BF16), w_br_diff[0].astype(BF16)
    wmem_b, wout_b = w_br_mem[0].astype(BF16), w_out[0].astype(BF16)
    weg_b, weu_b, wed_b = w_exp_gate[0], w_exp_up[0], w_exp_down[0]
    pad = LANES - N_GROUPS - N_EXPERTS
    w_router = jnp.concatenate(
        [w_router_group[0], w_router_expert[0], jnp.zeros((d, pad), F32)], axis=1)
    b_router = jnp.concatenate(
        [b_router_group[0], b_router_expert[0], jnp.zeros((pad,), F32)]).reshape(1, LANES)
    lam_vecs = jnp.concatenate([diff_lam_q1, diff_lam_k1, diff_lam_q2, diff_lam_k2], axis=0)

    tz, dec_bias, lam_tile = _prep(rel_bias, lam_vecs, diff_blk, lam_init)
    lam = lam_tile[0:1, 0:1]

    def heads_last(a_t, n_heads, width):
        bb, _, tt = a_t.shape
        return jnp.transpose(a_t.reshape(bb, n_heads, width, tt), (0, 3, 1, 2))[None]

    def keys_last(cache, n_heads, width):
        pool, page = cache.shape[1], cache.shape[2]
        return jnp.transpose(cache[0], (0, 2, 3, 1)).reshape(pool, n_heads * width, page)

    k_sb, v_sb, k_df, v_df, pb = _norm_proj(
        x_prompt, norm_mix_g[0], w_in_b, _col_scale(),
        [(OFF_K_SB, D_SB), (OFF_V_SB, D_SB), (OFF_K_DF, D_QK_DIFF), (OFF_V_DF, D_DIFF)], True,
        "proj_prompt")
    mk, mv, mem_b = _norm_proj(
        mem_prompt, mem_norm_g[0], w_mem_kv[0].astype(BF16),
        jnp.ones((1, 2 * D_MEM), F32), [(0, D_MEM), (D_MEM, D_MEM)], True, "proj_memory")
    o_sb = _sb_prompt(pb, sb_tq, SB_KEY_BLOCK)
    o_df = _diff_prompt(pb, tz, lam, rel_bias, diff_norm_g[0], diff_blk, lam_init)
    o_mem = _mem_attn(pb, OFF_Q_MEM // D_MEM, mem_b, 0, mem_b, 1, min(MEM_QUERY_BLOCK, t), "mem_prompt")
    xmid, xn, route, counts = _finish(
        x_prompt.reshape(b * t, d), o_sb.reshape(b * t, D_SB), o_df.reshape(b * t, D_DIFF),
        o_mem.reshape(b * t, D_MEM), norm_mix_g[0], wg_b, b_gate[0], wsb_b, wdf_b, wmem_b, wout_b,
        norm_ffn_g[0], w_router, b_router, "finish_prompt")
    y_prompt = _moe_and_final_norm(xmid, xn, route, counts, weg_b, weu_b, wed_b, norm_final_g)

    q_sb_s, k_sb_s, v_sb_s, q_df_s, k_df_s, v_df_s, q_mem_s, _ = _norm_proj(
        x_sample.reshape(1, bs * ts, d), norm_mix_g[0], w_in_b, jnp.ones((1, D_IN), F32),
        [(OFF_Q_SB, D_SB), (OFF_K_SB, D_SB), (OFF_V_SB, D_SB), (OFF_Q_DF, D_QK_DIFF),
         (OFF_K_DF, D_QK_DIFF), (OFF_V_DF, D_DIFF), (OFF_Q_MEM, D_MEM)], False, "proj_sample")
    o_sb_s = _sb_decode(
        q_sb_s.reshape(bs, ts, D_SB), k_sb_s.reshape(bs, ts, D_SB), v_sb_s.reshape(bs, ts, D_SB),
        keys_last(cache_sb_k, H_SB, DH_SB), keys_last(cache_sb_v, H_SB, DH_SB), page_table)
    o_df_s = _diff_decode(
        q_df_s.reshape(bs, ts, D_QK_DIFF), k_df_s.reshape(bs, ts, D_QK_DIFF),
        v_df_s.reshape(bs, ts, D_DIFF), keys_last(cache_diff_k, H_DIFF, 2 * DQ_DIFF),
        keys_last(cache_diff_v, H_DIFF, DV_DIFF), page_table, lam, dec_bias,
        jnp.tile(diff_norm_g[0], H_DIFF).reshape(1, D_DIFF), lam_init)
    o_mem_s = _mem_attn(
        q_mem_s.reshape(bs, ts, D_MEM), 0, cache_mem_k[0].reshape(bs, n_mem, D_MEM), 0,
        cache_mem_v[0].reshape(bs, n_mem, D_MEM), 0, ts, "mem_sample")
    xmid_s, xn_s, route_s, counts_s = _finish(
        x_sample.reshape(bs * ts, d), o_sb_s.reshape(bs * ts, D_SB), o_df_s.reshape(bs * ts, D_DIFF),
        o_mem_s.reshape(bs * ts, D_MEM), norm_mix_g[0], wg_b, b_gate[0], wsb_b, wdf_b, wmem_b, wout_b,
        norm_ffn_g[0], w_router, b_router, "finish_sample")
    y_sample = _moe_and_final_norm(xmid_s, xn_s, route_s, counts_s, weg_b, weu_b, wed_b, norm_final_g)

    return (y_prompt.reshape(b, t, d), y_sample.reshape(bs, ts, d),
            heads_last(k_sb, H_SB, DH_SB), heads_last(v_sb, H_SB, DH_SB),
            heads_last(k_df, H_DIFF, 2 * DQ_DIFF), heads_last(v_df, H_DIFF, DV_DIFF),
            heads_last(mk, H_MEM, DH_MEM), heads_last(mv, H_MEM, DH_MEM),
            k_sb_s.reshape(1, bs, ts, H_SB, DH_SB), v_sb_s.reshape(1, bs, ts, H_SB, DH_SB),
            k_df_s.reshape(1, bs, ts, H_DIFF, 2 * DQ_DIFF), v_df_s.reshape(1, bs, ts, H_DIFF, DV_DIFF))
```

```python
import functools
import math

import jax
import jax.numpy as jnp
from jax import lax
from jax.experimental import pallas as pl
from jax.experimental.pallas import tpu as pltpu

F32 = jnp.float32
BF16 = jnp.bfloat16
I32 = jnp.int32

D_MODEL = 1024
PAGE_SIZE = 128
H_SB = 8
DH_SB = 64
H_DIFF = 4
DQ_DIFF = 32
DV_DIFF = 64
H_MEM = 4
DH_MEM = 64
D_SB = H_SB * DH_SB
D_DIFF = H_DIFF * DV_DIFF
D_MEM = H_MEM * DH_MEM
D_QK_DIFF = H_DIFF * 2 * DQ_DIFF
D_IN = 3 * D_SB + 2 * D_QK_DIFF + D_DIFF + D_MEM
N_BUCKETS = 32
MAX_EXACT = 16
MAX_DISTANCE = 128
N_GROUPS = 4
EXPERTS_PER_GROUP = 8
N_EXPERTS = N_GROUPS * EXPERTS_PER_GROUP
D_EXPERT = 512
EPS = 1e-6
NEG_INF = -1e30

LANES = 128
SUBLANES = 8
VMEM_LIMIT = 48 * 1024 * 1024
DIFF_VMEM_LIMIT = 56 * 1024 * 1024

OFF_Q_SB = 0
OFF_K_SB = D_SB
OFF_V_SB = 2 * D_SB
OFF_Q_DF = 3 * D_SB
OFF_K_DF = OFF_Q_DF + D_QK_DIFF
OFF_V_DF = OFF_K_DF + D_QK_DIFF
OFF_Q_MEM = OFF_V_DF + D_DIFF

MEM_QUERY_BLOCK = 1024
SB_QUERY_BLOCK = 2048
SB_KEY_BLOCK = 256
DIFF_BLOCK = 512
MOE_ROWS = 256
TOK_TILE = 256
FINISH_TILE = 512
DEST_TILE = 1024
ROW_TILE = 256
ROW_DMA_UNROLL = 32
SB_PAGES_PER_STEP = 32
DIFF_PAGES_PER_STEP = 64

R_EID1, R_EID2, R_W1, R_W2, R_RANK1, R_RANK2 = 0, 1, 2, 3, 4, 5


def _log2(n):
    assert n > 0 and n & (n - 1) == 0, n
    return n.bit_length() - 1


SUBLANE_SHIFT = _log2(SUBLANES)
MOE_ROWS_LOG2 = _log2(MOE_ROWS)


def _params(sem, vmem=VMEM_LIMIT):
    return pltpu.CompilerParams(dimension_semantics=sem, vmem_limit_bytes=vmem)


def _rmsnorm(x, g):
    ms = jnp.mean(x * x, axis=-1, keepdims=True)
    return (x * lax.rsqrt(ms + EPS)) * g


def _dot(a, b):
    return jnp.dot(a, b, preferred_element_type=F32)


def _dot_nt(a, b):
    return lax.dot_general(a, b, (((1,), (1,)), ((), ())), preferred_element_type=F32)


def _split_bf16(x):
    hi = x.astype(BF16)
    lo = (x - hi.astype(F32)).astype(BF16)
    return hi, lo


def _softplus(z):
    neg_abs = pltpu.bitcast(pltpu.bitcast(z, jnp.uint32) | jnp.uint32(0x80000000), F32)
    return jnp.maximum(z, 0.0) + jnp.log(1.0 + jnp.exp(neg_abs))


def _col_to_row(v):
    n = v.shape[0]
    r = lax.broadcasted_iota(I32, (n, LANES), 0)
    c = lax.broadcasted_iota(I32, (n, LANES), 1)
    return jnp.sum(jnp.where(r == c, v, 0.0), axis=0, keepdims=True)


def _norm_proj_kernel(x_ref, g_ref, w_ref, s_ref, *out_refs, f32_cols, transposed):
    u = _rmsnorm(x_ref[0], g_ref[...]).astype(BF16)
    p = _dot(u, w_ref[...])
    for ref, (lo, width) in zip(out_refs[:-1], f32_cols):
        ref[0] = p[:, lo:lo + width].T if transposed else p[:, lo:lo + width]
    out_refs[-1][0] = (p * s_ref[...]).astype(BF16)


def _norm_proj(x, g, w_bf16, col_scale, f32_cols, transposed, name):
    b, t, d = x.shape
    n_out = w_bf16.shape[1]
    tm = min(TOK_TILE, t)
    if transposed:
        out_shape = [jax.ShapeDtypeStruct((b, width, t), F32) for _, width in f32_cols]
        out_specs = [pl.BlockSpec((1, width, tm), lambda bi, i: (bi, 0, i)) for _, width in f32_cols]
    else:
        out_shape = [jax.ShapeDtypeStruct((b, t, width), F32) for _, width in f32_cols]
        out_specs = [pl.BlockSpec((1, tm, width), lambda bi, i: (bi, i, 0)) for _, width in f32_cols]
    out_shape.append(jax.ShapeDtypeStruct((b, t, n_out), BF16))
    out_specs.append(pl.BlockSpec((1, tm, n_out), lambda bi, i: (bi, i, 0)))
    return pl.pallas_call(
        functools.partial(_norm_proj_kernel, f32_cols=tuple(f32_cols), transposed=transposed),
        grid=(b, t // tm),
        in_specs=[
            pl.BlockSpec((1, tm, d), lambda bi, i: (bi, i, 0)),
            pl.BlockSpec((1, d), lambda bi, i: (0, 0)),
            pl.BlockSpec((d, n_out), lambda bi, i: (0, 0)),
            pl.BlockSpec((1, n_out), lambda bi, i: (0, 0)),
        ],
        out_specs=out_specs,
        out_shape=out_shape,
        compiler_params=_params(("parallel", "parallel")),
        name=name,
    )(x, g.reshape(1, d), w_bf16, col_scale)


def _t5_bucket(delta):
    n = jnp.maximum(delta, 0)
    nf = jnp.maximum(n, 1).astype(F32)
    large = MAX_EXACT + (jnp.log(nf / MAX_EXACT) / math.log(MAX_DISTANCE / MAX_EXACT)
                         * (N_BUCKETS - MAX_EXACT)).astype(I32)
    large = jnp.minimum(large, N_BUCKETS - 1)
    return jnp.where(n < MAX_EXACT, n, large)


def _bias_of_bucket(bucket, rel_ref, head):
    out = jnp.zeros(bucket.shape, F32)
    for b in range(N_BUCKETS):
        out = jnp.where(bucket == b, rel_ref[b, head], out)
    return out


def _bias_by_head(bucket, head, rel_ref):
    acc = jnp.zeros(bucket.shape, F32)
    for h in range(H_DIFF):
        acc = jnp.where(head == h, _bias_of_bucket(bucket, rel_ref, h), acc)
    return acc


def _prep_kernel(rel_ref, lam_ref, tz_ref, dec_ref, lam_out_ref, *, blk, lam_init, n_tok):
    r = lax.broadcasted_iota(I32, (blk, blk), 0)
    c = lax.broadcasted_iota(I32, (blk, blk), 1)
    for off in range(2):
        bucket = _t5_bucket(r - c + off * blk)
        for h in range(H_DIFF):
            tz_ref[h, off] = _bias_of_bucket(bucket, rel_ref, h)
    key = lax.broadcasted_iota(I32, (LANES, PAGE_SIZE), 1)
    row = lax.broadcasted_iota(I32, (LANES, PAGE_SIZE), 0)
    qi = jnp.right_shift(row, SUBLANE_SHIFT) & (n_tok - 1)
    head = row & (SUBLANES - 1)
    dec_ref[0] = _bias_by_head(_t5_bucket(PAGE_SIZE + qi - key), head, rel_ref)
    dec_ref[1] = _bias_by_head(_t5_bucket(qi - key), head, rel_ref)
    dec_ref[2] = _bias_by_head(jnp.full((LANES, PAGE_SIZE), N_BUCKETS - 1, I32), head, rel_ref)
    lq1, lk1, lq2, lk2 = lam_ref[0:1, :], lam_ref[1:2, :], lam_ref[2:3, :], lam_ref[3:4, :]
    lam = (jnp.exp(jnp.sum(lq1 * lk1, axis=-1, keepdims=True))
           - jnp.exp(jnp.sum(lq2 * lk2, axis=-1, keepdims=True)) + lam_init)
    lam_out_ref[...] = jnp.broadcast_to(lam, (SUBLANES, LANES))


def _prep(rel_bias, lam_vecs, blk, lam_init, n_tok):
    assert 2 * n_tok * SUBLANES <= LANES
    _log2(n_tok)
    return pl.pallas_call(
        functools.partial(_prep_kernel, blk=blk, lam_init=lam_init, n_tok=n_tok),
        in_specs=[
            pl.BlockSpec(memory_space=pltpu.SMEM),
            pl.BlockSpec(memory_space=pltpu.VMEM),
        ],
        out_specs=[pl.BlockSpec(memory_space=pltpu.VMEM)] * 3,
        out_shape=[
            jax.ShapeDtypeStruct((H_DIFF, 2, blk, blk), F32),
            jax.ShapeDtypeStruct((3, PAGE_SIZE, LANES), F32),
            jax.ShapeDtypeStruct((SUBLANES, LANES), F32),
        ],
        name="prep_bias_lambda",
    )(rel_bias, lam_vecs)


def _sb_prompt_kernel(q_ref, k_ref, v_ref, o_ref, acc_ref, c_ref, *, tq, tk):
    i = pl.program_id(2)
    n_diag = tq // tk
    r = lax.broadcasted_iota(I32, (tk, tk), 0)
    c = lax.broadcasted_iota(I32, (tk, tk), 1)
    tri = (r >= c).astype(BF16)

    def block(j, row0, diag):
        start = pl.multiple_of(j * tk, tk)
        rows = tq - row0
        if diag:
            rr = lax.broadcasted_iota(I32, (rows, tk), 0)
            cc = lax.broadcasted_iota(I32, (rows, tk), 1)
            strict = cc < rr
        for hh in range(2):
            lo = hh * DH_SB
            q = q_ref[0, row0:tq, lo:lo + DH_SB]
            k = k_ref[0, pl.ds(start, tk), lo:lo + DH_SB]
            v = v_ref[0, pl.ds(start, tk), lo:lo + DH_SB]
            z = _dot_nt(q, k)
            drop = _softplus(z)
            if diag:
                drop = jnp.where(strict, drop, 0.0)
            suffix = _dot(drop.astype(BF16), tri)
            log_w = z - suffix - c_ref[hh, row0:tq]
            if diag:
                log_w = jnp.where(strict, log_w, NEG_INF)
            a = jnp.exp(log_w)
            acc_ref[hh, row0:tq] += _dot(a.astype(BF16), v)
            c_ref[hh, row0:tq] += suffix[:, 0:1]

    acc_ref[...] = jnp.zeros_like(acc_ref)
    c_ref[...] = jnp.zeros_like(c_ref)
    for dd in range(n_diag - 1, -1, -1):
        block(i * n_diag + dd, dd * tk, True)

    def body(kk, carry):
        block(i * n_diag - 1 - kk, 0, False)
        return carry

    lax.fori_loop(0, i * n_diag, body, 0)
    o_ref[0] = jnp.concatenate([acc_ref[0], acc_ref[1]], axis=1).astype(BF16)


def _sb_prompt(pb, tq, tk):
    b, t, _ = pb.shape
    qb, kb, vb = OFF_Q_SB // LANES, OFF_K_SB // LANES, OFF_V_SB // LANES
    return pl.pallas_call(
        functools.partial(_sb_prompt_kernel, tq=tq, tk=tk),
        grid=(b, H_SB // 2, t // tq),
        in_specs=[
            pl.BlockSpec((1, tq, LANES), lambda bi, hp, i: (bi, i, qb + hp)),
            pl.BlockSpec((1, t, LANES), lambda bi, hp, i: (bi, 0, kb + hp)),
            pl.BlockSpec((1, t, LANES), lambda bi, hp, i: (bi, 0, vb + hp)),
        ],
        out_specs=pl.BlockSpec((1, tq, LANES), lambda bi, hp, i: (bi, i, hp)),
        out_shape=jax.ShapeDtypeStruct((b, t, D_SB), BF16),
        scratch_shapes=[pltpu.VMEM((2, tq, DH_SB), F32), pltpu.VMEM((2, tq, 1), F32)],
        compiler_params=_params(("parallel", "parallel", "arbitrary")),
        name="sb_prompt",
    )(pb, pb, pb)


def _diff_prompt_kernel(lam_ref, rel_ref, q_ref, k_ref, v_ref, tz_ref, g_ref, o_ref,
                        q2_ref, m_ref, acc_ref, z_ref, *, blk, lam_init):
    hp = pl.program_id(1)
    i = pl.program_id(2)
    r = lax.broadcasted_iota(I32, (2 * blk, blk), 0)
    c = lax.broadcasted_iota(I32, (2 * blk, blk), 1)
    causal = c <= jnp.where(r >= blk, r - blk, r)
    lane = lax.broadcasted_iota(I32, (blk, DV_DIFF), 1)
    ones_col = (lane == 0).astype(BF16)
    scale = DQ_DIFF ** -0.5
    for hh in range(2):
        q = q_ref[0, :, hh * DV_DIFF:(hh + 1) * DV_DIFF]
        zero = jnp.zeros_like(q)
        q2_ref[hh, 0:blk, :] = jnp.where(lane < DQ_DIFF, q, zero)
        q2_ref[hh, blk:2 * blk, :] = jnp.where(lane >= DQ_DIFF, q, zero)
    m_ref[...] = jnp.full_like(m_ref, NEG_INF)
    acc_ref[...] = jnp.zeros_like(acc_ref)

    def logits(j, buf):
        start = pl.multiple_of(j * blk, blk)
        for hh in range(2):
            k = k_ref[0, pl.ds(start, blk), hh * DV_DIFF:(hh + 1) * DV_DIFF]
            z_ref[buf, hh] = _dot_nt(q2_ref[hh], k)

    def consume(j, buf, kind):
        start = pl.multiple_of(j * blk, blk)
        for hh in range(2):
            lo = hh * DV_DIFF
            v = jnp.concatenate([v_ref[0, pl.ds(start, blk), lo:lo + DV_DIFF], ones_col], axis=1)
            z = z_ref[buf, hh] * scale
            if kind == 2:
                z = z + rel_ref[N_BUCKETS - 1, hp * 2 + hh]
            else:
                bias = tz_ref[hh, kind]
                z = z + jnp.concatenate([bias, bias], axis=0)
            if kind == 0:
                z = jnp.where(causal, z, NEG_INF)
            m_prev = m_ref[hh]
            m_new = jnp.maximum(m_prev, jnp.max(z, axis=1, keepdims=True))
            alpha = jnp.exp(m_prev - m_new)
            p = jnp.exp(z - m_new)
            acc_ref[hh] = alpha * acc_ref[hh] + _dot(p.astype(BF16), v)
            m_ref[hh] = m_new

    logits(i, 0)
    logits(jnp.maximum(i - 1, 0), 1)
    consume(i, 0, 0)

    @pl.when(i >= 1)
    def _():
        logits(jnp.maximum(i - 2, 0), 0)
        consume(i - 1, 1, 1)

    n_far = jnp.maximum(i - 1, 0)

    def pair(p, carry):
        ja = i - 2 - 2 * p
        logits(ja - 1, 1)
        consume(ja, 0, 2)
        logits(jnp.maximum(ja - 2, 0), 0)
        consume(ja - 1, 1, 2)
        return carry

    lax.fori_loop(0, n_far // 2, pair, 0)

    @pl.when(n_far % 2 == 1)
    def _():
        consume(0, 0, 2)

    lam = lam_ref[0, 0]
    outs = []
    for hh in range(2):
        acc = acc_ref[hh]
        o = acc[:, 0:DV_DIFF] / acc[:, DV_DIFF:DV_DIFF + 1]
        o = o[0:blk] - lam * o[blk:2 * blk]
        outs.append(_rmsnorm(o, g_ref[...]) * (1.0 - lam_init))
    o_ref[0] = jnp.concatenate(outs, axis=1).astype(BF16)


def _diff_prompt(pb, tz, lam, rel_bias, g_diff, blk, lam_init):
    b, t, _ = pb.shape
    qb, kb, vb = OFF_Q_DF // LANES, OFF_K_DF // LANES, OFF_V_DF // LANES
    return pl.pallas_call(
        functools.partial(_diff_prompt_kernel, blk=blk, lam_init=lam_init),
        grid=(b, H_DIFF // 2, t // blk),
        in_specs=[
            pl.BlockSpec(memory_space=pltpu.SMEM),
            pl.BlockSpec(memory_space=pltpu.SMEM),
            pl.BlockSpec((1, blk, LANES), lambda bi, hp, i: (bi, i, qb + hp)),
            pl.BlockSpec((1, t, LANES), lambda bi, hp, i: (bi, 0, kb + hp)),
            pl.BlockSpec((1, t, LANES), lambda bi, hp, i: (bi, 0, vb + hp)),
            pl.BlockSpec((2, 2, blk, blk), lambda bi, hp, i: (hp, 0, 0, 0)),
            pl.BlockSpec((1, DV_DIFF), lambda bi, hp, i: (0, 0)),
        ],
        out_specs=pl.BlockSpec((1, blk, LANES), lambda bi, hp, i: (bi, i, hp)),
        out_shape=jax.ShapeDtypeStruct((b, t, D_DIFF), BF16),
        scratch_shapes=[
            pltpu.VMEM((2, 2 * blk, DV_DIFF), BF16),
            pltpu.VMEM((2, 2 * blk, 1), F32),
            pltpu.VMEM((2, 2 * blk, 2 * DV_DIFF), F32),
            pltpu.VMEM((2, 2, 2 * blk, blk), F32),
        ],
        compiler_params=_params(("parallel", "parallel", "arbitrary"), vmem=DIFF_VMEM_LIMIT),
        name="diff_prompt",
    )(lam, rel_bias, pb, pb, pb, tz, g_diff.reshape(1, DV_DIFF))


def _mem_attn_kernel(q_ref, k_ref, v_ref, o_ref, *, tq):
    rows = max(tq, SUBLANES)
    q_all = q_ref[0].astype(F32) * (DH_MEM ** -0.5)
    if rows > tq:
        q_all = jnp.concatenate([q_all, jnp.zeros((rows - tq, D_MEM), F32)], axis=0)
    q_all = q_all.astype(BF16)
    outs = []
    for h in range(H_MEM):
        lo = h * DH_MEM
        q = q_all[:, lo:lo + DH_MEM]
        k = k_ref[0, :, lo:lo + DH_MEM].astype(BF16)
        v = v_ref[0, :, lo:lo + DH_MEM].astype(BF16)
        z = _dot_nt(q, k)
        p = jnp.exp(z - jnp.max(z, axis=1, keepdims=True))
        outs.append(_dot(p.astype(BF16), v) / jnp.sum(p, axis=1, keepdims=True))
    o_ref[0] = jnp.concatenate(outs, axis=1)[0:tq]


def _mem_attn(q, q_block, k, k_block, v, v_block, tq, name):
    b, t, _ = q.shape
    m = k.shape[1]
    return pl.pallas_call(
        functools.partial(_mem_attn_kernel, tq=tq),
        grid=(b, t // tq),
        in_specs=[
            pl.BlockSpec((1, tq, D_MEM), lambda bi, i: (bi, i, q_block)),
            pl.BlockSpec((1, m, D_MEM), lambda bi, i: (bi, 0, k_block)),
            pl.BlockSpec((1, m, D_MEM), lambda bi, i: (bi, 0, v_block)),
        ],
        out_specs=pl.BlockSpec((1, tq, D_MEM), lambda bi, i: (bi, i, 0)),
        out_shape=jax.ShapeDtypeStruct((b, t, D_MEM), F32),
        compiler_params=_params(("parallel", "parallel")),
        name=name,
    )(q, k, v)


def _pad_rows(x, rows):
    return jnp.concatenate([x, jnp.zeros((rows - x.shape[0], x.shape[1]), x.dtype)], axis=0)


def _query_columns(q, n_heads, width):
    t = q.shape[0]
    rows = jnp.concatenate(
        [jnp.broadcast_to(q[i:i + 1], (SUBLANES, q.shape[1])) for i in range(t)], axis=0)
    r = lax.broadcasted_iota(I32, rows.shape, 0)
    c = lax.broadcasted_iota(I32, rows.shape, 1)
    rows = jnp.where(jnp.right_shift(c, _log2(width)) == (r & (SUBLANES - 1)), rows, 0.0)
    return rows


def _suffix_sum_lanes(x):
    lane = lax.broadcasted_iota(I32, x.shape, 1)
    shift = 1
    while shift < PAGE_SIZE:
        x = x + jnp.where(lane < PAGE_SIZE - shift, pltpu.roll(x, PAGE_SIZE - shift, axis=1), 0.0)
        shift *= 2
    return x


def _head_rows_to_tokens(acc_t, n_tok, width):
    rows = _pad_lanes(acc_t, LANES).T[0:n_tok * SUBLANES]
    rr = lax.broadcasted_iota(I32, rows.shape, 0)
    cc = lax.broadcasted_iota(I32, rows.shape, 1)
    rows = jnp.where(jnp.right_shift(cc, _log2(width)) == (rr & (SUBLANES - 1)), rows, 0.0)
    return rows


def _pad_lanes(x, lanes):
    return jnp.concatenate([x, jnp.zeros((x.shape[0], lanes - x.shape[1]), x.dtype)], axis=1)


def _sb_decode_kernel(pt_ref, q_ref, kn_ref, vn_ref, *refs, n_tok, n_pages, pps):
    kt_refs = refs[:pps]
    vt_refs = refs[pps:2 * pps]
    o_ref, qt_ref, acc_ref, c_ref = refs[2 * pps:]
    j = pl.program_id(1)
    n_rows = n_tok * SUBLANES

    def attend(z_pages, vt, masks):
        drops, suffixes = [], []
        for z, mask in zip(z_pages, masks):
            drop = _softplus(z)
            if mask is not None:
                drop = jnp.where(mask, drop, 0.0)
            suffixes.append(_suffix_sum_lanes(drop))
        carry = c_ref[...]
        weights = []
        for z, suffix, mask in zip(z_pages, suffixes, masks):
            log_w = z - suffix - carry
            if mask is not None:
                log_w = jnp.where(mask, log_w, NEG_INF)
            weights.append(jnp.exp(log_w).astype(BF16))
            carry = carry + suffix[:, 0:1]
        c_ref[...] = carry
        acc_ref[...] += _dot_nt(vt, jnp.concatenate(weights, axis=1))

    @pl.when(j == 0)
    def _():
        q = q_ref[0] * (DH_SB ** -0.5)
        qt_ref[...] = _query_columns(q, H_SB, DH_SB).astype(BF16)
        acc_ref[...] = jnp.zeros_like(acc_ref)
        c_ref[...] = jnp.zeros_like(c_ref)
        kn = _pad_rows(kn_ref[0], PAGE_SIZE).astype(BF16)
        vn_t = _pad_rows(vn_ref[0], PAGE_SIZE).T.astype(BF16)
        row = lax.broadcasted_iota(I32, (n_rows, PAGE_SIZE), 0)
        key = lax.broadcasted_iota(I32, (n_rows, PAGE_SIZE), 1)
        attend([_dot_nt(qt_ref[...], kn)], vn_t, [key < jnp.right_shift(row, SUBLANE_SHIFT)])

    kt = jnp.concatenate([ref[0].astype(BF16) for ref in kt_refs], axis=1)
    vt = jnp.concatenate([ref[0].astype(BF16) for ref in vt_refs], axis=1)
    z = _dot(qt_ref[...], kt)
    attend([z[:, p * PAGE_SIZE:(p + 1) * PAGE_SIZE] for p in range(pps)], vt,
           [None] * pps)

    @pl.when(j == n_pages // pps - 1)
    def _():
        rows = _head_rows_to_tokens(acc_ref[...], n_tok, DH_SB)
        o_ref[0] = jnp.sum(rows.reshape(n_tok, SUBLANES, D_SB), axis=1)


def _page_specs(width, n_pages, pps):
    def spec(p):
        return pl.BlockSpec(
            (1, width, PAGE_SIZE),
            lambda bi, j, pt: (pt[bi, n_pages - 1 - (j * pps + p)], 0, 0))
    return [spec(p) for p in range(pps)]


def _sb_decode(q, k_new, v_new, cache_kt, cache_vt, page_table):
    b, t, _ = q.shape
    n_pages = page_table.shape[1]
    pps = SB_PAGES_PER_STEP
    tok = pl.BlockSpec((1, t, D_SB), lambda bi, j, pt: (bi, 0, 0))
    grid_spec = pltpu.PrefetchScalarGridSpec(
        num_scalar_prefetch=1,
        grid=(b, n_pages // pps),
        in_specs=[tok, tok, tok] + _page_specs(D_SB, n_pages, pps) + _page_specs(D_SB, n_pages, pps),
        out_specs=tok,
        scratch_shapes=[
            pltpu.VMEM((t * SUBLANES, D_SB), BF16),
            pltpu.VMEM((D_SB, t * SUBLANES), F32),
            pltpu.VMEM((t * SUBLANES, 1), F32),
        ],
    )
    return pl.pallas_call(
        functools.partial(_sb_decode_kernel, n_tok=t, n_pages=n_pages, pps=pps),
        grid_spec=grid_spec,
        out_shape=jax.ShapeDtypeStruct((b, t, D_SB), F32),
        compiler_params=_params(("parallel", "arbitrary")),
        name="sb_decode",
    )(page_table, q, k_new, v_new, *([cache_kt] * pps), *([cache_vt] * pps))


def _diff_decode_kernel(pt_ref, lam_ref, q_ref, kn_ref, vn_ref, bias_ref, g_ref, *refs,
                        n_tok, n_pages, lam_init, pps):
    kt_refs = refs[:pps]
    vt_refs = refs[pps:2 * pps]
    o_ref, qt_ref, acc_ref, m_ref, l_ref = refs[2 * pps:]
    j = pl.program_id(1)
    half = n_tok * SUBLANES
    scale = DQ_DIFF ** -0.5

    def attend(z, vt):
        m_prev = m_ref[...]
        m_new = jnp.maximum(m_prev, jnp.max(z, axis=1, keepdims=True))
        alpha = jnp.exp(m_prev - m_new)
        p = jnp.exp(z - m_new)
        l_ref[...] = alpha * l_ref[...] + jnp.sum(p, axis=1, keepdims=True)
        acc_ref[...] = (acc_ref[...] * _col_to_row(alpha)[:, 0:2 * half]
                        + _dot_nt(vt, p.astype(BF16)))
        m_ref[...] = m_new

    @pl.when(j == 0)
    def _():
        q = _query_columns(q_ref[0], H_DIFF, 2 * DQ_DIFF)
        lane = lax.broadcasted_iota(I32, q.shape, 1)
        first = (jnp.right_shift(lane, _log2(DQ_DIFF)) & 1) == 0
        q2 = jnp.concatenate([jnp.where(first, q, 0.0), jnp.where(first, 0.0, q)], axis=0)
        qt_ref[...] = q2.astype(BF16)
        acc_ref[...] = jnp.zeros_like(acc_ref)
        m_ref[...] = jnp.full_like(m_ref, NEG_INF)
        l_ref[...] = jnp.zeros_like(l_ref)
        kn = _pad_rows(kn_ref[0], PAGE_SIZE).astype(BF16)
        vn_t = _pad_rows(vn_ref[0], PAGE_SIZE).T.astype(BF16)
        row = lax.broadcasted_iota(I32, (2 * half, PAGE_SIZE), 0)
        key = lax.broadcasted_iota(I32, (2 * half, PAGE_SIZE), 1)
        z_new = _dot_nt(qt_ref[...], kn) * scale + bias_ref[1, 0:2 * half]
        attend(jnp.where(key <= (jnp.right_shift(row, SUBLANE_SHIFT) & (n_tok - 1)), z_new, NEG_INF), vn_t)

    kt = jnp.concatenate([ref[0].astype(BF16) for ref in kt_refs], axis=1)
    vt = jnp.concatenate([ref[0].astype(BF16) for ref in vt_refs], axis=1)
    far = bias_ref[2, 0:2 * half]
    near = jnp.where(j == 0, bias_ref[0, 0:2 * half], far)
    bias = jnp.concatenate([near] + [far] * (pps - 1), axis=1)
    attend(_dot(qt_ref[...], kt) * scale + bias, vt)

    @pl.when(j == n_pages // pps - 1)
    def _():
        rows = _pad_lanes(acc_ref[...], LANES).T[0:2 * half] / l_ref[...]
        o = rows[0:half] - lam_ref[0, 0] * rows[half:2 * half]
        rr = lax.broadcasted_iota(I32, o.shape, 0)
        cc = lax.broadcasted_iota(I32, o.shape, 1)
        o = jnp.where(jnp.right_shift(cc, _log2(DV_DIFF)) == (rr & (SUBLANES - 1)), o, 0.0)
        ms = jnp.sum(o * o, axis=1, keepdims=True) * (1.0 / DV_DIFF)
        o = (o * lax.rsqrt(ms + EPS)) * g_ref[...] * (1.0 - lam_init)
        o_ref[0] = jnp.sum(o.reshape(n_tok, SUBLANES, D_DIFF), axis=1)


def _diff_decode(q, k_new, v_new, cache_kt, cache_vt, page_table, lam, dec_bias, g_tiled, lam_init):
    b, t, _ = q.shape
    n_pages = page_table.shape[1]
    pps = DIFF_PAGES_PER_STEP
    tok = pl.BlockSpec((1, t, D_DIFF), lambda bi, j, pt: (bi, 0, 0))
    grid_spec = pltpu.PrefetchScalarGridSpec(
        num_scalar_prefetch=1,
        grid=(b, n_pages // pps),
        in_specs=[
            pl.BlockSpec(memory_space=pltpu.SMEM),
            tok, tok, tok,
            pl.BlockSpec((3, LANES, PAGE_SIZE), lambda bi, j, pt: (0, 0, 0)),
            pl.BlockSpec((1, D_DIFF), lambda bi, j, pt: (0, 0)),
        ] + _page_specs(D_QK_DIFF, n_pages, pps) + _page_specs(D_DIFF, n_pages, pps),
        out_specs=tok,
        scratch_shapes=[
            pltpu.VMEM((2 * t * SUBLANES, D_QK_DIFF), BF16),
            pltpu.VMEM((D_DIFF, 2 * t * SUBLANES), F32),
            pltpu.VMEM((2 * t * SUBLANES, 1), F32),
            pltpu.VMEM((2 * t * SUBLANES, 1), F32),
        ],
    )
    return pl.pallas_call(
        functools.partial(_diff_decode_kernel, n_tok=t, n_pages=n_pages, lam_init=lam_init, pps=pps),
        grid_spec=grid_spec,
        out_shape=jax.ShapeDtypeStruct((b, t, D_DIFF), F32),
        compiler_params=_params(("parallel", "arbitrary")),
        name="diff_decode",
    )(page_table, lam, q, k_new, v_new, dec_bias, g_tiled,
      *([cache_kt] * pps), *([cache_vt] * pps))


def _finish_kernel(x_ref, osb_ref, odf_ref, omem_ref, gmix_ref, wg_ref, bg_ref, wsb_ref, wdf_ref,
                   wmem_ref, wout_ref, gffn_ref, wr_ref, br_ref, xmid_ref, xn_ref, route_ref, count_ref,
                   carry_ref):
    x = x_ref[...]
    u = _rmsnorm(x, gmix_ref[...]).astype(BF16)
    gates = jax.nn.sigmoid(_dot(u, wg_ref[...]) + bg_ref[...])
    h = (gates[:, 0:D_MODEL] * _dot(osb_ref[...].astype(BF16), wsb_ref[...])
         + gates[:, D_MODEL:2 * D_MODEL] * _dot(odf_ref[...].astype(BF16), wdf_ref[...])
         + gates[:, 2 * D_MODEL:3 * D_MODEL] * _dot(omem_ref[...].astype(BF16), wmem_ref[...]))
    xm = x + _dot(h.astype(BF16), wout_ref[...])
    xmid_ref[...] = xm
    xn = _rmsnorm(xm, gffn_ref[...])
    xn_ref[...] = xn
    xh, xl = _split_bf16(xn)
    wh, wl = _split_bf16(wr_ref[...])
    lg = _dot(xh, wh) + _dot(xh, wl) + _dot(xl, wh) + br_ref[...]
    lane = lax.broadcasted_iota(I32, lg.shape, 1)
    is_group = lane < N_GROUPS
    gl = jnp.where(is_group, lg, -jnp.inf)
    gmax = jnp.max(gl, axis=1, keepdims=True)
    grp = jnp.min(jnp.where(gl == gmax, lane, LANES), axis=1, keepdims=True)
    p_grp = 1.0 / jnp.sum(jnp.where(is_group, jnp.exp(gl - gmax), 0.0), axis=1, keepdims=True)
    in_group = (lane >= N_GROUPS) & (lane < N_GROUPS + N_EXPERTS) & (
        jnp.right_shift(lane - N_GROUPS, _log2(EXPERTS_PER_GROUP)) == grp)
    el = jnp.where(in_group, lg, -jnp.inf)
    v1 = jnp.max(el, axis=1, keepdims=True)
    i1 = jnp.min(jnp.where(el == v1, lane, LANES), axis=1, keepdims=True)
    el2 = jnp.where(lane == i1, -jnp.inf, el)
    v2 = jnp.max(el2, axis=1, keepdims=True)
    i2 = jnp.min(jnp.where(el2 == v2, lane, LANES), axis=1, keepdims=True)
    e = jnp.exp(v2 - v1)
    w1 = (1.0 / (1.0 + e)) * p_grp
    w2 = (e / (1.0 + e)) * p_grp
    @pl.when(pl.program_id(0) == 0)
    def _():
        carry_ref[...] = jnp.zeros_like(carry_ref)

    tm = lg.shape[0]
    oh1 = lane == i1 - N_GROUPS
    oh2 = lane == i2 - N_GROUPS
    both = jnp.where(oh1 | oh2, 1.0, 0.0)
    r = lax.broadcasted_iota(I32, (tm, tm), 0)
    c = lax.broadcasted_iota(I32, (tm, tm), 1)
    before = _dot((c < r).astype(BF16), both.astype(BF16)) + carry_ref[...]
    rank1 = jnp.sum(jnp.where(oh1, before, 0.0), axis=1, keepdims=True)
    rank2 = jnp.sum(jnp.where(oh2, before, 0.0), axis=1, keepdims=True)
    carry_ref[...] += jnp.sum(both, axis=0, keepdims=True)
    count_ref[...] = jnp.broadcast_to(carry_ref[...], count_ref.shape)

    rec = jnp.where(lane == R_EID1, (i1 - N_GROUPS).astype(F32), 0.0)
    rec = jnp.where(lane == R_EID2, (i2 - N_GROUPS).astype(F32), rec)
    rec = jnp.where(lane == R_W1, w1, rec)
    rec = jnp.where(lane == R_W2, w2, rec)
    rec = jnp.where(lane == R_RANK1, rank1, rec)
    rec = jnp.where(lane == R_RANK2, rank2, rec)
    route_ref[...] = rec


def _finish(x, o_sb, o_df, o_mem, g_mix, wg, bg, wsb, wdf, wmem, wout, g_ffn, wr, br, name):
    n, d = x.shape
    tm = min(FINISH_TILE, n)

    def rows(width):
        return pl.BlockSpec((tm, width), lambda i: (i, 0))

    def whole(a):
        return pl.BlockSpec(a.shape, lambda i: (0, 0))

    args = (x, o_sb, o_df, o_mem, g_mix.reshape(1, d), wg, bg.reshape(1, -1), wsb, wdf, wmem, wout,
            g_ffn.reshape(1, d), wr, br)
    in_specs = [rows(d), rows(D_SB), rows(D_DIFF), rows(D_MEM)] + [whole(a) for a in args[4:]]
    return pl.pallas_call(
        _finish_kernel,
        grid=(n // tm,),
        in_specs=in_specs,
        out_specs=[rows(d), rows(d), rows(LANES), pl.BlockSpec((SUBLANES, LANES), lambda i: (0, 0))],
        out_shape=[
            jax.ShapeDtypeStruct((n, d), F32),
            jax.ShapeDtypeStruct((n, d), F32),
            jax.ShapeDtypeStruct((n, LANES), F32),
            jax.ShapeDtypeStruct((SUBLANES, LANES), F32),
        ],
        scratch_shapes=[pltpu.VMEM((1, LANES), F32)],
        compiler_params=_params(("arbitrary",)),
        name=name,
    )(*args)


def _one_hots(route):
    lane = lax.broadcasted_iota(I32, route.shape, 1)
    oh1 = lane == route[:, R_EID1:R_EID1 + 1].astype(I32)
    oh2 = lane == route[:, R_EID2:R_EID2 + 1].astype(I32)
    return oh1, oh2


def _moe_dest_kernel(route_ref, count_ref, dest_ref, blk_ref):
    n_blk = blk_ref.shape[0]
    blocks = jnp.right_shift(count_ref[...].astype(I32) + (MOE_ROWS - 1), MOE_ROWS_LOG2).astype(F32)
    r = lax.broadcasted_iota(I32, (LANES, LANES), 0)
    c = lax.broadcasted_iota(I32, (LANES, LANES), 1)
    upto = (r <= c).astype(BF16)
    block_end = _dot(blocks.astype(BF16), upto)
    row_start = (block_end - blocks)[0:1] * float(MOE_ROWS)
    route = route_ref[...]
    oh1, oh2 = _one_hots(route)
    d1 = jnp.sum(jnp.where(oh1, row_start, 0.0), axis=1, keepdims=True) + route[:, R_RANK1:R_RANK1 + 1]
    d2 = jnp.sum(jnp.where(oh2, row_start, 0.0), axis=1, keepdims=True) + route[:, R_RANK2:R_RANK2 + 1]
    lane = lax.broadcasted_iota(I32, route.shape, 1)
    dest_ref[...] = jnp.where(lane == 0, d1, jnp.where(lane == 1, d2, 0.0)).astype(I32)

    @pl.when(pl.program_id(0) == 0)
    def _():
        b_idx = lax.broadcasted_iota(I32, (n_blk, LANES), 0).astype(F32)
        lane_b = lax.broadcasted_iota(I32, (n_blk, LANES), 1)
        done = (block_end[0:1] <= b_idx) & (lane_b < N_EXPERTS)
        expert = jnp.minimum(jnp.sum(jnp.where(done, 1.0, 0.0), axis=1, keepdims=True),
                             float(N_EXPERTS - 1))
        used = block_end[0:1, N_EXPERTS - 1:N_EXPERTS]
        blk_ref[...] = jnp.where(lane_b == 0, expert, jnp.where(lane_b == 1, used, 0.0)).astype(I32)


def _moe_dest(route, counts, n_blk):
    n = route.shape[0]
    tm = min(DEST_TILE, n)
    n_blk_pad = -(-n_blk // SUBLANES) * SUBLANES
    return pl.pallas_call(
        _moe_dest_kernel,
        grid=(n // tm,),
        in_specs=[pl.BlockSpec((tm, LANES), lambda i: (i, 0)),
                  pl.BlockSpec((SUBLANES, LANES), lambda i: (0, 0))],
        out_specs=[pl.BlockSpec((tm, LANES), lambda i: (i, 0)),
                   pl.BlockSpec((n_blk_pad, LANES), lambda i: (0, 0))],
        out_shape=[jax.ShapeDtypeStruct((n, LANES), I32),
                   jax.ShapeDtypeStruct((n_blk_pad, LANES), I32)],
        compiler_params=_params(("arbitrary",)),
        name="moe_dest",
    )(route, counts)


def _dispatch_kernel(dest_ref, x_ref, init_ref, xs_ref, sem):
    del init_ref
    tm = x_ref.shape[0]

    def row_copy(t, d):
        return pltpu.make_async_copy(x_ref.at[pl.ds(t, 1)], xs_ref.at[pl.ds(d, 1)], sem)

    def start(t, carry):
        row_copy(t, dest_ref[0, 0, 2 * t]).start(priority=0)
        row_copy(t, dest_ref[0, 0, 2 * t + 1]).start(priority=1)
        return carry

    def wait(t, carry):
        row_copy(t, dest_ref[0, 0, 2 * t]).wait()
        row_copy(t, dest_ref[0, 0, 2 * t + 1]).wait()
        return carry

    lax.fori_loop(0, tm, start, 0, unroll=ROW_DMA_UNROLL)
    lax.fori_loop(0, tm, wait, 0, unroll=ROW_DMA_UNROLL)


def _dispatch(dest_tiles, xn, n_rows):
    n, d = xn.shape
    tm = dest_tiles.shape[2] // 2
    return pl.pallas_call(
        _dispatch_kernel,
        grid=(n // tm,),
        in_specs=[
            pl.BlockSpec((1, 1, 2 * tm), lambda i: (i, 0, 0), memory_space=pltpu.SMEM),
            pl.BlockSpec((tm, d), lambda i: (i, 0)),
            pl.BlockSpec(memory_space=pl.ANY),
        ],
        out_specs=pl.BlockSpec(memory_space=pl.ANY),
        out_shape=jax.ShapeDtypeStruct((n_rows, d), F32),
        scratch_shapes=[pltpu.SemaphoreType.DMA(())],
        input_output_aliases={2: 0},
        compiler_params=_params(("arbitrary",)),
        name="moe_dispatch",
    )(dest_tiles, xn, jnp.zeros((n_rows, d), F32))


def _expert_kernel(be_ref, used_ref, xs_ref, wg_ref, wu_ref, wd_ref, y_ref, wg_b, wu_b, wd_b):
    b = pl.program_id(0)

    @pl.when(b < used_ref[0])
    def _():
        @pl.when((b == 0) | (be_ref[b] != be_ref[jnp.maximum(b - 1, 0)]))
        def _():
            wg_b[...] = wg_ref[0].astype(BF16)
            wu_b[...] = wu_ref[0].astype(BF16)
            wd_b[...] = wd_ref[0].astype(BF16)

        x = xs_ref[...].astype(BF16)
        h = jax.nn.silu(_dot(x, wg_b[...])) * _dot(x, wu_b[...])
        y_ref[...] = _dot(h.astype(BF16), wd_b[...])

    @pl.when(b >= used_ref[0])
    def _():
        y_ref[...] = jnp.zeros_like(y_ref)


def _experts(blk_exp, used, xs, wg, wu, wd):
    n_rows, d = xs.shape
    n_blk = n_rows // MOE_ROWS
    grid_spec = pltpu.PrefetchScalarGridSpec(
        num_scalar_prefetch=2,
        grid=(n_blk,),
        in_specs=[
            pl.BlockSpec((MOE_ROWS, d), lambda b, be, used: (b, 0)),
            pl.BlockSpec((1, d, D_EXPERT), lambda b, be, used: (be[b], 0, 0)),
            pl.BlockSpec((1, d, D_EXPERT), lambda b, be, used: (be[b], 0, 0)),
            pl.BlockSpec((1, D_EXPERT, d), lambda b, be, used: (be[b], 0, 0)),
        ],
        out_specs=pl.BlockSpec((MOE_ROWS, d), lambda b, be, used: (b, 0)),
        scratch_shapes=[pltpu.VMEM((d, D_EXPERT), BF16), pltpu.VMEM((d, D_EXPERT), BF16),
                        pltpu.VMEM((D_EXPERT, d), BF16)],
    )
    return pl.pallas_call(
        _expert_kernel,
        grid_spec=grid_spec,
        out_shape=jax.ShapeDtypeStruct((n_rows, d), F32),
        compiler_params=_params(("arbitrary",)),
        name="moe_experts",
    )(blk_exp, used, xs, wg, wu, wd)


def _combine_kernel(dest_ref, next_ref, route_ref, xmid_ref, g_ref, yb_ref, out_ref, buf_ref, sems):
    i = pl.program_id(0)
    tm = xmid_ref.shape[0]
    slot = i % 2

    def row_copy(s, t, k, d):
        return pltpu.make_async_copy(yb_ref.at[pl.ds(d, 1)], buf_ref.at[s, k, pl.ds(t, 1)], sems.at[s])

    def request(idx_ref, s):
        def start(t, carry):
            row_copy(s, t, 0, idx_ref[0, 0, 2 * t]).start(priority=0)
            row_copy(s, t, 1, idx_ref[0, 0, 2 * t + 1]).start(priority=1)
            return carry

        lax.fori_loop(0, tm, start, 0, unroll=ROW_DMA_UNROLL)

    @pl.when(i == 0)
    def _():
        request(dest_ref, 0)

    @pl.when(i + 1 < pl.num_programs(0))
    def _():
        request(next_ref, 1 - slot)

    def wait(t, carry):
        row_copy(slot, t, 0, 0).wait()
        row_copy(slot, t, 1, 0).wait()
        return carry

    lax.fori_loop(0, tm, wait, 0, unroll=ROW_DMA_UNROLL)
    route = route_ref[...]
    y = buf_ref[slot, 0] * route[:, R_W1:R_W1 + 1] + buf_ref[slot, 1] * route[:, R_W2:R_W2 + 1]
    out_ref[...] = _rmsnorm(xmid_ref[...] + y, g_ref[...])


def _combine(dest_tiles, route, xmid, g_final, yb):
    n, d = xmid.shape
    tm = dest_tiles.shape[2] // 2
    last = n // tm - 1
    return pl.pallas_call(
        _combine_kernel,
        grid=(n // tm,),
        in_specs=[
            pl.BlockSpec((1, 1, 2 * tm), lambda i: (i, 0, 0), memory_space=pltpu.SMEM),
            pl.BlockSpec((1, 1, 2 * tm), lambda i: (jnp.minimum(i + 1, last), 0, 0), memory_space=pltpu.SMEM),
            pl.BlockSpec((tm, LANES), lambda i: (i, 0)),
            pl.BlockSpec((tm, d), lambda i: (i, 0)),
            pl.BlockSpec((1, d), lambda i: (0, 0)),
            pl.BlockSpec(memory_space=pl.ANY),
        ],
        out_specs=pl.BlockSpec((tm, d), lambda i: (i, 0)),
        out_shape=jax.ShapeDtypeStruct((n, d), F32),
        scratch_shapes=[pltpu.VMEM((2, 2, tm, d), F32), pltpu.SemaphoreType.DMA((2,))],
        compiler_params=_params(("arbitrary",)),
        name="moe_combine",
    )(dest_tiles, dest_tiles, route, xmid, g_final.reshape(1, d), yb)


def _moe_and_final_norm(xmid, xn, route, counts, wg, wu, wd, g_final):
    n, d = xmid.shape
    n_blk = -(-(2 * n + N_EXPERTS * (MOE_ROWS - 1)) // MOE_ROWS)
    dest, blk = _moe_dest(route, counts, n_blk)
    tm = min(ROW_TILE, n)
    dest_tiles = dest[:, 0:2].reshape(n // tm, 1, 2 * tm)
    xs = _dispatch(dest_tiles, xn, n_blk * MOE_ROWS)
    yb = _experts(blk[0:n_blk, 0], blk[0, 1:2], xs, wg, wu, wd)
    return _combine(dest_tiles, route, xmid, g_final, yb)


def _col_scale():
    s = jnp.ones((1, D_IN), F32)
    return s.at[:, OFF_Q_SB:OFF_Q_SB + D_SB].set(DH_SB ** -0.5)


def kernel(x_prompt, x_sample, cache_sb_k, cache_sb_v, cache_diff_k, cache_diff_v, cache_mem_k, cache_mem_v, page_table, mem_prompt, norm_mix_g, w_in, diff_lam_q1, diff_lam_k1, diff_lam_q2, diff_lam_k2, diff_norm_g, mem_norm_g, w_mem_kv, w_gate, b_gate, w_br_sb, w_br_diff, w_br_mem, w_out, norm_ffn_g, w_router_group, b_router_group, w_router_expert, b_router_expert, w_exp_gate, w_exp_up, w_exp_down, rel_bias, norm_final_g):
    depth = w_in.shape[0]
    assert depth == 1, "single-layer stack only"
    assert page_table.shape[1] % SB_PAGES_PER_STEP == 0 and page_table.shape[1] % DIFF_PAGES_PER_STEP == 0
    b, t, d = x_prompt.shape
    bs, ts, _ = x_sample.shape
    n_mem = mem_prompt.shape[1]
    sb_tq, diff_blk = min(SB_QUERY_BLOCK, t), min(DIFF_BLOCK, t)
    lam_init = 0.8 - 0.6 * math.exp(-0.3 * 0)

    w_in_b = w_in[0].astype(BF16)
    wg_b, wsb_b, wdf_b = w_gate[0].astype(BF16), w_br_sb[0].astype(BF16), w_br_diff[0].astype(BF16)
    wmem_b, wout_b = w_br_mem[0].astype(BF16), w_out[0].astype(BF16)
    weg_b, weu_b, wed_b = w_exp_gate[0], w_exp_up[0], w_exp_down[0]
    pad = LANES - N_GROUPS - N_EXPERTS
    w_router = jnp.concatenate(
        [w_router_group[0], w_router_expert[0], jnp.zeros((d, pad), F32)], axis=1)
    b_router = jnp.concatenate(
        [b_router_group[0], b_router_expert[0], jnp.zeros((pad,), F32)]).reshape(1, LANES)
    lam_vecs = jnp.concatenate([diff_lam_q1, diff_lam_k1, diff_lam_q2, diff_lam_k2], axis=0)

    tz, dec_bias, lam_tile = _prep(rel_bias, lam_vecs, diff_blk, lam_init, ts)
    lam = lam_tile[0:1, 0:1]

    def heads_last(a_t, n_heads, width):
        bb, _, tt = a_t.shape
        return jnp.transpose(a_t.reshape(bb, n_heads, width, tt), (0, 3, 1, 2))[None]

    def keys_last(cache, n_heads, width):
        pool, page = cache.shape[1], cache.shape[2]
        return jnp.transpose(cache[0], (0, 2, 3, 1)).reshape(pool, n_heads * width, page)

    k_sb, v_sb, k_df, v_df, pb = _norm_proj(
        x_prompt, norm_mix_g[0], w_in_b, _col_scale(),
        [(OFF_K_SB, D_SB), (OFF_V_SB, D_SB), (OFF_K_DF, D_QK_DIFF), (OFF_V_DF, D_DIFF)], True,
        "proj_prompt")
    mk, mv, mem_b = _norm_proj(
        mem_prompt, mem_norm_g[0], w_mem_kv[0].astype(BF16),
        jnp.ones((1, 2 * D_MEM), F32), [(0, D_MEM), (D_MEM, D_MEM)], True, "proj_memory")
    o_sb = _sb_prompt(pb, sb_tq, SB_KEY_BLOCK)
    o_df = _diff_prompt(pb, tz, lam, rel_bias, diff_norm_g[0], diff_blk, lam_init)
    o_mem = _mem_attn(pb, OFF_Q_MEM // D_MEM, mem_b, 0, mem_b, 1, min(MEM_QUERY_BLOCK, t), "mem_prompt")
    xmid, xn, route, counts = _finish(
        x_prompt.reshape(b * t, d), o_sb.reshape(b * t, D_SB), o_df.reshape(b * t, D_DIFF),
        o_mem.reshape(b * t, D_MEM), norm_mix_g[0], wg_b, b_gate[0], wsb_b, wdf_b, wmem_b, wout_b,
        norm_ffn_g[0], w_router, b_router, "finish_prompt")
    y_prompt = _moe_and_final_norm(xmid, xn, route, counts, weg_b, weu_b, wed_b, norm_final_g)

    q_sb_s, k_sb_s, v_sb_s, q_df_s, k_df_s, v_df_s, q_mem_s, _ = _norm_proj(
        x_sample.reshape(1, bs * ts, d), norm_mix_g[0], w_in_b, jnp.ones((1, D_IN), F32),
        [(OFF_Q_SB, D_SB), (OFF_K_SB, D_SB), (OFF_V_SB, D_SB), (OFF_Q_DF, D_QK_DIFF),
         (OFF_K_DF, D_QK_DIFF), (OFF_V_DF, D_DIFF), (OFF_Q_MEM, D_MEM)], False, "proj_sample")
    o_sb_s = _sb_decode(
        q_sb_s.reshape(bs, ts, D_SB), k_sb_s.reshape(bs, ts, D_SB), v_sb_s.reshape(bs, ts, D_SB),
        keys_last(cache_sb_k, H_SB, DH_SB), keys_last(cache_sb_v, H_SB, DH_SB), page_table)
    o_df_s = _diff_decode(
        q_df_s.reshape(bs, ts, D_QK_DIFF), k_df_s.reshape(bs, ts, D_QK_DIFF),
        v_df_s.reshape(bs, ts, D_DIFF), keys_last(cache_diff_k, H_DIFF, 2 * DQ_DIFF),
        keys_last(cache_diff_v, H_DIFF, DV_DIFF), page_table, lam, dec_bias,
        jnp.tile(diff_norm_g[0], H_DIFF).reshape(1, D_DIFF), lam_init)
    o_mem_s = _mem_attn(
        q_mem_s.reshape(bs, ts, D_MEM), 0, cache_mem_k[0].reshape(bs, n_mem, D_MEM), 0,
        cache_mem_v[0].reshape(bs, n_mem, D_MEM), 0, ts, "mem_sample")
    xmid_s, xn_s, route_s, counts_s = _finish(
        x_sample.reshape(bs * ts, d), o_sb_s.reshape(bs * ts, D_SB), o_df_s.reshape(bs * ts, D_DIFF),
        o_mem_s.reshape(bs * ts, D_MEM), norm_mix_g[0], wg_b, b_gate[0], wsb_b, wdf_b, wmem_b, wout_b,
        norm_ffn_g[0], w_router, b_router, "finish_sample")
    y_sample = _moe_and_final_norm(xmid_s, xn_s, route_s, counts_s, weg_b, weu_b, wed_b, norm_final_g)

    return (y_prompt.reshape(b, t, d), y_sample.reshape(bs, ts, d),
            heads_last(k_sb, H_SB, DH_SB), heads_last(v_sb, H_SB, DH_SB),
            heads_last(k_df, H_DIFF, 2 * DQ_DIFF), heads_last(v_df, H_DIFF, DV_DIFF),
            heads_last(mk, H_MEM, DH_MEM), heads_last(mv, H_MEM, DH_MEM),
            k_sb_s.reshape(1, bs, ts, H_SB, DH_SB), v_sb_s.reshape(1, bs, ts, H_SB, DH_SB),
            k_df_s.reshape(1, bs, ts, H_DIFF, 2 * DQ_DIFF), v_df_s.reshape(1, bs, ts, H_DIFF, DV_DIFF))
```

```python
import functools
import math

import jax
import jax.numpy as jnp
from jax import lax
from jax.experimental import pallas as pl
from jax.experimental.pallas import tpu as pltpu

F32 = jnp.float32
BF16 = jnp.bfloat16
I32 = jnp.int32

D_MODEL = 1024
PAGE_SIZE = 128
H_SB = 8
DH_SB = 64
H_DIFF = 4
DQ_DIFF = 32
DV_DIFF = 64
H_MEM = 4
DH_MEM = 64
D_SB = H_SB * DH_SB
D_DIFF = H_DIFF * DV_DIFF
D_MEM = H_MEM * DH_MEM
D_QK_DIFF = H_DIFF * 2 * DQ_DIFF
D_IN = 3 * D_SB + 2 * D_QK_DIFF + D_DIFF + D_MEM
N_BUCKETS = 32
MAX_EXACT = 16
MAX_DISTANCE = 128
N_GROUPS = 4
EXPERTS_PER_GROUP = 8
N_EXPERTS = N_GROUPS * EXPERTS_PER_GROUP
D_EXPERT = 512
EPS = 1e-6
NEG_INF = -1e30

LANES = 128
SUBLANES = 8
VMEM_LIMIT = 48 * 1024 * 1024
DIFF_VMEM_LIMIT = 56 * 1024 * 1024

OFF_Q_SB = 0
OFF_K_SB = D_SB
OFF_V_SB = 2 * D_SB
OFF_Q_DF = 3 * D_SB
OFF_K_DF = OFF_Q_DF + D_QK_DIFF
OFF_V_DF = OFF_K_DF + D_QK_DIFF
OFF_Q_MEM = OFF_V_DF + D_DIFF

MEM_QUERY_BLOCK = 1024
SB_QUERY_BLOCK = 2048
SB_KEY_BLOCK = 256
DIFF_BLOCK = 512
MOE_ROWS = 256
TOK_TILE = 512
FINISH_TILE = 512
DEST_TILE = 1024
ROW_TILE = 512
ROW_DMA_UNROLL = 32
SB_PAGES_PER_STEP = 32
DIFF_PAGES_PER_STEP = 64

R_EID1, R_EID2, R_W1, R_W2, R_RANK1, R_RANK2 = 0, 1, 2, 3, 4, 5


def _log2(n):
    assert n > 0 and n & (n - 1) == 0, n
    return n.bit_length() - 1


SUBLANE_SHIFT = _log2(SUBLANES)
MOE_ROWS_LOG2 = _log2(MOE_ROWS)


def _params(sem, vmem=VMEM_LIMIT):
    return pltpu.CompilerParams(dimension_semantics=sem, vmem_limit_bytes=vmem)


def _rmsnorm(x, g):
    ms = jnp.mean(x * x, axis=-1, keepdims=True)
    return (x * lax.rsqrt(ms + EPS)) * g


def _dot(a, b):
    return jnp.dot(a, b, preferred_element_type=F32)


def _dot_nt(a, b):
    return lax.dot_general(a, b, (((1,), (1,)), ((), ())), preferred_element_type=F32)


def _split_bf16(x):
    hi = x.astype(BF16)
    lo = (x - hi.astype(F32)).astype(BF16)
    return hi, lo


def _softplus(z):
    neg_abs = pltpu.bitcast(pltpu.bitcast(z, jnp.uint32) | jnp.uint32(0x80000000), F32)
    return jnp.maximum(z, 0.0) + jnp.log(1.0 + jnp.exp(neg_abs))


def _col_to_row(v):
    n = v.shape[0]
    r = lax.broadcasted_iota(I32, (n, LANES), 0)
    c = lax.broadcasted_iota(I32, (n, LANES), 1)
    return jnp.sum(jnp.where(r == c, v, 0.0), axis=0, keepdims=True)


def _norm_proj_kernel(x_ref, g_ref, w_ref, s_ref, *out_refs, f32_cols, transposed):
    u = _rmsnorm(x_ref[0], g_ref[...]).astype(BF16)
    p = _dot(u, w_ref[...])
    for ref, (lo, width) in zip(out_refs[:-1], f32_cols):
        ref[0] = p[:, lo:lo + width].T if transposed else p[:, lo:lo + width]
    out_refs[-1][0] = (p * s_ref[...]).astype(BF16)


def _norm_proj(x, g, w_bf16, col_scale, f32_cols, transposed, name):
    b, t, d = x.shape
    n_out = w_bf16.shape[1]
    tm = min(TOK_TILE, t)
    if transposed:
        out_shape = [jax.ShapeDtypeStruct((b, width, t), F32) for _, width in f32_cols]
        out_specs = [pl.BlockSpec((1, width, tm), lambda bi, i: (bi, 0, i)) for _, width in f32_cols]
    else:
        out_shape = [jax.ShapeDtypeStruct((b, t, width), F32) for _, width in f32_cols]
        out_specs = [pl.BlockSpec((1, tm, width), lambda bi, i: (bi, i, 0)) for _, width in f32_cols]
    out_shape.append(jax.ShapeDtypeStruct((b, t, n_out), BF16))
    out_specs.append(pl.BlockSpec((1, tm, n_out), lambda bi, i: (bi, i, 0)))
    return pl.pallas_call(
        functools.partial(_norm_proj_kernel, f32_cols=tuple(f32_cols), transposed=transposed),
        grid=(b, t // tm),
        in_specs=[
            pl.BlockSpec((1, tm, d), lambda bi, i: (bi, i, 0)),
            pl.BlockSpec((1, d), lambda bi, i: (0, 0)),
            pl.BlockSpec((d, n_out), lambda bi, i: (0, 0)),
            pl.BlockSpec((1, n_out), lambda bi, i: (0, 0)),
        ],
        out_specs=out_specs,
        out_shape=out_shape,
        compiler_params=_params(("parallel", "parallel")),
        name=name,
    )(x, g.reshape(1, d), w_bf16, col_scale)


def _t5_bucket(delta):
    n = jnp.maximum(delta, 0)
    nf = jnp.maximum(n, 1).astype(F32)
    large = MAX_EXACT + (jnp.log(nf / MAX_EXACT) / math.log(MAX_DISTANCE / MAX_EXACT)
                         * (N_BUCKETS - MAX_EXACT)).astype(I32)
    large = jnp.minimum(large, N_BUCKETS - 1)
    return jnp.where(n < MAX_EXACT, n, large)


def _bias_of_bucket(bucket, rel_ref, head):
    out = jnp.zeros(bucket.shape, F32)
    for b in range(N_BUCKETS):
        out = jnp.where(bucket == b, rel_ref[b, head], out)
    return out


def _bias_by_head(bucket, head, rel_ref):
    acc = jnp.zeros(bucket.shape, F32)
    for h in range(H_DIFF):
        acc = jnp.where(head == h, _bias_of_bucket(bucket, rel_ref, h), acc)
    return acc


def _prep_kernel(rel_ref, lam_ref, tz_ref, dec_ref, lam_out_ref, *, blk, lam_init, n_tok):
    r = lax.broadcasted_iota(I32, (blk, blk), 0)
    c = lax.broadcasted_iota(I32, (blk, blk), 1)
    for off in range(2):
        bucket = _t5_bucket(r - c + off * blk)
        for h in range(H_DIFF):
            tz_ref[h, off] = _bias_of_bucket(bucket, rel_ref, h)
    key = lax.broadcasted_iota(I32, (LANES, PAGE_SIZE), 1)
    row = lax.broadcasted_iota(I32, (LANES, PAGE_SIZE), 0)
    qi = jnp.right_shift(row, SUBLANE_SHIFT) & (n_tok - 1)
    head = row & (SUBLANES - 1)
    dec_ref[0] = _bias_by_head(_t5_bucket(PAGE_SIZE + qi - key), head, rel_ref)
    dec_ref[1] = _bias_by_head(_t5_bucket(qi - key), head, rel_ref)
    dec_ref[2] = _bias_by_head(jnp.full((LANES, PAGE_SIZE), N_BUCKETS - 1, I32), head, rel_ref)
    lq1, lk1, lq2, lk2 = lam_ref[0:1, :], lam_ref[1:2, :], lam_ref[2:3, :], lam_ref[3:4, :]
    lam = (jnp.exp(jnp.sum(lq1 * lk1, axis=-1, keepdims=True))
           - jnp.exp(jnp.sum(lq2 * lk2, axis=-1, keepdims=True)) + lam_init)
    lam_out_ref[...] = jnp.broadcast_to(lam, (SUBLANES, LANES))


def _prep(rel_bias, lam_vecs, blk, lam_init, n_tok):
    assert 2 * n_tok * SUBLANES <= LANES
    _log2(n_tok)
    return pl.pallas_call(
        functools.partial(_prep_kernel, blk=blk, lam_init=lam_init, n_tok=n_tok),
        in_specs=[
            pl.BlockSpec(memory_space=pltpu.SMEM),
            pl.BlockSpec(memory_space=pltpu.VMEM),
        ],
        out_specs=[pl.BlockSpec(memory_space=pltpu.VMEM)] * 3,
        out_shape=[
            jax.ShapeDtypeStruct((H_DIFF, 2, blk, blk), F32),
            jax.ShapeDtypeStruct((3, PAGE_SIZE, LANES), F32),
            jax.ShapeDtypeStruct((SUBLANES, LANES), F32),
        ],
        name="prep_bias_lambda",
    )(rel_bias, lam_vecs)


def _sb_prompt_kernel(q_ref, k_ref, v_ref, o_ref, acc_ref, c_ref, *, tq, tk):
    i = pl.program_id(2)
    n_diag = tq // tk
    r = lax.broadcasted_iota(I32, (tk, tk), 0)
    c = lax.broadcasted_iota(I32, (tk, tk), 1)
    tri = (r >= c).astype(BF16)

    def block(j, row0, diag):
        start = pl.multiple_of(j * tk, tk)
        rows = tq - row0
        if diag:
            rr = lax.broadcasted_iota(I32, (rows, tk), 0)
            cc = lax.broadcasted_iota(I32, (rows, tk), 1)
            strict = cc < rr
        for hh in range(2):
            lo = hh * DH_SB
            q = q_ref[0, row0:tq, lo:lo + DH_SB]
            k = k_ref[0, pl.ds(start, tk), lo:lo + DH_SB]
            v = v_ref[0, pl.ds(start, tk), lo:lo + DH_SB]
            z = _dot_nt(q, k)
            drop = _softplus(z)
            if diag:
                drop = jnp.where(strict, drop, 0.0)
            suffix = _dot(drop.astype(BF16), tri)
            log_w = z - suffix - c_ref[hh, row0:tq]
            if diag:
                log_w = jnp.where(strict, log_w, NEG_INF)
            a = jnp.exp(log_w)
            acc_ref[hh, row0:tq] += _dot(a.astype(BF16), v)
            c_ref[hh, row0:tq] += suffix[:, 0:1]

    acc_ref[...] = jnp.zeros_like(acc_ref)
    c_ref[...] = jnp.zeros_like(c_ref)
    for dd in range(n_diag - 1, -1, -1):
        block(i * n_diag + dd, dd * tk, True)

    def body(kk, carry):
        block(i * n_diag - 1 - kk, 0, False)
        return carry

    lax.fori_loop(0, i * n_diag, body, 0)
    o_ref[0] = jnp.concatenate([acc_ref[0], acc_ref[1]], axis=1).astype(BF16)


def _sb_prompt(pb, tq, tk):
    b, t, _ = pb.shape
    qb, kb, vb = OFF_Q_SB // LANES, OFF_K_SB // LANES, OFF_V_SB // LANES
    return pl.pallas_call(
        functools.partial(_sb_prompt_kernel, tq=tq, tk=tk),
        grid=(b, H_SB // 2, t // tq),
        in_specs=[
            pl.BlockSpec((1, tq, LANES), lambda bi, hp, i: (bi, i, qb + hp)),
            pl.BlockSpec((1, t, LANES), lambda bi, hp, i: (bi, 0, kb + hp)),
            pl.BlockSpec((1, t, LANES), lambda bi, hp, i: (bi, 0, vb + hp)),
        ],
        out_specs=pl.BlockSpec((1, tq, LANES), lambda bi, hp, i: (bi, i, hp)),
        out_shape=jax.ShapeDtypeStruct((b, t, D_SB), BF16),
        scratch_shapes=[pltpu.VMEM((2, tq, DH_SB), F32), pltpu.VMEM((2, tq, 1), F32)],
        compiler_params=_params(("parallel", "parallel", "arbitrary")),
        name="sb_prompt",
    )(pb, pb, pb)


def _diff_prompt_kernel(lam_ref, rel_ref, q_ref, k_ref, v_ref, tz_ref, g_ref, o_ref,
                        q2_ref, m_ref, acc_ref, z_ref, *, blk, lam_init):
    hp = pl.program_id(1)
    i = pl.program_id(2)
    r = lax.broadcasted_iota(I32, (2 * blk, blk), 0)
    c = lax.broadcasted_iota(I32, (2 * blk, blk), 1)
    causal = c <= jnp.where(r >= blk, r - blk, r)
    lane = lax.broadcasted_iota(I32, (blk, DV_DIFF), 1)
    ones_col = (lane == 0).astype(BF16)
    scale = DQ_DIFF ** -0.5
    for hh in range(2):
        q = q_ref[0, :, hh * DV_DIFF:(hh + 1) * DV_DIFF]
        zero = jnp.zeros_like(q)
        q2_ref[hh, 0:blk, :] = jnp.where(lane < DQ_DIFF, q, zero)
        q2_ref[hh, blk:2 * blk, :] = jnp.where(lane >= DQ_DIFF, q, zero)
    m_ref[...] = jnp.full_like(m_ref, NEG_INF)
    acc_ref[...] = jnp.zeros_like(acc_ref)

    def logits(j, buf):
        start = pl.multiple_of(j * blk, blk)
        for hh in range(2):
            k = k_ref[0, pl.ds(start, blk), hh * DV_DIFF:(hh + 1) * DV_DIFF]
            z_ref[buf, hh] = _dot_nt(q2_ref[hh], k)

    def consume(j, buf, kind):
        start = pl.multiple_of(j * blk, blk)
        for hh in range(2):
            lo = hh * DV_DIFF
            v = jnp.concatenate([v_ref[0, pl.ds(start, blk), lo:lo + DV_DIFF], ones_col], axis=1)
            z = z_ref[buf, hh] * scale
            if kind == 2:
                z = z + rel_ref[N_BUCKETS - 1, hp * 2 + hh]
            else:
                bias = tz_ref[hh, kind]
                z = z + jnp.concatenate([bias, bias], axis=0)
            if kind == 0:
                z = jnp.where(causal, z, NEG_INF)
            m_prev = m_ref[hh]
            m_new = jnp.maximum(m_prev, jnp.max(z, axis=1, keepdims=True))
            alpha = jnp.exp(m_prev - m_new)
            p = jnp.exp(z - m_new)
            acc_ref[hh] = alpha * acc_ref[hh] + _dot(p.astype(BF16), v)
            m_ref[hh] = m_new

    logits(i, 0)
    logits(jnp.maximum(i - 1, 0), 1)
    consume(i, 0, 0)

    @pl.when(i >= 1)
    def _():
        logits(jnp.maximum(i - 2, 0), 0)
        consume(i - 1, 1, 1)

    n_far = jnp.maximum(i - 1, 0)

    def pair(p, carry):
        ja = i - 2 - 2 * p
        logits(ja - 1, 1)
        consume(ja, 0, 2)
        logits(jnp.maximum(ja - 2, 0), 0)
        consume(ja - 1, 1, 2)
        return carry

    lax.fori_loop(0, n_far // 2, pair, 0)

    @pl.when(n_far % 2 == 1)
    def _():
        consume(0, 0, 2)

    lam = lam_ref[0, 0]
    outs = []
    for hh in range(2):
        acc = acc_ref[hh]
        o = acc[:, 0:DV_DIFF] / acc[:, DV_DIFF:DV_DIFF + 1]
        o = o[0:blk] - lam * o[blk:2 * blk]
        outs.append(_rmsnorm(o, g_ref[...]) * (1.0 - lam_init))
    o_ref[0] = jnp.concatenate(outs, axis=1).astype(BF16)


def _diff_prompt(pb, tz, lam, rel_bias, g_diff, blk, lam_init):
    b, t, _ = pb.shape
    qb, kb, vb = OFF_Q_DF // LANES, OFF_K_DF // LANES, OFF_V_DF // LANES
    return pl.pallas_call(
        functools.partial(_diff_prompt_kernel, blk=blk, lam_init=lam_init),
        grid=(b, H_DIFF // 2, t // blk),
        in_specs=[
            pl.BlockSpec(memory_space=pltpu.SMEM),
            pl.BlockSpec(memory_space=pltpu.SMEM),
            pl.BlockSpec((1, blk, LANES), lambda bi, hp, i: (bi, i, qb + hp)),
            pl.BlockSpec((1, t, LANES), lambda bi, hp, i: (bi, 0, kb + hp)),
            pl.BlockSpec((1, t, LANES), lambda bi, hp, i: (bi, 0, vb + hp)),
            pl.BlockSpec((2, 2, blk, blk), lambda bi, hp, i: (hp, 0, 0, 0)),
            pl.BlockSpec((1, DV_DIFF), lambda bi, hp, i: (0, 0)),
        ],
        out_specs=pl.BlockSpec((1, blk, LANES), lambda bi, hp, i: (bi, i, hp)),
        out_shape=jax.ShapeDtypeStruct((b, t, D_DIFF), BF16),
        scratch_shapes=[
            pltpu.VMEM((2, 2 * blk, DV_DIFF), BF16),
            pltpu.VMEM((2, 2 * blk, 1), F32),
            pltpu.VMEM((2, 2 * blk, 2 * DV_DIFF), F32),
            pltpu.VMEM((2, 2, 2 * blk, blk), F32),
        ],
        compiler_params=_params(("parallel", "parallel", "arbitrary"), vmem=DIFF_VMEM_LIMIT),
        name="diff_prompt",
    )(lam, rel_bias, pb, pb, pb, tz, g_diff.reshape(1, DV_DIFF))


def _mem_attn_kernel(q_ref, k_ref, v_ref, o_ref, *, tq):
    rows = max(tq, SUBLANES)
    q_all = q_ref[0].astype(F32) * (DH_MEM ** -0.5)
    if rows > tq:
        q_all = jnp.concatenate([q_all, jnp.zeros((rows - tq, D_MEM), F32)], axis=0)
    q_all = q_all.astype(BF16)
    outs = []
    for h in range(H_MEM):
        lo = h * DH_MEM
        q = q_all[:, lo:lo + DH_MEM]
        k = k_ref[0, :, lo:lo + DH_MEM].astype(BF16)
        v = v_ref[0, :, lo:lo + DH_MEM].astype(BF16)
        z = _dot_nt(q, k)
        p = jnp.exp(z - jnp.max(z, axis=1, keepdims=True))
        outs.append(_dot(p.astype(BF16), v) / jnp.sum(p, axis=1, keepdims=True))
    o_ref[0] = jnp.concatenate(outs, axis=1)[0:tq]


def _mem_attn(q, q_block, k, k_block, v, v_block, tq, name):
    b, t, _ = q.shape
    m = k.shape[1]
    return pl.pallas_call(
        functools.partial(_mem_attn_kernel, tq=tq),
        grid=(b, t // tq),
        in_specs=[
            pl.BlockSpec((1, tq, D_MEM), lambda bi, i: (bi, i, q_block)),
            pl.BlockSpec((1, m, D_MEM), lambda bi, i: (bi, 0, k_block)),
            pl.BlockSpec((1, m, D_MEM), lambda bi, i: (bi, 0, v_block)),
        ],
        out_specs=pl.BlockSpec((1, tq, D_MEM), lambda bi, i: (bi, i, 0)),
        out_shape=jax.ShapeDtypeStruct((b, t, D_MEM), F32),
        compiler_params=_params(("parallel", "parallel")),
        name=name,
    )(q, k, v)


def _pad_rows(x, rows):
    return jnp.concatenate([x, jnp.zeros((rows - x.shape[0], x.shape[1]), x.dtype)], axis=0)


def _query_columns(q, n_heads, width):
    t = q.shape[0]
    rows = jnp.concatenate(
        [jnp.broadcast_to(q[i:i + 1], (SUBLANES, q.shape[1])) for i in range(t)], axis=0)
    r = lax.broadcasted_iota(I32, rows.shape, 0)
    c = lax.broadcasted_iota(I32, rows.shape, 1)
    rows = jnp.where(jnp.right_shift(c, _log2(width)) == (r & (SUBLANES - 1)), rows, 0.0)
    return rows


def _suffix_sum_lanes(x):
    lane = lax.broadcasted_iota(I32, x.shape, 1)
    shift = 1
    while shift < PAGE_SIZE:
        x = x + jnp.where(lane < PAGE_SIZE - shift, pltpu.roll(x, PAGE_SIZE - shift, axis=1), 0.0)
        shift *= 2
    return x


def _head_rows_to_tokens(acc_t, n_tok, width):
    rows = _pad_lanes(acc_t, LANES).T[0:n_tok * SUBLANES]
    rr = lax.broadcasted_iota(I32, rows.shape, 0)
    cc = lax.broadcasted_iota(I32, rows.shape, 1)
    rows = jnp.where(jnp.right_shift(cc, _log2(width)) == (rr & (SUBLANES - 1)), rows, 0.0)
    return rows


def _pad_lanes(x, lanes):
    return jnp.concatenate([x, jnp.zeros((x.shape[0], lanes - x.shape[1]), x.dtype)], axis=1)


def _sb_decode_kernel(pt_ref, q_ref, kn_ref, vn_ref, *refs, n_tok, n_pages, pps):
    kt_refs = refs[:pps]
    vt_refs = refs[pps:2 * pps]
    o_ref, qt_ref, acc_ref, c_ref = refs[2 * pps:]
    j = pl.program_id(1)
    n_rows = n_tok * SUBLANES

    def attend(z_pages, vt, masks):
        drops, suffixes = [], []
        for z, mask in zip(z_pages, masks):
            drop = _softplus(z)
            if mask is not None:
                drop = jnp.where(mask, drop, 0.0)
            suffixes.append(_suffix_sum_lanes(drop))
        carry = c_ref[...]
        weights = []
        for z, suffix, mask in zip(z_pages, suffixes, masks):
            log_w = z - suffix - carry
            if mask is not None:
                log_w = jnp.where(mask, log_w, NEG_INF)
            weights.append(jnp.exp(log_w).astype(BF16))
            carry = carry + suffix[:, 0:1]
        c_ref[...] = carry
        acc_ref[...] += _dot_nt(vt, jnp.concatenate(weights, axis=1))

    @pl.when(j == 0)
    def _():
        q = q_ref[0] * (DH_SB ** -0.5)
        qt_ref[...] = _query_columns(q, H_SB, DH_SB).astype(BF16)
        acc_ref[...] = jnp.zeros_like(acc_ref)
        c_ref[...] = jnp.zeros_like(c_ref)
        kn = _pad_rows(kn_ref[0], PAGE_SIZE).astype(BF16)
        vn_t = _pad_rows(vn_ref[0], PAGE_SIZE).T.astype(BF16)
        row = lax.broadcasted_iota(I32, (n_rows, PAGE_SIZE), 0)
        key = lax.broadcasted_iota(I32, (n_rows, PAGE_SIZE), 1)
        attend([_dot_nt(qt_ref[...], kn)], vn_t, [key < jnp.right_shift(row, SUBLANE_SHIFT)])

    kt = jnp.concatenate([ref[0].astype(BF16) for ref in kt_refs], axis=1)
    vt = jnp.concatenate([ref[0].astype(BF16) for ref in vt_refs], axis=1)
    z = _dot(qt_ref[...], kt)
    attend([z[:, p * PAGE_SIZE:(p + 1) * PAGE_SIZE] for p in range(pps)], vt,
           [None] * pps)

    @pl.when(j == n_pages // pps - 1)
    def _():
        rows = _head_rows_to_tokens(acc_ref[...], n_tok, DH_SB)
        o_ref[0] = jnp.sum(rows.reshape(n_tok, SUBLANES, D_SB), axis=1)


def _page_specs(width, n_pages, pps):
    def spec(p):
        return pl.BlockSpec(
            (1, width, PAGE_SIZE),
            lambda bi, j, pt: (pt[bi, n_pages - 1 - (j * pps + p)], 0, 0))
    return [spec(p) for p in range(pps)]


def _sb_decode(q, k_new, v_new, cache_kt, cache_vt, page_table):
    b, t, _ = q.shape
    n_pages = page_table.shape[1]
    pps = SB_PAGES_PER_STEP
    tok = pl.BlockSpec((1, t, D_SB), lambda bi, j, pt: (bi, 0, 0))
    grid_spec = pltpu.PrefetchScalarGridSpec(
        num_scalar_prefetch=1,
        grid=(b, n_pages // pps),
        in_specs=[tok, tok, tok] + _page_specs(D_SB, n_pages, pps) + _page_specs(D_SB, n_pages, pps),
        out_specs=tok,
        scratch_shapes=[
            pltpu.VMEM((t * SUBLANES, D_SB), BF16),
            pltpu.VMEM((D_SB, t * SUBLANES), F32),
            pltpu.VMEM((t * SUBLANES, 1), F32),
        ],
    )
    return pl.pallas_call(
        functools.partial(_sb_decode_kernel, n_tok=t, n_pages=n_pages, pps=pps),
        grid_spec=grid_spec,
        out_shape=jax.ShapeDtypeStruct((b, t, D_SB), F32),
        compiler_params=_params(("parallel", "arbitrary")),
        name="sb_decode",
    )(page_table, q, k_new, v_new, *([cache_kt] * pps), *([cache_vt] * pps))


def _diff_decode_kernel(pt_ref, lam_ref, q_ref, kn_ref, vn_ref, bias_ref, g_ref, *refs,
                        n_tok, n_pages, lam_init, pps):
    kt_refs = refs[:pps]
    vt_refs = refs[pps:2 * pps]
    o_ref, qt_ref, acc_ref, m_ref, l_ref = refs[2 * pps:]
    j = pl.program_id(1)
    half = n_tok * SUBLANES
    scale = DQ_DIFF ** -0.5

    def attend(z, vt):
        m_prev = m_ref[...]
        m_new = jnp.maximum(m_prev, jnp.max(z, axis=1, keepdims=True))
        alpha = jnp.exp(m_prev - m_new)
        p = jnp.exp(z - m_new)
        l_ref[...] = alpha * l_ref[...] + jnp.sum(p, axis=1, keepdims=True)
        acc_ref[...] = (acc_ref[...] * _col_to_row(alpha)[:, 0:2 * half]
                        + _dot_nt(vt, p.astype(BF16)))
        m_ref[...] = m_new

    @pl.when(j == 0)
    def _():
        q = _query_columns(q_ref[0], H_DIFF, 2 * DQ_DIFF)
        lane = lax.broadcasted_iota(I32, q.shape, 1)
        first = (jnp.right_shift(lane, _log2(DQ_DIFF)) & 1) == 0
        q2 = jnp.concatenate([jnp.where(first, q, 0.0), jnp.where(first, 0.0, q)], axis=0)
        qt_ref[...] = q2.astype(BF16)
        acc_ref[...] = jnp.zeros_like(acc_ref)
        m_ref[...] = jnp.full_like(m_ref, NEG_INF)
        l_ref[...] = jnp.zeros_like(l_ref)
        kn = _pad_rows(kn_ref[0], PAGE_SIZE).astype(BF16)
        vn_t = _pad_rows(vn_ref[0], PAGE_SIZE).T.astype(BF16)
        row = lax.broadcasted_iota(I32, (2 * half, PAGE_SIZE), 0)
        key = lax.broadcasted_iota(I32, (2 * half, PAGE_SIZE), 1)
        z_new = _dot_nt(qt_ref[...], kn) * scale + bias_ref[1, 0:2 * half]
        attend(jnp.where(key <= (jnp.right_shift(row, SUBLANE_SHIFT) & (n_tok - 1)), z_new, NEG_INF), vn_t)

    kt = jnp.concatenate([ref[0].astype(BF16) for ref in kt_refs], axis=1)
    vt = jnp.concatenate([ref[0].astype(BF16) for ref in vt_refs], axis=1)
    far = bias_ref[2, 0:2 * half]
    near = jnp.where(j == 0, bias_ref[0, 0:2 * half], far)
    bias = jnp.concatenate([near] + [far] * (pps - 1), axis=1)
    attend(_dot(qt_ref[...], kt) * scale + bias, vt)

    @pl.when(j == n_pages // pps - 1)
    def _():
        rows = _pad_lanes(acc_ref[...], LANES).T[0:2 * half] / l_ref[...]
        o = rows[0:half] - lam_ref[0, 0] * rows[half:2 * half]
        rr = lax.broadcasted_iota(I32, o.shape, 0)
        cc = lax.broadcasted_iota(I32, o.shape, 1)
        o = jnp.where(jnp.right_shift(cc, _log2(DV_DIFF)) == (rr & (SUBLANES - 1)), o, 0.0)
        ms = jnp.sum(o * o, axis=1, keepdims=True) * (1.0 / DV_DIFF)
        o = (o * lax.rsqrt(ms + EPS)) * g_ref[...] * (1.0 - lam_init)
        o_ref[0] = jnp.sum(o.reshape(n_tok, SUBLANES, D_DIFF), axis=1)


def _diff_decode(q, k_new, v_new, cache_kt, cache_vt, page_table, lam, dec_bias, g_tiled, lam_init):
    b, t, _ = q.shape
    n_pages = page_table.shape[1]
    pps = DIFF_PAGES_PER_STEP
    tok = pl.BlockSpec((1, t, D_DIFF), lambda bi, j, pt: (bi, 0, 0))
    grid_spec = pltpu.PrefetchScalarGridSpec(
        num_scalar_prefetch=1,
        grid=(b, n_pages // pps),
        in_specs=[
            pl.BlockSpec(memory_space=pltpu.SMEM),
            tok, tok, tok,
            pl.BlockSpec((3, LANES, PAGE_SIZE), lambda bi, j, pt: (0, 0, 0)),
            pl.BlockSpec((1, D_DIFF), lambda bi, j, pt: (0, 0)),
        ] + _page_specs(D_QK_DIFF, n_pages, pps) + _page_specs(D_DIFF, n_pages, pps),
        out_specs=tok,
        scratch_shapes=[
            pltpu.VMEM((2 * t * SUBLANES, D_QK_DIFF), BF16),
            pltpu.VMEM((D_DIFF, 2 * t * SUBLANES), F32),
            pltpu.VMEM((2 * t * SUBLANES, 1), F32),
            pltpu.VMEM((2 * t * SUBLANES, 1), F32),
        ],
    )
    return pl.pallas_call(
        functools.partial(_diff_decode_kernel, n_tok=t, n_pages=n_pages, lam_init=lam_init, pps=pps),
        grid_spec=grid_spec,
        out_shape=jax.ShapeDtypeStruct((b, t, D_DIFF), F32),
        compiler_params=_params(("parallel", "arbitrary")),
        name="diff_decode",
    )(page_table, lam, q, k_new, v_new, dec_bias, g_tiled,
      *([cache_kt] * pps), *([cache_vt] * pps))


def _finish_kernel(x_ref, osb_ref, odf_ref, omem_ref, gmix_ref, wg_ref, bg_ref, wsb_ref, wdf_ref,
                   wmem_ref, wout_ref, gffn_ref, wr_ref, br_ref, xmid_ref, xn_ref, route_ref, count_ref,
                   carry_ref):
    x = x_ref[...]
    u = _rmsnorm(x, gmix_ref[...]).astype(BF16)
    gates = jax.nn.sigmoid(_dot(u, wg_ref[...]) + bg_ref[...])
    h = (gates[:, 0:D_MODEL] * _dot(osb_ref[...].astype(BF16), wsb_ref[...])
         + gates[:, D_MODEL:2 * D_MODEL] * _dot(odf_ref[...].astype(BF16), wdf_ref[...])
         + gates[:, 2 * D_MODEL:3 * D_MODEL] * _dot(omem_ref[...].astype(BF16), wmem_ref[...]))
    xm = x + _dot(h.astype(BF16), wout_ref[...])
    xmid_ref[...] = xm
    xn = _rmsnorm(xm, gffn_ref[...])
    xn_ref[...] = xn
    xh, xl = _split_bf16(xn)
    wh, wl = _split_bf16(wr_ref[...])
    lg = _dot(xh, wh) + _dot(xh, wl) + _dot(xl, wh) + br_ref[...]
    lane = lax.broadcasted_iota(I32, lg.shape, 1)
    is_group = lane < N_GROUPS
    gl = jnp.where(is_group, lg, -jnp.inf)
    gmax = jnp.max(gl, axis=1, keepdims=True)
    grp = jnp.min(jnp.where(gl == gmax, lane, LANES), axis=1, keepdims=True)
    p_grp = 1.0 / jnp.sum(jnp.where(is_group, jnp.exp(gl - gmax), 0.0), axis=1, keepdims=True)
    in_group = (lane >= N_GROUPS) & (lane < N_GROUPS + N_EXPERTS) & (
        jnp.right_shift(lane - N_GROUPS, _log2(EXPERTS_PER_GROUP)) == grp)
    el = jnp.where(in_group, lg, -jnp.inf)
    v1 = jnp.max(el, axis=1, keepdims=True)
    i1 = jnp.min(jnp.where(el == v1, lane, LANES), axis=1, keepdims=True)
    el2 = jnp.where(lane == i1, -jnp.inf, el)
    v2 = jnp.max(el2, axis=1, keepdims=True)
    i2 = jnp.min(jnp.where(el2 == v2, lane, LANES), axis=1, keepdims=True)
    e = jnp.exp(v2 - v1)
    w1 = (1.0 / (1.0 + e)) * p_grp
    w2 = (e / (1.0 + e)) * p_grp
    @pl.when(pl.program_id(0) == 0)
    def _():
        carry_ref[...] = jnp.zeros_like(carry_ref)

    tm = lg.shape[0]
    oh1 = lane == i1 - N_GROUPS
    oh2 = lane == i2 - N_GROUPS
    both = jnp.where(oh1 | oh2, 1.0, 0.0)
    r = lax.broadcasted_iota(I32, (tm, tm), 0)
    c = lax.broadcasted_iota(I32, (tm, tm), 1)
    before = _dot((c < r).astype(BF16), both.astype(BF16)) + carry_ref[...]
    rank1 = jnp.sum(jnp.where(oh1, before, 0.0), axis=1, keepdims=True)
    rank2 = jnp.sum(jnp.where(oh2, before, 0.0), axis=1, keepdims=True)
    carry_ref[...] += jnp.sum(both, axis=0, keepdims=True)
    count_ref[...] = jnp.broadcast_to(carry_ref[...], count_ref.shape)

    rec = jnp.where(lane == R_EID1, (i1 - N_GROUPS).astype(F32), 0.0)
    rec = jnp.where(lane == R_EID2, (i2 - N_GROUPS).astype(F32), rec)
    rec = jnp.where(lane == R_W1, w1, rec)
    rec = jnp.where(lane == R_W2, w2, rec)
    rec = jnp.where(lane == R_RANK1, rank1, rec)
    rec = jnp.where(lane == R_RANK2, rank2, rec)
    route_ref[...] = rec


def _finish(x, o_sb, o_df, o_mem, g_mix, wg, bg, wsb, wdf, wmem, wout, g_ffn, wr, br, name):
    n, d = x.shape
    tm = min(FINISH_TILE, n)

    def rows(width):
        return pl.BlockSpec((tm, width), lambda i: (i, 0))

    def whole(a):
        return pl.BlockSpec(a.shape, lambda i: (0, 0))

    args = (x, o_sb, o_df, o_mem, g_mix.reshape(1, d), wg, bg.reshape(1, -1), wsb, wdf, wmem, wout,
            g_ffn.reshape(1, d), wr, br)
    in_specs = [rows(d), rows(D_SB), rows(D_DIFF), rows(D_MEM)] + [whole(a) for a in args[4:]]
    return pl.pallas_call(
        _finish_kernel,
        grid=(n // tm,),
        in_specs=in_specs,
        out_specs=[rows(d), rows(d), rows(LANES), pl.BlockSpec((SUBLANES, LANES), lambda i: (0, 0))],
        out_shape=[
            jax.ShapeDtypeStruct((n, d), F32),
            jax.ShapeDtypeStruct((n, d), F32),
            jax.ShapeDtypeStruct((n, LANES), F32),
            jax.ShapeDtypeStruct((SUBLANES, LANES), F32),
        ],
        scratch_shapes=[pltpu.VMEM((1, LANES), F32)],
        compiler_params=_params(("arbitrary",)),
        name=name,
    )(*args)


def _one_hots(route):
    lane = lax.broadcasted_iota(I32, route.shape, 1)
    oh1 = lane == route[:, R_EID1:R_EID1 + 1].astype(I32)
    oh2 = lane == route[:, R_EID2:R_EID2 + 1].astype(I32)
    return oh1, oh2


def _moe_dest_kernel(route_ref, count_ref, dest_ref, blk_ref):
    n_blk = blk_ref.shape[0]
    blocks = jnp.right_shift(count_ref[...].astype(I32) + (MOE_ROWS - 1), MOE_ROWS_LOG2).astype(F32)
    r = lax.broadcasted_iota(I32, (LANES, LANES), 0)
    c = lax.broadcasted_iota(I32, (LANES, LANES), 1)
    upto = (r <= c).astype(BF16)
    block_end = _dot(blocks.astype(BF16), upto)
    row_start = (block_end - blocks)[0:1] * float(MOE_ROWS)
    route = route_ref[...]
    oh1, oh2 = _one_hots(route)
    d1 = jnp.sum(jnp.where(oh1, row_start, 0.0), axis=1, keepdims=True) + route[:, R_RANK1:R_RANK1 + 1]
    d2 = jnp.sum(jnp.where(oh2, row_start, 0.0), axis=1, keepdims=True) + route[:, R_RANK2:R_RANK2 + 1]
    lane = lax.broadcasted_iota(I32, route.shape, 1)
    dest_ref[...] = jnp.where(lane == 0, d1, jnp.where(lane == 1, d2, 0.0)).astype(I32)

    @pl.when(pl.program_id(0) == 0)
    def _():
        b_idx = lax.broadcasted_iota(I32, (n_blk, LANES), 0).astype(F32)
        lane_b = lax.broadcasted_iota(I32, (n_blk, LANES), 1)
        done = (block_end[0:1] <= b_idx) & (lane_b < N_EXPERTS)
        expert = jnp.minimum(jnp.sum(jnp.where(done, 1.0, 0.0), axis=1, keepdims=True),
                             float(N_EXPERTS - 1))
        used = block_end[0:1, N_EXPERTS - 1:N_EXPERTS]
        blk_ref[...] = jnp.where(lane_b == 0, expert, jnp.where(lane_b == 1, used, 0.0)).astype(I32)


def _moe_dest(route, counts, n_blk):
    n = route.shape[0]
    tm = min(DEST_TILE, n)
    n_blk_pad = -(-n_blk // SUBLANES) * SUBLANES
    return pl.pallas_call(
        _moe_dest_kernel,
        grid=(n // tm,),
        in_specs=[pl.BlockSpec((tm, LANES), lambda i: (i, 0)),
                  pl.BlockSpec((SUBLANES, LANES), lambda i: (0, 0))],
        out_specs=[pl.BlockSpec((tm, LANES), lambda i: (i, 0)),
                   pl.BlockSpec((n_blk_pad, LANES), lambda i: (0, 0))],
        out_shape=[jax.ShapeDtypeStruct((n, LANES), I32),
                   jax.ShapeDtypeStruct((n_blk_pad, LANES), I32)],
        compiler_params=_params(("arbitrary",)),
        name="moe_dest",
    )(route, counts)


def _dispatch_kernel(dest_ref, x_ref, init_ref, xs_ref, sem):
    del init_ref
    tm = x_ref.shape[0]

    def row_copy(t, d):
        return pltpu.make_async_copy(x_ref.at[pl.ds(t, 1)], xs_ref.at[pl.ds(d, 1)], sem)

    def start(t, carry):
        row_copy(t, dest_ref[0, 0, 2 * t]).start(priority=0)
        row_copy(t, dest_ref[0, 0, 2 * t + 1]).start(priority=1)
        return carry

    def wait(t, carry):
        row_copy(t, dest_ref[0, 0, 2 * t]).wait()
        row_copy(t, dest_ref[0, 0, 2 * t + 1]).wait()
        return carry

    lax.fori_loop(0, tm, start, 0, unroll=ROW_DMA_UNROLL)
    lax.fori_loop(0, tm, wait, 0, unroll=ROW_DMA_UNROLL)


def _dispatch(dest_tiles, xn, n_rows):
    n, d = xn.shape
    tm = dest_tiles.shape[2] // 2
    return pl.pallas_call(
        _dispatch_kernel,
        grid=(n // tm,),
        in_specs=[
            pl.BlockSpec((1, 1, 2 * tm), lambda i: (i, 0, 0), memory_space=pltpu.SMEM),
            pl.BlockSpec((tm, d), lambda i: (i, 0)),
            pl.BlockSpec(memory_space=pl.ANY),
        ],
        out_specs=pl.BlockSpec(memory_space=pl.ANY),
        out_shape=jax.ShapeDtypeStruct((n_rows, d), F32),
        scratch_shapes=[pltpu.SemaphoreType.DMA(())],
        input_output_aliases={2: 0},
        compiler_params=_params(("arbitrary",)),
        name="moe_dispatch",
    )(dest_tiles, xn, jnp.zeros((n_rows, d), F32))


def _expert_kernel(be_ref, used_ref, xs_ref, wg_ref, wu_ref, wd_ref, y_ref, wg_b, wu_b, wd_b):
    b = pl.program_id(0)

    @pl.when(b < used_ref[0])
    def _():
        @pl.when((b == 0) | (be_ref[b] != be_ref[jnp.maximum(b - 1, 0)]))
        def _():
            wg_b[...] = wg_ref[0].astype(BF16)
            wu_b[...] = wu_ref[0].astype(BF16)
            wd_b[...] = wd_ref[0].astype(BF16)

        x = xs_ref[...].astype(BF16)
        h = jax.nn.silu(_dot(x, wg_b[...])) * _dot(x, wu_b[...])
        y_ref[...] = _dot(h.astype(BF16), wd_b[...])

    @pl.when(b >= used_ref[0])
    def _():
        y_ref[...] = jnp.zeros_like(y_ref)


def _experts(blk_exp, used, xs, wg, wu, wd):
    n_rows, d = xs.shape
    n_blk = n_rows // MOE_ROWS
    grid_spec = pltpu.PrefetchScalarGridSpec(
        num_scalar_prefetch=2,
        grid=(n_blk,),
        in_specs=[
            pl.BlockSpec((MOE_ROWS, d), lambda b, be, used: (b, 0)),
            pl.BlockSpec((1, d, D_EXPERT), lambda b, be, used: (be[b], 0, 0)),
            pl.BlockSpec((1, d, D_EXPERT), lambda b, be, used: (be[b], 0, 0)),
            pl.BlockSpec((1, D_EXPERT, d), lambda b, be, used: (be[b], 0, 0)),
        ],
        out_specs=pl.BlockSpec((MOE_ROWS, d), lambda b, be, used: (b, 0)),
        scratch_shapes=[pltpu.VMEM((d, D_EXPERT), BF16), pltpu.VMEM((d, D_EXPERT), BF16),
                        pltpu.VMEM((D_EXPERT, d), BF16)],
    )
    return pl.pallas_call(
        _expert_kernel,
        grid_spec=grid_spec,
        out_shape=jax.ShapeDtypeStruct((n_rows, d), F32),
        compiler_params=_params(("arbitrary",)),
        name="moe_experts",
    )(blk_exp, used, xs, wg, wu, wd)


def _combine_kernel(dest_ref, next_ref, route_ref, xmid_ref, g_ref, yb_ref, out_ref, buf_ref, sems):
    i = pl.program_id(0)
    tm = xmid_ref.shape[0]
    slot = i % 2

    def row_copy(s, t, k, d):
        return pltpu.make_async_copy(yb_ref.at[pl.ds(d, 1)], buf_ref.at[s, k, pl.ds(t, 1)], sems.at[s])

    def request(idx_ref, s):
        def start(t, carry):
            row_copy(s, t, 0, idx_ref[0, 0, 2 * t]).start(priority=0)
            row_copy(s, t, 1, idx_ref[0, 0, 2 * t + 1]).start(priority=1)
            return carry

        lax.fori_loop(0, tm, start, 0, unroll=ROW_DMA_UNROLL)

    @pl.when(i == 0)
    def _():
        request(dest_ref, 0)

    @pl.when(i + 1 < pl.num_programs(0))
    def _():
        request(next_ref, 1 - slot)

    def wait(t, carry):
        row_copy(slot, t, 0, 0).wait()
        row_copy(slot, t, 1, 0).wait()
        return carry

    lax.fori_loop(0, tm, wait, 0, unroll=ROW_DMA_UNROLL)
    route = route_ref[...]
    y = buf_ref[slot, 0] * route[:, R_W1:R_W1 + 1] + buf_ref[slot, 1] * route[:, R_W2:R_W2 + 1]
    out_ref[...] = _rmsnorm(xmid_ref[...] + y, g_ref[...])


def _combine(dest_tiles, route, xmid, g_final, yb):
    n, d = xmid.shape
    tm = dest_tiles.shape[2] // 2
    last = n // tm - 1
    return pl.pallas_call(
        _combine_kernel,
        grid=(n // tm,),
        in_specs=[
            pl.BlockSpec((1, 1, 2 * tm), lambda i: (i, 0, 0), memory_space=pltpu.SMEM),
            pl.BlockSpec((1, 1, 2 * tm), lambda i: (jnp.minimum(i + 1, last), 0, 0), memory_space=pltpu.SMEM),
            pl.BlockSpec((tm, LANES), lambda i: (i, 0)),
            pl.BlockSpec((tm, d), lambda i: (i, 0)),
            pl.BlockSpec((1, d), lambda i: (0, 0)),
            pl.BlockSpec(memory_space=pl.ANY),
        ],
        out_specs=pl.BlockSpec((tm, d), lambda i: (i, 0)),
        out_shape=jax.ShapeDtypeStruct((n, d), F32),
        scratch_shapes=[pltpu.VMEM((2, 2, tm, d), F32), pltpu.SemaphoreType.DMA((2,))],
        compiler_params=_params(("arbitrary",)),
        name="moe_combine",
    )(dest_tiles, dest_tiles, route, xmid, g_final.reshape(1, d), yb)


def _moe_and_final_norm(xmid, xn, route, counts, wg, wu, wd, g_final):
    n, d = xmid.shape
    n_blk = -(-(2 * n + N_EXPERTS * (MOE_ROWS - 1)) // MOE_ROWS)
    dest, blk = _moe_dest(route, counts, n_blk)
    tm = min(ROW_TILE, n)
    dest_tiles = dest[:, 0:2].reshape(n // tm, 1, 2 * tm)
    xs = _dispatch(dest_tiles, xn, n_blk * MOE_ROWS)
    yb = _experts(blk[0:n_blk, 0], blk[0, 1:2], xs, wg, wu, wd)
    return _combine(dest_tiles, route, xmid, g_final, yb)


def _col_scale():
    s = jnp.ones((1, D_IN), F32)
    return s.at[:, OFF_Q_SB:OFF_Q_SB + D_SB].set(DH_SB ** -0.5)


def kernel(x_prompt, x_sample, cache_sb_k, cache_sb_v, cache_diff_k, cache_diff_v, cache_mem_k, cache_mem_v, page_table, mem_prompt, norm_mix_g, w_in, diff_lam_q1, diff_lam_k1, diff_lam_q2, diff_lam_k2, diff_norm_g, mem_norm_g, w_mem_kv, w_gate, b_gate, w_br_sb, w_br_diff, w_br_mem, w_out, norm_ffn_g, w_router_group, b_router_group, w_router_expert, b_router_expert, w_exp_gate, w_exp_up, w_exp_down, rel_bias, norm_final_g):
    depth = w_in.shape[0]
    assert depth == 1, "single-layer stack only"
    assert page_table.shape[1] % SB_PAGES_PER_STEP == 0 and page_table.shape[1] % DIFF_PAGES_PER_STEP == 0
    b, t, d = x_prompt.shape
    bs, ts, _ = x_sample.shape
    n_mem = mem_prompt.shape[1]
    sb_tq, diff_blk = min(SB_QUERY_BLOCK, t), min(DIFF_BLOCK, t)
    lam_init = 0.8 - 0.6 * math.exp(-0.3 * 0)

    w_in_b = w_in[0].astype(BF16)
    wg_b, wsb_b, wdf_b = w_gate[0].astype(BF16), w_br_sb[0].astype(BF16), w_br_diff[0].astype(BF16)
    wmem_b, wout_b = w_br_mem[0].astype(BF16), w_out[0].astype(BF16)
    weg_b, weu_b, wed_b = w_exp_gate[0], w_exp_up[0], w_exp_down[0]
    pad = LANES - N_GROUPS - N_EXPERTS
    w_router = jnp.concatenate(
        [w_router_group[0], w_router_expert[0], jnp.zeros((d, pad), F32)], axis=1)
    b_router = jnp.concatenate(
        [b_router_group[0], b_router_expert[0], jnp.zeros((pad,), F32)]).reshape(1, LANES)
    lam_vecs = jnp.concatenate([diff_lam_q1, diff_lam_k1, diff_lam_q2, diff_lam_k2], axis=0)

    tz, dec_bias, lam_tile = _prep(rel_bias, lam_vecs, diff_blk, lam_init, ts)
    lam = lam_tile[0:1, 0:1]

    def heads_last(a_t, n_heads, width):
        bb, _, tt = a_t.shape
        return jnp.transpose(a_t.reshape(bb, n_heads, width, tt), (0, 3, 1, 2))[None]

    def keys_last(cache, n_heads, width):
        pool, page = cache.shape[1], cache.shape[2]
        return jnp.transpose(cache[0], (0, 2, 3, 1)).reshape(pool, n_heads * width, page)

    k_sb, v_sb, k_df, v_df, pb = _norm_proj(
        x_prompt, norm_mix_g[0], w_in_b, _col_scale(),
        [(OFF_K_SB, D_SB), (OFF_V_SB, D_SB), (OFF_K_DF, D_QK_DIFF), (OFF_V_DF, D_DIFF)], True,
        "proj_prompt")
    mk, mv, mem_b = _norm_proj(
        mem_prompt, mem_norm_g[0], w_mem_kv[0].astype(BF16),
        jnp.ones((1, 2 * D_MEM), F32), [(0, D_MEM), (D_MEM, D_MEM)], True, "proj_memory")
    o_sb = _sb_prompt(pb, sb_tq, SB_KEY_BLOCK)
    o_df = _diff_prompt(pb, tz, lam, rel_bias, diff_norm_g[0], diff_blk, lam_init)
    o_mem = _mem_attn(pb, OFF_Q_MEM // D_MEM, mem_b, 0, mem_b, 1, min(MEM_QUERY_BLOCK, t), "mem_prompt")
    xmid, xn, route, counts = _finish(
        x_prompt.reshape(b * t, d), o_sb.reshape(b * t, D_SB), o_df.reshape(b * t, D_DIFF),
        o_mem.reshape(b * t, D_MEM), norm_mix_g[0], wg_b, b_gate[0], wsb_b, wdf_b, wmem_b, wout_b,
        norm_ffn_g[0], w_router, b_router, "finish_prompt")
    y_prompt = _moe_and_final_norm(xmid, xn, route, counts, weg_b, weu_b, wed_b, norm_final_g)

    q_sb_s, k_sb_s, v_sb_s, q_df_s, k_df_s, v_df_s, q_mem_s, _ = _norm_proj(
        x_sample.reshape(1, bs * ts, d), norm_mix_g[0], w_in_b, jnp.ones((1, D_IN), F32),
        [(OFF_Q_SB, D_SB), (OFF_K_SB, D_SB), (OFF_V_SB, D_SB), (OFF_Q_DF, D_QK_DIFF),
         (OFF_K_DF, D_QK_DIFF), (OFF_V_DF, D_DIFF), (OFF_Q_MEM, D_MEM)], False, "proj_sample")
    o_sb_s = _sb_decode(
        q_sb_s.reshape(bs, ts, D_SB), k_sb_s.reshape(bs, ts, D_SB), v_sb_s.reshape(bs, ts, D_SB),
        keys_last(cache_sb_k, H_SB, DH_SB), keys_last(cache_sb_v, H_SB, DH_SB), page_table)
    o_df_s = _diff_decode(
        q_df_s.reshape(bs, ts, D_QK_DIFF), k_df_s.reshape(bs, ts, D_QK_DIFF),
        v_df_s.reshape(bs, ts, D_DIFF), keys_last(cache_diff_k, H_DIFF, 2 * DQ_DIFF),
        keys_last(cache_diff_v, H_DIFF, DV_DIFF), page_table, lam, dec_bias,
        jnp.tile(diff_norm_g[0], H_DIFF).reshape(1, D_DIFF), lam_init)
    o_mem_s = _mem_attn(
        q_mem_s.reshape(bs, ts, D_MEM), 0, cache_mem_k[0].reshape(bs, n_mem, D_MEM), 0,
        cache_mem_v[0].reshape(bs, n_mem, D_MEM), 0, ts, "mem_sample")
    xmid_s, xn_s, route_s, counts_s = _finish(
        x_sample.reshape(bs * ts, d), o_sb_s.reshape(bs * ts, D_SB), o_df_s.reshape(bs * ts, D_DIFF),
        o_mem_s.reshape(bs * ts, D_MEM), norm_mix_g[0], wg_b, b_gate[0], wsb_b, wdf_b, wmem_b, wout_b,
        norm_ffn_g[0], w_router, b_router, "finish_sample")
    y_sample = _moe_and_final_norm(xmid_s, xn_s, route_s, counts_s, weg_b, weu_b, wed_b, norm_final_g)

    return (y_prompt.reshape(b, t, d), y_sample.reshape(bs, ts, d),
            heads_last(k_sb, H_SB, DH_SB), heads_last(v_sb, H_SB, DH_SB),
            heads_last(k_df, H_DIFF, 2 * DQ_DIFF), heads_last(v_df, H_DIFF, DV_DIFF),
            heads_last(mk, H_MEM, DH_MEM), heads_last(mv, H_MEM, DH_MEM),
            k_sb_s.reshape(1, bs, ts, H_SB, DH_SB), v_sb_s.reshape(1, bs, ts, H_SB, DH_SB),
            k_df_s.reshape(1, bs, ts, H_DIFF, 2 * DQ_DIFF), v_df_s.reshape(1, bs, ts, H_DIFF, DV_DIFF))
```

```python
import functools
import math

import jax
import jax.numpy as jnp
from jax import lax
from jax.experimental import pallas as pl
from jax.experimental.pallas import tpu as pltpu

F32 = jnp.float32
BF16 = jnp.bfloat16
I32 = jnp.int32

D_MODEL = 1024
PAGE_SIZE = 128
H_SB = 8
DH_SB = 64
H_DIFF = 4
DQ_DIFF = 32
DV_DIFF = 64
H_MEM = 4
DH_MEM = 64
D_SB = H_SB * DH_SB
D_DIFF = H_DIFF * DV_DIFF
D_MEM = H_MEM * DH_MEM
D_QK_DIFF = H_DIFF * 2 * DQ_DIFF
D_IN = 3 * D_SB + 2 * D_QK_DIFF + D_DIFF + D_MEM
N_BUCKETS = 32
MAX_EXACT = 16
MAX_DISTANCE = 128
N_GROUPS = 4
EXPERTS_PER_GROUP = 8
N_EXPERTS = N_GROUPS * EXPERTS_PER_GROUP
D_EXPERT = 512
EPS = 1e-6
NEG_INF = -1e30

LANES = 128
SUBLANES = 8
VMEM_LIMIT = 48 * 1024 * 1024
DIFF_VMEM_LIMIT = 56 * 1024 * 1024

OFF_Q_SB = 0
OFF_K_SB = D_SB
OFF_V_SB = 2 * D_SB
OFF_Q_DF = 3 * D_SB
OFF_K_DF = OFF_Q_DF + D_QK_DIFF
OFF_V_DF = OFF_K_DF + D_QK_DIFF
OFF_Q_MEM = OFF_V_DF + D_DIFF

MEM_QUERY_BLOCK = 1024
SB_QUERY_BLOCK = 2048
SB_KEY_BLOCK = 256
DIFF_BLOCK = 512
MOE_ROWS = 256
TOK_TILE = 512
FINISH_TILE = 512
DEST_TILE = 1024
ROW_TILE = 512
ROW_DMA_UNROLL = 32
SB_PAGES_PER_STEP = 32
DIFF_PAGES_PER_STEP = 64

R_EID1, R_EID2, R_W1, R_W2, R_RANK1, R_RANK2 = 0, 1, 2, 3, 4, 5


def _log2(n):
    assert n > 0 and n & (n - 1) == 0, n
    return n.bit_length() - 1


SUBLANE_SHIFT = _log2(SUBLANES)
MOE_ROWS_LOG2 = _log2(MOE_ROWS)


def _params(sem, vmem=VMEM_LIMIT):
    return pltpu.CompilerParams(dimension_semantics=sem, vmem_limit_bytes=vmem)


def _rmsnorm(x, g):
    ms = jnp.mean(x * x, axis=-1, keepdims=True)
    return (x * lax.rsqrt(ms + EPS)) * g


def _dot(a, b):
    return jnp.dot(a, b, preferred_element_type=F32)


def _dot_nt(a, b):
    return lax.dot_general(a, b, (((1,), (1,)), ((), ())), preferred_element_type=F32)


def _split_bf16(x):
    hi = x.astype(BF16)
    lo = (x - hi.astype(F32)).astype(BF16)
    return hi, lo


def _softplus(z):
    neg_abs = pltpu.bitcast(pltpu.bitcast(z, jnp.uint32) | jnp.uint32(0x80000000), F32)
    return jnp.maximum(z, 0.0) + jnp.log(1.0 + jnp.exp(neg_abs))


def _col_to_row(v):
    n = v.shape[0]
    r = lax.broadcasted_iota(I32, (n, LANES), 0)
    c = lax.broadcasted_iota(I32, (n, LANES), 1)
    return jnp.sum(jnp.where(r == c, v, 0.0), axis=0, keepdims=True)


def _norm_proj_kernel(x_ref, g_ref, w_ref, s_ref, *out_refs, f32_cols, transposed):
    u = _rmsnorm(x_ref[0], g_ref[...]).astype(BF16)
    p = _dot(u, w_ref[...])
    for ref, (lo, width) in zip(out_refs[:-1], f32_cols):
        ref[0] = p[:, lo:lo + width].T if transposed else p[:, lo:lo + width]
    out_refs[-1][0] = (p * s_ref[...]).astype(BF16)


def _norm_proj(x, g, w_bf16, col_scale, f32_cols, transposed, name):
    b, t, d = x.shape
    n_out = w_bf16.shape[1]
    tm = min(TOK_TILE, t)
    if transposed:
        out_shape = [jax.ShapeDtypeStruct((b, width, t), F32) for _, width in f32_cols]
        out_specs = [pl.BlockSpec((1, width, tm), lambda bi, i: (bi, 0, i)) for _, width in f32_cols]
    else:
        out_shape = [jax.ShapeDtypeStruct((b, t, width), F32) for _, width in f32_cols]
        out_specs = [pl.BlockSpec((1, tm, width), lambda bi, i: (bi, i, 0)) for _, width in f32_cols]
    out_shape.append(jax.ShapeDtypeStruct((b, t, n_out), BF16))
    out_specs.append(pl.BlockSpec((1, tm, n_out), lambda bi, i: (bi, i, 0)))
    return pl.pallas_call(
        functools.partial(_norm_proj_kernel, f32_cols=tuple(f32_cols), transposed=transposed),
        grid=(b, t // tm),
        in_specs=[
            pl.BlockSpec((1, tm, d), lambda bi, i: (bi, i, 0)),
            pl.BlockSpec((1, d), lambda bi, i: (0, 0)),
            pl.BlockSpec((d, n_out), lambda bi, i: (0, 0)),
            pl.BlockSpec((1, n_out), lambda bi, i: (0, 0)),
        ],
        out_specs=out_specs,
        out_shape=out_shape,
        compiler_params=_params(("parallel", "parallel")),
        name=name,
    )(x, g.reshape(1, d), w_bf16, col_scale)


def _t5_bucket(delta):
    n = jnp.maximum(delta, 0)
    nf = jnp.maximum(n, 1).astype(F32)
    large = MAX_EXACT + (jnp.log(nf / MAX_EXACT) / math.log(MAX_DISTANCE / MAX_EXACT)
                         * (N_BUCKETS - MAX_EXACT)).astype(I32)
    large = jnp.minimum(large, N_BUCKETS - 1)
    return jnp.where(n < MAX_EXACT, n, large)


def _bias_of_bucket(bucket, rel_ref, head):
    out = jnp.zeros(bucket.shape, F32)
    for b in range(N_BUCKETS):
        out = jnp.where(bucket == b, rel_ref[b, head], out)
    return out


def _bias_by_head(bucket, head, rel_ref):
    acc = jnp.zeros(bucket.shape, F32)
    for h in range(H_DIFF):
        acc = jnp.where(head == h, _bias_of_bucket(bucket, rel_ref, h), acc)
    return acc


def _prep_kernel(rel_ref, lam_ref, tz_ref, dec_ref, lam_out_ref, *, blk, lam_init, n_tok):
    r = lax.broadcasted_iota(I32, (blk, blk), 0)
    c = lax.broadcasted_iota(I32, (blk, blk), 1)
    for off in range(2):
        bucket = _t5_bucket(r - c + off * blk)
        for h in range(H_DIFF):
            tz_ref[h, off] = _bias_of_bucket(bucket, rel_ref, h)
    key = lax.broadcasted_iota(I32, (LANES, PAGE_SIZE), 1)
    row = lax.broadcasted_iota(I32, (LANES, PAGE_SIZE), 0)
    qi = jnp.right_shift(row, SUBLANE_SHIFT) & (n_tok - 1)
    head = row & (SUBLANES - 1)
    dec_ref[0] = _bias_by_head(_t5_bucket(PAGE_SIZE + qi - key), head, rel_ref)
    dec_ref[1] = _bias_by_head(_t5_bucket(qi - key), head, rel_ref)
    dec_ref[2] = _bias_by_head(jnp.full((LANES, PAGE_SIZE), N_BUCKETS - 1, I32), head, rel_ref)
    lq1, lk1, lq2, lk2 = lam_ref[0:1, :], lam_ref[1:2, :], lam_ref[2:3, :], lam_ref[3:4, :]
    lam = (jnp.exp(jnp.sum(lq1 * lk1, axis=-1, keepdims=True))
           - jnp.exp(jnp.sum(lq2 * lk2, axis=-1, keepdims=True)) + lam_init)
    lam_out_ref[...] = jnp.broadcast_to(lam, (SUBLANES, LANES))


def _prep(rel_bias, lam_vecs, blk, lam_init, n_tok):
    assert 2 * n_tok * SUBLANES <= LANES
    _log2(n_tok)
    return pl.pallas_call(
        functools.partial(_prep_kernel, blk=blk, lam_init=lam_init, n_tok=n_tok),
        in_specs=[
            pl.BlockSpec(memory_space=pltpu.SMEM),
            pl.BlockSpec(memory_space=pltpu.VMEM),
        ],
        out_specs=[pl.BlockSpec(memory_space=pltpu.VMEM)] * 3,
        out_shape=[
            jax.ShapeDtypeStruct((H_DIFF, 2, blk, blk), F32),
            jax.ShapeDtypeStruct((3, PAGE_SIZE, LANES), F32),
            jax.ShapeDtypeStruct((SUBLANES, LANES), F32),
        ],
        name="prep_bias_lambda",
    )(rel_bias, lam_vecs)


def _sb_prompt_kernel(q_ref, k_ref, v_ref, o_ref, acc_ref, c_ref, *, tq, tk):
    i = pl.program_id(2)
    n_diag = tq // tk
    r = lax.broadcasted_iota(I32, (tk, tk), 0)
    c = lax.broadcasted_iota(I32, (tk, tk), 1)
    tri = (r >= c).astype(BF16)

    def block(j, row0, diag):
        start = pl.multiple_of(j * tk, tk)
        rows = tq - row0
        if diag:
            rr = lax.broadcasted_iota(I32, (rows, tk), 0)
            cc = lax.broadcasted_iota(I32, (rows, tk), 1)
            strict = cc < rr
        for hh in range(2):
            lo = hh * DH_SB
            q = q_ref[0, row0:tq, lo:lo + DH_SB]
            k = k_ref[0, pl.ds(start, tk), lo:lo + DH_SB]
            v = v_ref[0, pl.ds(start, tk), lo:lo + DH_SB]
            z = _dot_nt(q, k)
            drop = _softplus(z)
            if diag:
                drop = jnp.where(strict, drop, 0.0)
            suffix = _dot(drop.astype(BF16), tri)
            log_w = z - suffix - c_ref[hh, row0:tq]
            if diag:
                log_w = jnp.where(strict, log_w, NEG_INF)
            a = jnp.exp(log_w)
            acc_ref[hh, row0:tq] += _dot(a.astype(BF16), v)
            c_ref[hh, row0:tq] += suffix[:, 0:1]

    acc_ref[...] = jnp.zeros_like(acc_ref)
    c_ref[...] = jnp.zeros_like(c_ref)
    for dd in range(n_diag - 1, -1, -1):
        block(i * n_diag + dd, dd * tk, True)

    def body(kk, carry):
        block(i * n_diag - 1 - kk, 0, False)
        return carry

    lax.fori_loop(0, i * n_diag, body, 0)
    o_ref[0] = jnp.concatenate([acc_ref[0], acc_ref[1]], axis=1).astype(BF16)


def _sb_prompt(pb, tq, tk):
    b, t, _ = pb.shape
    qb, kb, vb = OFF_Q_SB // LANES, OFF_K_SB // LANES, OFF_V_SB // LANES
    return pl.pallas_call(
        functools.partial(_sb_prompt_kernel, tq=tq, tk=tk),
        grid=(b, H_SB // 2, t // tq),
        in_specs=[
            pl.BlockSpec((1, tq, LANES), lambda bi, hp, i: (bi, i, qb + hp)),
            pl.BlockSpec((1, t, LANES), lambda bi, hp, i: (bi, 0, kb + hp)),
            pl.BlockSpec((1, t, LANES), lambda bi, hp, i: (bi, 0, vb + hp)),
        ],
        out_specs=pl.BlockSpec((1, tq, LANES), lambda bi, hp, i: (bi, i, hp)),
        out_shape=jax.ShapeDtypeStruct((b, t, D_SB), BF16),
        scratch_shapes=[pltpu.VMEM((2, tq, DH_SB), F32), pltpu.VMEM((2, tq, 1), F32)],
        compiler_params=_params(("parallel", "parallel", "arbitrary")),
        name="sb_prompt",
    )(pb, pb, pb)


def _diff_prompt_kernel(lam_ref, rel_ref, q_ref, k_ref, v_ref, tz_ref, g_ref, o_ref,
                        q2_ref, m_ref, acc_ref, z_ref, *, blk, lam_init):
    hp = pl.program_id(1)
    i = pl.program_id(2)
    r = lax.broadcasted_iota(I32, (2 * blk, blk), 0)
    c = lax.broadcasted_iota(I32, (2 * blk, blk), 1)
    causal = c <= jnp.where(r >= blk, r - blk, r)
    lane = lax.broadcasted_iota(I32, (blk, DV_DIFF), 1)
    ones_col = (lane == 0).astype(BF16)
    scale = DQ_DIFF ** -0.5
    for hh in range(2):
        q = q_ref[0, :, hh * DV_DIFF:(hh + 1) * DV_DIFF]
        zero = jnp.zeros_like(q)
        q2_ref[hh, 0:blk, :] = jnp.where(lane < DQ_DIFF, q, zero)
        q2_ref[hh, blk:2 * blk, :] = jnp.where(lane >= DQ_DIFF, q, zero)
    m_ref[...] = jnp.full_like(m_ref, NEG_INF)
    acc_ref[...] = jnp.zeros_like(acc_ref)

    def logits(j, buf):
        start = pl.multiple_of(j * blk, blk)
        for hh in range(2):
            k = k_ref[0, pl.ds(start, blk), hh * DV_DIFF:(hh + 1) * DV_DIFF]
            z_ref[buf, hh] = _dot_nt(q2_ref[hh], k)

    def consume(j, buf, kind):
        start = pl.multiple_of(j * blk, blk)
        for hh in range(2):
            lo = hh * DV_DIFF
            v = jnp.concatenate([v_ref[0, pl.ds(start, blk), lo:lo + DV_DIFF], ones_col], axis=1)
            z = z_ref[buf, hh] * scale
            if kind == 2:
                z = z + rel_ref[N_BUCKETS - 1, hp * 2 + hh]
            else:
                bias = tz_ref[hh, kind]
                z = z + jnp.concatenate([bias, bias], axis=0)
            if kind == 0:
                z = jnp.where(causal, z, NEG_INF)
            m_prev = m_ref[hh]
            m_new = jnp.maximum(m_prev, jnp.max(z, axis=1, keepdims=True))
            alpha = jnp.exp(m_prev - m_new)
            p = jnp.exp(z - m_new)
            acc_ref[hh] = alpha * acc_ref[hh] + _dot(p.astype(BF16), v)
            m_ref[hh] = m_new

    logits(i, 0)
    logits(jnp.maximum(i - 1, 0), 1)
    consume(i, 0, 0)

    @pl.when(i >= 1)
    def _():
        logits(jnp.maximum(i - 2, 0), 0)
        consume(i - 1, 1, 1)

    n_far = jnp.maximum(i - 1, 0)

    def pair(p, carry):
        ja = i - 2 - 2 * p
        logits(ja - 1, 1)
        consume(ja, 0, 2)
        logits(jnp.maximum(ja - 2, 0), 0)
        consume(ja - 1, 1, 2)
        return carry

    lax.fori_loop(0, n_far // 2, pair, 0)

    @pl.when(n_far % 2 == 1)
    def _():
        consume(0, 0, 2)

    lam = lam_ref[0, 0]
    outs = []
    for hh in range(2):
        acc = acc_ref[hh]
        o = acc[:, 0:DV_DIFF] / acc[:, DV_DIFF:DV_DIFF + 1]
        o = o[0:blk] - lam * o[blk:2 * blk]
        outs.append(_rmsnorm(o, g_ref[...]) * (1.0 - lam_init))
    o_ref[0] = jnp.concatenate(outs, axis=1).astype(BF16)


def _diff_prompt(pb, tz, lam, rel_bias, g_diff, blk, lam_init):
    b, t, _ = pb.shape
    qb, kb, vb = OFF_Q_DF // LANES, OFF_K_DF // LANES, OFF_V_DF // LANES
    return pl.pallas_call(
        functools.partial(_diff_prompt_kernel, blk=blk, lam_init=lam_init),
        grid=(b, H_DIFF // 2, t // blk),
        in_specs=[
            pl.BlockSpec(memory_space=pltpu.SMEM),
            pl.BlockSpec(memory_space=pltpu.SMEM),
            pl.BlockSpec((1, blk, LANES), lambda bi, hp, i: (bi, i, qb + hp)),
            pl.BlockSpec((1, t, LANES), lambda bi, hp, i: (bi, 0, kb + hp)),
            pl.BlockSpec((1, t, LANES), lambda bi, hp, i: (bi, 0, vb + hp)),
            pl.BlockSpec((2, 2, blk, blk), lambda bi, hp, i: (hp, 0, 0, 0)),
            pl.BlockSpec((1, DV_DIFF), lambda bi, hp, i: (0, 0)),
        ],
        out_specs=pl.BlockSpec((1, blk, LANES), lambda bi, hp, i: (bi, i, hp)),
        out_shape=jax.ShapeDtypeStruct((b, t, D_DIFF), BF16),
        scratch_shapes=[
            pltpu.VMEM((2, 2 * blk, DV_DIFF), BF16),
            pltpu.VMEM((2, 2 * blk, 1), F32),
            pltpu.VMEM((2, 2 * blk, 2 * DV_DIFF), F32),
            pltpu.VMEM((2, 2, 2 * blk, blk), F32),
        ],
        compiler_params=_params(("parallel", "parallel", "arbitrary"), vmem=DIFF_VMEM_LIMIT),
        name="diff_prompt",
    )(lam, rel_bias, pb, pb, pb, tz, g_diff.reshape(1, DV_DIFF))


def _mem_attn_kernel(q_ref, k_ref, v_ref, o_ref, *, tq):
    rows = max(tq, SUBLANES)
    q_all = q_ref[0].astype(F32) * (DH_MEM ** -0.5)
    if rows > tq:
        q_all = jnp.concatenate([q_all, jnp.zeros((rows - tq, D_MEM), F32)], axis=0)
    q_all = q_all.astype(BF16)
    outs = []
    for h in range(H_MEM):
        lo = h * DH_MEM
        q = q_all[:, lo:lo + DH_MEM]
        k = k_ref[0, :, lo:lo + DH_MEM].astype(BF16)
        v = v_ref[0, :, lo:lo + DH_MEM].astype(BF16)
        z = _dot_nt(q, k)
        p = jnp.exp(z - jnp.max(z, axis=1, keepdims=True))
        outs.append(_dot(p.astype(BF16), v) / jnp.sum(p, axis=1, keepdims=True))
    o_ref[0] = jnp.concatenate(outs, axis=1)[0:tq]


def _mem_attn(q, q_block, k, k_block, v, v_block, tq, name):
    b, t, _ = q.shape
    m = k.shape[1]
    return pl.pallas_call(
        functools.partial(_mem_attn_kernel, tq=tq),
        grid=(b, t // tq),
        in_specs=[
            pl.BlockSpec((1, tq, D_MEM), lambda bi, i: (bi, i, q_block)),
            pl.BlockSpec((1, m, D_MEM), lambda bi, i: (bi, 0, k_block)),
            pl.BlockSpec((1, m, D_MEM), lambda bi, i: (bi, 0, v_block)),
        ],
        out_specs=pl.BlockSpec((1, tq, D_MEM), lambda bi, i: (bi, i, 0)),
        out_shape=jax.ShapeDtypeStruct((b, t, D_MEM), F32),
        compiler_params=_params(("parallel", "parallel")),
        name=name,
    )(q, k, v)


def _pad_rows(x, rows):
    return jnp.concatenate([x, jnp.zeros((rows - x.shape[0], x.shape[1]), x.dtype)], axis=0)


def _query_columns(q, n_heads, width):
    t = q.shape[0]
    rows = jnp.concatenate(
        [jnp.broadcast_to(q[i:i + 1], (SUBLANES, q.shape[1])) for i in range(t)], axis=0)
    r = lax.broadcasted_iota(I32, rows.shape, 0)
    c = lax.broadcasted_iota(I32, rows.shape, 1)
    rows = jnp.where(jnp.right_shift(c, _log2(width)) == (r & (SUBLANES - 1)), rows, 0.0)
    return rows


def _suffix_sum_lanes(x):
    lane = lax.broadcasted_iota(I32, x.shape, 1)
    shift = 1
    while shift < PAGE_SIZE:
        x = x + jnp.where(lane < PAGE_SIZE - shift, pltpu.roll(x, PAGE_SIZE - shift, axis=1), 0.0)
        shift *= 2
    return x


def _head_rows_to_tokens(acc_t, n_tok, width):
    rows = _pad_lanes(acc_t, LANES).T[0:n_tok * SUBLANES]
    rr = lax.broadcasted_iota(I32, rows.shape, 0)
    cc = lax.broadcasted_iota(I32, rows.shape, 1)
    rows = jnp.where(jnp.right_shift(cc, _log2(width)) == (rr & (SUBLANES - 1)), rows, 0.0)
    return rows


def _pad_lanes(x, lanes):
    return jnp.concatenate([x, jnp.zeros((x.shape[0], lanes - x.shape[1]), x.dtype)], axis=1)


def _sb_decode_kernel(pt_ref, q_ref, kn_ref, vn_ref, *refs, n_tok, n_pages, pps):
    kt_refs = refs[:pps]
    vt_refs = refs[pps:2 * pps]
    o_ref, qt_ref, acc_ref, c_ref = refs[2 * pps:]
    j = pl.program_id(1)
    n_rows = n_tok * SUBLANES

    def attend(z_pages, vt, masks):
        drops = []
        for z, mask in zip(z_pages, masks):
            drop = _softplus(z)
            if mask is not None:
                drop = jnp.where(mask, drop, 0.0)
            drops.append(drop)
        hi, lo_part = _split_bf16(jnp.concatenate(drops, axis=0))
        r2 = lax.broadcasted_iota(I32, (2 * PAGE_SIZE, PAGE_SIZE), 0)
        c2 = lax.broadcasted_iota(I32, (2 * PAGE_SIZE, PAGE_SIZE), 1)
        tri2 = ((r2 & (PAGE_SIZE - 1)) >= c2).astype(BF16)
        suffix_all = _dot(jnp.concatenate([hi, lo_part], axis=1), tri2)
        suffixes = [suffix_all[p * n_rows:(p + 1) * n_rows] for p in range(len(drops))]
        carry = c_ref[...]
        weights = []
        for z, suffix, mask in zip(z_pages, suffixes, masks):
            log_w = z - suffix - carry
            if mask is not None:
                log_w = jnp.where(mask, log_w, NEG_INF)
            weights.append(jnp.exp(log_w).astype(BF16))
            carry = carry + suffix[:, 0:1]
        c_ref[...] = carry
        acc_ref[...] += _dot_nt(vt, jnp.concatenate(weights, axis=1))

    @pl.when(j == 0)
    def _():
        q = q_ref[0] * (DH_SB ** -0.5)
        qt_ref[...] = _query_columns(q, H_SB, DH_SB).astype(BF16)
        acc_ref[...] = jnp.zeros_like(acc_ref)
        c_ref[...] = jnp.zeros_like(c_ref)
        kn = _pad_rows(kn_ref[0], PAGE_SIZE).astype(BF16)
        vn_t = _pad_rows(vn_ref[0], PAGE_SIZE).T.astype(BF16)
        row = lax.broadcasted_iota(I32, (n_rows, PAGE_SIZE), 0)
        key = lax.broadcasted_iota(I32, (n_rows, PAGE_SIZE), 1)
        attend([_dot_nt(qt_ref[...], kn)], vn_t, [key < jnp.right_shift(row, SUBLANE_SHIFT)])

    kt = jnp.concatenate([ref[0].astype(BF16) for ref in kt_refs], axis=1)
    vt = jnp.concatenate([ref[0].astype(BF16) for ref in vt_refs], axis=1)
    z = _dot(qt_ref[...], kt)
    attend([z[:, p * PAGE_SIZE:(p + 1) * PAGE_SIZE] for p in range(pps)], vt,
           [None] * pps)

    @pl.when(j == n_pages // pps - 1)
    def _():
        rows = _head_rows_to_tokens(acc_ref[...], n_tok, DH_SB)
        o_ref[0] = jnp.sum(rows.reshape(n_tok, SUBLANES, D_SB), axis=1)


def _page_specs(width, n_pages, pps):
    def spec(p):
        return pl.BlockSpec(
            (1, width, PAGE_SIZE),
            lambda bi, j, pt: (pt[bi, n_pages - 1 - (j * pps + p)], 0, 0))
    return [spec(p) for p in range(pps)]


def _sb_decode(q, k_new, v_new, cache_kt, cache_vt, page_table):
    b, t, _ = q.shape
    n_pages = page_table.shape[1]
    pps = SB_PAGES_PER_STEP
    tok = pl.BlockSpec((1, t, D_SB), lambda bi, j, pt: (bi, 0, 0))
    grid_spec = pltpu.PrefetchScalarGridSpec(
        num_scalar_prefetch=1,
        grid=(b, n_pages // pps),
        in_specs=[tok, tok, tok] + _page_specs(D_SB, n_pages, pps) + _page_specs(D_SB, n_pages, pps),
        out_specs=tok,
        scratch_shapes=[
            pltpu.VMEM((t * SUBLANES, D_SB), BF16),
            pltpu.VMEM((D_SB, t * SUBLANES), F32),
            pltpu.VMEM((t * SUBLANES, 1), F32),
        ],
    )
    return pl.pallas_call(
        functools.partial(_sb_decode_kernel, n_tok=t, n_pages=n_pages, pps=pps),
        grid_spec=grid_spec,
        out_shape=jax.ShapeDtypeStruct((b, t, D_SB), F32),
        compiler_params=_params(("parallel", "arbitrary")),
        name="sb_decode",
    )(page_table, q, k_new, v_new, *([cache_kt] * pps), *([cache_vt] * pps))


def _diff_decode_kernel(pt_ref, lam_ref, q_ref, kn_ref, vn_ref, bias_ref, g_ref, *refs,
                        n_tok, n_pages, lam_init, pps):
    kt_refs = refs[:pps]
    vt_refs = refs[pps:2 * pps]
    o_ref, qt_ref, acc_ref, m_ref, l_ref = refs[2 * pps:]
    j = pl.program_id(1)
    half = n_tok * SUBLANES
    scale = DQ_DIFF ** -0.5

    def attend(z, vt):
        m_prev = m_ref[...]
        m_new = jnp.maximum(m_prev, jnp.max(z, axis=1, keepdims=True))
        alpha = jnp.exp(m_prev - m_new)
        p = jnp.exp(z - m_new)
        l_ref[...] = alpha * l_ref[...] + jnp.sum(p, axis=1, keepdims=True)
        acc_ref[...] = (acc_ref[...] * _col_to_row(alpha)[:, 0:2 * half]
                        + _dot_nt(vt, p.astype(BF16)))
        m_ref[...] = m_new

    @pl.when(j == 0)
    def _():
        q = _query_columns(q_ref[0], H_DIFF, 2 * DQ_DIFF)
        lane = lax.broadcasted_iota(I32, q.shape, 1)
        first = (jnp.right_shift(lane, _log2(DQ_DIFF)) & 1) == 0
        q2 = jnp.concatenate([jnp.where(first, q, 0.0), jnp.where(first, 0.0, q)], axis=0)
        qt_ref[...] = q2.astype(BF16)
        acc_ref[...] = jnp.zeros_like(acc_ref)
        m_ref[...] = jnp.full_like(m_ref, NEG_INF)
        l_ref[...] = jnp.zeros_like(l_ref)
        kn = _pad_rows(kn_ref[0], PAGE_SIZE).astype(BF16)
        vn_t = _pad_rows(vn_ref[0], PAGE_SIZE).T.astype(BF16)
        row = lax.broadcasted_iota(I32, (2 * half, PAGE_SIZE), 0)
        key = lax.broadcasted_iota(I32, (2 * half, PAGE_SIZE), 1)
        z_new = _dot_nt(qt_ref[...], kn) * scale + bias_ref[1, 0:2 * half]
        attend(jnp.where(key <= (jnp.right_shift(row, SUBLANE_SHIFT) & (n_tok - 1)), z_new, NEG_INF), vn_t)

    kt = jnp.concatenate([ref[0].astype(BF16) for ref in kt_refs], axis=1)
    vt = jnp.concatenate([ref[0].astype(BF16) for ref in vt_refs], axis=1)
    far = bias_ref[2, 0:2 * half]
    near = jnp.where(j == 0, bias_ref[0, 0:2 * half], far)
    bias = jnp.concatenate([near] + [far] * (pps - 1), axis=1)
    attend(_dot(qt_ref[...], kt) * scale + bias, vt)

    @pl.when(j == n_pages // pps - 1)
    def _():
        rows = _pad_lanes(acc_ref[...], LANES).T[0:2 * half] / l_ref[...]
        o = rows[0:half] - lam_ref[0, 0] * rows[half:2 * half]
        rr = lax.broadcasted_iota(I32, o.shape, 0)
        cc = lax.broadcasted_iota(I32, o.shape, 1)
        o = jnp.where(jnp.right_shift(cc, _log2(DV_DIFF)) == (rr & (SUBLANES - 1)), o, 0.0)
        ms = jnp.sum(o * o, axis=1, keepdims=True) * (1.0 / DV_DIFF)
        o = (o * lax.rsqrt(ms + EPS)) * g_ref[...] * (1.0 - lam_init)
        o_ref[0] = jnp.sum(o.reshape(n_tok, SUBLANES, D_DIFF), axis=1)


def _diff_decode(q, k_new, v_new, cache_kt, cache_vt, page_table, lam, dec_bias, g_tiled, lam_init):
    b, t, _ = q.shape
    n_pages = page_table.shape[1]
    pps = DIFF_PAGES_PER_STEP
    tok = pl.BlockSpec((1, t, D_DIFF), lambda bi, j, pt: (bi, 0, 0))
    grid_spec = pltpu.PrefetchScalarGridSpec(
        num_scalar_prefetch=1,
        grid=(b, n_pages // pps),
        in_specs=[
            pl.BlockSpec(memory_space=pltpu.SMEM),
            tok, tok, tok,
            pl.BlockSpec((3, LANES, PAGE_SIZE), lambda bi, j, pt: (0, 0, 0)),
            pl.BlockSpec((1, D_DIFF), lambda bi, j, pt: (0, 0)),
        ] + _page_specs(D_QK_DIFF, n_pages, pps) + _page_specs(D_DIFF, n_pages, pps),
        out_specs=tok,
        scratch_shapes=[
            pltpu.VMEM((2 * t * SUBLANES, D_QK_DIFF), BF16),
            pltpu.VMEM((D_DIFF, 2 * t * SUBLANES), F32),
            pltpu.VMEM((2 * t * SUBLANES, 1), F32),
            pltpu.VMEM((2 * t * SUBLANES, 1), F32),
        ],
    )
    return pl.pallas_call(
        functools.partial(_diff_decode_kernel, n_tok=t, n_pages=n_pages, lam_init=lam_init, pps=pps),
        grid_spec=grid_spec,
        out_shape=jax.ShapeDtypeStruct((b, t, D_DIFF), F32),
        compiler_params=_params(("parallel", "arbitrary")),
        name="diff_decode",
    )(page_table, lam, q, k_new, v_new, dec_bias, g_tiled,
      *([cache_kt] * pps), *([cache_vt] * pps))


def _finish_kernel(x_ref, osb_ref, odf_ref, omem_ref, gmix_ref, wg_ref, bg_ref, wsb_ref, wdf_ref,
                   wmem_ref, wout_ref, gffn_ref, wr_ref, br_ref, xmid_ref, xn_ref, route_ref, count_ref,
                   carry_ref):
    x = x_ref[...]
    u = _rmsnorm(x, gmix_ref[...]).astype(BF16)
    gates = jax.nn.sigmoid(_dot(u, wg_ref[...]) + bg_ref[...])
    h = (gates[:, 0:D_MODEL] * _dot(osb_ref[...].astype(BF16), wsb_ref[...])
         + gates[:, D_MODEL:2 * D_MODEL] * _dot(odf_ref[...].astype(BF16), wdf_ref[...])
         + gates[:, 2 * D_MODEL:3 * D_MODEL] * _dot(omem_ref[...].astype(BF16), wmem_ref[...]))
    xm = x + _dot(h.astype(BF16), wout_ref[...])
    xmid_ref[...] = xm
    xn = _rmsnorm(xm, gffn_ref[...])
    xn_ref[...] = xn
    xh, xl = _split_bf16(xn)
    wh, wl = _split_bf16(wr_ref[...])
    lg = _dot(xh, wh) + _dot(xh, wl) + _dot(xl, wh) + br_ref[...]
    lane = lax.broadcasted_iota(I32, lg.shape, 1)
    is_group = lane < N_GROUPS
    gl = jnp.where(is_group, lg, -jnp.inf)
    gmax = jnp.max(gl, axis=1, keepdims=True)
    grp = jnp.min(jnp.where(gl == gmax, lane, LANES), axis=1, keepdims=True)
    p_grp = 1.0 / jnp.sum(jnp.where(is_group, jnp.exp(gl - gmax), 0.0), axis=1, keepdims=True)
    in_group = (lane >= N_GROUPS) & (lane < N_GROUPS + N_EXPERTS) & (
        jnp.right_shift(lane - N_GROUPS, _log2(EXPERTS_PER_GROUP)) == grp)
    el = jnp.where(in_group, lg, -jnp.inf)
    v1 = jnp.max(el, axis=1, keepdims=True)
    i1 = jnp.min(jnp.where(el == v1, lane, LANES), axis=1, keepdims=True)
    el2 = jnp.where(lane == i1, -jnp.inf, el)
    v2 = jnp.max(el2, axis=1, keepdims=True)
    i2 = jnp.min(jnp.where(el2 == v2, lane, LANES), axis=1, keepdims=True)
    e = jnp.exp(v2 - v1)
    w1 = (1.0 / (1.0 + e)) * p_grp
    w2 = (e / (1.0 + e)) * p_grp
    @pl.when(pl.program_id(0) == 0)
    def _():
        carry_ref[...] = jnp.zeros_like(carry_ref)

    tm = lg.shape[0]
    oh1 = lane == i1 - N_GROUPS
    oh2 = lane == i2 - N_GROUPS
    both = jnp.where(oh1 | oh2, 1.0, 0.0)
    r = lax.broadcasted_iota(I32, (tm, tm), 0)
    c = lax.broadcasted_iota(I32, (tm, tm), 1)
    before = _dot((c < r).astype(BF16), both.astype(BF16)) + carry_ref[...]
    rank1 = jnp.sum(jnp.where(oh1, before, 0.0), axis=1, keepdims=True)
    rank2 = jnp.sum(jnp.where(oh2, before, 0.0), axis=1, keepdims=True)
    carry_ref[...] += jnp.sum(both, axis=0, keepdims=True)
    count_ref[...] = jnp.broadcast_to(carry_ref[...], count_ref.shape)

    rec = jnp.where(lane == R_EID1, (i1 - N_GROUPS).astype(F32), 0.0)
    rec = jnp.where(lane == R_EID2, (i2 - N_GROUPS).astype(F32), rec)
    rec = jnp.where(lane == R_W1, w1, rec)
    rec = jnp.where(lane == R_W2, w2, rec)
    rec = jnp.where(lane == R_RANK1, rank1, rec)
    rec = jnp.where(lane == R_RANK2, rank2, rec)
    route_ref[...] = rec


def _finish(x, o_sb, o_df, o_mem, g_mix, wg, bg, wsb, wdf, wmem, wout, g_ffn, wr, br, name):
    n, d = x.shape
    tm = min(FINISH_TILE, n)

    def rows(width):
        return pl.BlockSpec((tm, width), lambda i: (i, 0))

    def whole(a):
        return pl.BlockSpec(a.shape, lambda i: (0, 0))

    args = (x, o_sb, o_df, o_mem, g_mix.reshape(1, d), wg, bg.reshape(1, -1), wsb, wdf, wmem, wout,
            g_ffn.reshape(1, d), wr, br)
    in_specs = [rows(d), rows(D_SB), rows(D_DIFF), rows(D_MEM)] + [whole(a) for a in args[4:]]
    return pl.pallas_call(
        _finish_kernel,
        grid=(n // tm,),
        in_specs=in_specs,
        out_specs=[rows(d), rows(d), rows(LANES), pl.BlockSpec((SUBLANES, LANES), lambda i: (0, 0))],
        out_shape=[
            jax.ShapeDtypeStruct((n, d), F32),
            jax.ShapeDtypeStruct((n, d), F32),
            jax.ShapeDtypeStruct((n, LANES), F32),
            jax.ShapeDtypeStruct((SUBLANES, LANES), F32),
        ],
        scratch_shapes=[pltpu.VMEM((1, LANES), F32)],
        compiler_params=_params(("arbitrary",)),
        name=name,
    )(*args)


def _one_hots(route):
    lane = lax.broadcasted_iota(I32, route.shape, 1)
    oh1 = lane == route[:, R_EID1:R_EID1 + 1].astype(I32)
    oh2 = lane == route[:, R_EID2:R_EID2 + 1].astype(I32)
    return oh1, oh2


def _moe_dest_kernel(route_ref, count_ref, dest_ref, blk_ref):
    n_blk = blk_ref.shape[0]
    blocks = jnp.right_shift(count_ref[...].astype(I32) + (MOE_ROWS - 1), MOE_ROWS_LOG2).astype(F32)
    r = lax.broadcasted_iota(I32, (LANES, LANES), 0)
    c = lax.broadcasted_iota(I32, (LANES, LANES), 1)
    upto = (r <= c).astype(BF16)
    block_end = _dot(blocks.astype(BF16), upto)
    row_start = (block_end - blocks)[0:1] * float(MOE_ROWS)
    route = route_ref[...]
    oh1, oh2 = _one_hots(route)
    d1 = jnp.sum(jnp.where(oh1, row_start, 0.0), axis=1, keepdims=True) + route[:, R_RANK1:R_RANK1 + 1]
    d2 = jnp.sum(jnp.where(oh2, row_start, 0.0), axis=1, keepdims=True) + route[:, R_RANK2:R_RANK2 + 1]
    lane = lax.broadcasted_iota(I32, route.shape, 1)
    dest_ref[...] = jnp.where(lane == 0, d1, jnp.where(lane == 1, d2, 0.0)).astype(I32)

    @pl.when(pl.program_id(0) == 0)
    def _():
        b_idx = lax.broadcasted_iota(I32, (n_blk, LANES), 0).astype(F32)
        lane_b = lax.broadcasted_iota(I32, (n_blk, LANES), 1)
        done = (block_end[0:1] <= b_idx) & (lane_b < N_EXPERTS)
        expert = jnp.minimum(jnp.sum(jnp.where(done, 1.0, 0.0), axis=1, keepdims=True),
                             float(N_EXPERTS - 1))
        used = block_end[0:1, N_EXPERTS - 1:N_EXPERTS]
        blk_ref[...] = jnp.where(lane_b == 0, expert, jnp.where(lane_b == 1, used, 0.0)).astype(I32)


def _moe_dest(route, counts, n_blk):
    n = route.shape[0]
    tm = min(DEST_TILE, n)
    n_blk_pad = -(-n_blk // SUBLANES) * SUBLANES
    return pl.pallas_call(
        _moe_dest_kernel,
        grid=(n // tm,),
        in_specs=[pl.BlockSpec((tm, LANES), lambda i: (i, 0)),
                  pl.BlockSpec((SUBLANES, LANES), lambda i: (0, 0))],
        out_specs=[pl.BlockSpec((tm, LANES), lambda i: (i, 0)),
                   pl.BlockSpec((n_blk_pad, LANES), lambda i: (0, 0))],
        out_shape=[jax.ShapeDtypeStruct((n, LANES), I32),
                   jax.ShapeDtypeStruct((n_blk_pad, LANES), I32)],
        compiler_params=_params(("arbitrary",)),
        name="moe_dest",
    )(route, counts)


def _dispatch_kernel(dest_ref, x_ref, init_ref, xs_ref, sem):
    del init_ref
    tm = x_ref.shape[0]

    def row_copy(t, d):
        return pltpu.make_async_copy(x_ref.at[pl.ds(t, 1)], xs_ref.at[pl.ds(d, 1)], sem)

    def start(t, carry):
        row_copy(t, dest_ref[0, 0, 2 * t]).start(priority=0)
        row_copy(t, dest_ref[0, 0, 2 * t + 1]).start(priority=1)
        return carry

    def wait(t, carry):
        row_copy(t, dest_ref[0, 0, 2 * t]).wait()
        row_copy(t, dest_ref[0, 0, 2 * t + 1]).wait()
        return carry

    lax.fori_loop(0, tm, start, 0, unroll=ROW_DMA_UNROLL)
    lax.fori_loop(0, tm, wait, 0, unroll=ROW_DMA_UNROLL)


def _dispatch(dest_tiles, xn, n_rows):
    n, d = xn.shape
    tm = dest_tiles.shape[2] // 2
    return pl.pallas_call(
        _dispatch_kernel,
        grid=(n // tm,),
        in_specs=[
            pl.BlockSpec((1, 1, 2 * tm), lambda i: (i, 0, 0), memory_space=pltpu.SMEM),
            pl.BlockSpec((tm, d), lambda i: (i, 0)),
            pl.BlockSpec(memory_space=pl.ANY),
        ],
        out_specs=pl.BlockSpec(memory_space=pl.ANY),
        out_shape=jax.ShapeDtypeStruct((n_rows, d), F32),
        scratch_shapes=[pltpu.SemaphoreType.DMA(())],
        input_output_aliases={2: 0},
        compiler_params=_params(("arbitrary",)),
        name="moe_dispatch",
    )(dest_tiles, xn, jnp.zeros((n_rows, d), F32))


def _expert_kernel(be_ref, used_ref, xs_ref, wg_ref, wu_ref, wd_ref, y_ref, wg_b, wu_b, wd_b):
    b = pl.program_id(0)

    @pl.when(b < used_ref[0])
    def _():
        @pl.when((b == 0) | (be_ref[b] != be_ref[jnp.maximum(b - 1, 0)]))
        def _():
            wg_b[...] = wg_ref[0].astype(BF16)
            wu_b[...] = wu_ref[0].astype(BF16)
            wd_b[...] = wd_ref[0].astype(BF16)

        x = xs_ref[...].astype(BF16)
        h = jax.nn.silu(_dot(x, wg_b[...])) * _dot(x, wu_b[...])
        y_ref[...] = _dot(h.astype(BF16), wd_b[...])

    @pl.when(b >= used_ref[0])
    def _():
        y_ref[...] = jnp.zeros_like(y_ref)


def _experts(blk_exp, used, xs, wg, wu, wd):
    n_rows, d = xs.shape
    n_blk = n_rows // MOE_ROWS
    grid_spec = pltpu.PrefetchScalarGridSpec(
        num_scalar_prefetch=2,
        grid=(n_blk,),
        in_specs=[
            pl.BlockSpec((MOE_ROWS, d), lambda b, be, used: (b, 0)),
            pl.BlockSpec((1, d, D_EXPERT), lambda b, be, used: (be[b], 0, 0)),
            pl.BlockSpec((1, d, D_EXPERT), lambda b, be, used: (be[b], 0, 0)),
            pl.BlockSpec((1, D_EXPERT, d), lambda b, be, used: (be[b], 0, 0)),
        ],
        out_specs=pl.BlockSpec((MOE_ROWS, d), lambda b, be, used: (b, 0)),
        scratch_shapes=[pltpu.VMEM((d, D_EXPERT), BF16), pltpu.VMEM((d, D_EXPERT), BF16),
                        pltpu.VMEM((D_EXPERT, d), BF16)],
    )
    return pl.pallas_call(
        _expert_kernel,
        grid_spec=grid_spec,
        out_shape=jax.ShapeDtypeStruct((n_rows, d), F32),
        compiler_params=_params(("arbitrary",)),
        name="moe_experts",
    )(blk_exp, used, xs, wg, wu, wd)


def _combine_kernel(dest_ref, next_ref, route_ref, xmid_ref, g_ref, yb_ref, out_ref, buf_ref, sems):
    i = pl.program_id(0)
    tm = xmid_ref.shape[0]
    slot = i % 2

    def row_copy(s, t, k, d):
        return pltpu.make_async_copy(yb_ref.at[pl.ds(d, 1)], buf_ref.at[s, k, pl.ds(t, 1)], sems.at[s])

    def request(idx_ref, s):
        def start(t, carry):
            row_copy(s, t, 0, idx_ref[0, 0, 2 * t]).start(priority=0)
            row_copy(s, t, 1, idx_ref[0, 0, 2 * t + 1]).start(priority=1)
            return carry

        lax.fori_loop(0, tm, start, 0, unroll=ROW_DMA_UNROLL)

    @pl.when(i == 0)
    def _():
        request(dest_ref, 0)

    @pl.when(i + 1 < pl.num_programs(0))
    def _():
        request(next_ref, 1 - slot)

    def wait(t, carry):
        row_copy(slot, t, 0, 0).wait()
        row_copy(slot, t, 1, 0).wait()
        return carry

    lax.fori_loop(0, tm, wait, 0, unroll=ROW_DMA_UNROLL)
    route = route_ref[...]
    y = buf_ref[slot, 0] * route[:, R_W1:R_W1 + 1] + buf_ref[slot, 1] * route[:, R_W2:R_W2 + 1]
    out_ref[...] = _rmsnorm(xmid_ref[...] + y, g_ref[...])


def _combine(dest_tiles, route, xmid, g_final, yb):
    n, d = xmid.shape
    tm = dest_tiles.shape[2] // 2
    last = n // tm - 1
    return pl.pallas_call(
        _combine_kernel,
        grid=(n // tm,),
        in_specs=[
            pl.BlockSpec((1, 1, 2 * tm), lambda i: (i, 0, 0), memory_space=pltpu.SMEM),
            pl.BlockSpec((1, 1, 2 * tm), lambda i: (jnp.minimum(i + 1, last), 0, 0), memory_space=pltpu.SMEM),
            pl.BlockSpec((tm, LANES), lambda i: (i, 0)),
            pl.BlockSpec((tm, d), lambda i: (i, 0)),
            pl.BlockSpec((1, d), lambda i: (0, 0)),
            pl.BlockSpec(memory_space=pl.ANY),
        ],
        out_specs=pl.BlockSpec((tm, d), lambda i: (i, 0)),
        out_shape=jax.ShapeDtypeStruct((n, d), F32),
        scratch_shapes=[pltpu.VMEM((2, 2, tm, d), F32), pltpu.SemaphoreType.DMA((2,))],
        compiler_params=_params(("arbitrary",)),
        name="moe_combine",
    )(dest_tiles, dest_tiles, route, xmid, g_final.reshape(1, d), yb)


def _moe_and_final_norm(xmid, xn, route, counts, wg, wu, wd, g_final):
    n, d = xmid.shape
    n_blk = -(-(2 * n + N_EXPERTS * (MOE_ROWS - 1)) // MOE_ROWS)
    dest, blk = _moe_dest(route, counts, n_blk)
    tm = min(ROW_TILE, n)
    dest_tiles = dest[:, 0:2].reshape(n // tm, 1, 2 * tm)
    xs = _dispatch(dest_tiles, xn, n_blk * MOE_ROWS)
    yb = _experts(blk[0:n_blk, 0], blk[0, 1:2], xs, wg, wu, wd)
    return _combine(dest_tiles, route, xmid, g_final, yb)


def _col_scale():
    s = jnp.ones((1, D_IN), F32)
    return s.at[:, OFF_Q_SB:OFF_Q_SB + D_SB].set(DH_SB ** -0.5)


def kernel(x_prompt, x_sample, cache_sb_k, cache_sb_v, cache_diff_k, cache_diff_v, cache_mem_k, cache_mem_v, page_table, mem_prompt, norm_mix_g, w_in, diff_lam_q1, diff_lam_k1, diff_lam_q2, diff_lam_k2, diff_norm_g, mem_norm_g, w_mem_kv, w_gate, b_gate, w_br_sb, w_br_diff, w_br_mem, w_out, norm_ffn_g, w_router_group, b_router_group, w_router_expert, b_router_expert, w_exp_gate, w_exp_up, w_exp_down, rel_bias, norm_final_g):
    depth = w_in.shape[0]
    assert depth == 1, "single-layer stack only"
    assert page_table.shape[1] % SB_PAGES_PER_STEP == 0 and page_table.shape[1] % DIFF_PAGES_PER_STEP == 0
    b, t, d = x_prompt.shape
    bs, ts, _ = x_sample.shape
    n_mem = mem_prompt.shape[1]
    sb_tq, diff_blk = min(SB_QUERY_BLOCK, t), min(DIFF_BLOCK, t)
    lam_init = 0.8 - 0.6 * math.exp(-0.3 * 0)

    w_in_b = w_in[0].astype(BF16)
    wg_b, wsb_b, wdf_b = w_gate[0].astype(BF16), w_br_sb[0].astype(BF16), w_br_diff[0].astype(BF16)
    wmem_b, wout_b = w_br_mem[0].astype(BF16), w_out[0].astype(BF16)
    weg_b, weu_b, wed_b = w_exp_gate[0], w_exp_up[0], w_exp_down[0]
    pad = LANES - N_GROUPS - N_EXPERTS
    w_router = jnp.concatenate(
        [w_router_group[0], w_router_expert[0], jnp.zeros((d, pad), F32)], axis=1)
    b_router = jnp.concatenate(
        [b_router_group[0], b_router_expert[0], jnp.zeros((pad,), F32)]).reshape(1, LANES)
    lam_vecs = jnp.concatenate([diff_lam_q1, diff_lam_k1, diff_lam_q2, diff_lam_k2], axis=0)

    tz, dec_bias, lam_tile = _prep(rel_bias, lam_vecs, diff_blk, lam_init, ts)
    lam = lam_tile[0:1, 0:1]

    def heads_last(a_t, n_heads, width):
        bb, _, tt = a_t.shape
        return jnp.transpose(a_t.reshape(bb, n_heads, width, tt), (0, 3, 1, 2))[None]

    def keys_last(cache, n_heads, width):
        pool, page = cache.shape[1], cache.shape[2]
        return jnp.transpose(cache[0], (0, 2, 3, 1)).reshape(pool, n_heads * width, page)

    k_sb, v_sb, k_df, v_df, pb = _norm_proj(
        x_prompt, norm_mix_g[0], w_in_b, _col_scale(),
        [(OFF_K_SB, D_SB), (OFF_V_SB, D_SB), (OFF_K_DF, D_QK_DIFF), (OFF_V_DF, D_DIFF)], True,
        "proj_prompt")
    mk, mv, mem_b = _norm_proj(
        mem_prompt, mem_norm_g[0], w_mem_kv[0].astype(BF16),
        jnp.ones((1, 2 * D_MEM), F32), [(0, D_MEM), (D_MEM, D_MEM)], True, "proj_memory")
    o_sb = _sb_prompt(pb, sb_tq, SB_KEY_BLOCK)
    o_df = _diff_prompt(pb, tz, lam, rel_bias, diff_norm_g[0], diff_blk, lam_init)
    o_mem = _mem_attn(pb, OFF_Q_MEM // D_MEM, mem_b, 0, mem_b, 1, min(MEM_QUERY_BLOCK, t), "mem_prompt")
    xmid, xn, route, counts = _finish(
        x_prompt.reshape(b * t, d), o_sb.reshape(b * t, D_SB), o_df.reshape(b * t, D_DIFF),
        o_mem.reshape(b * t, D_MEM), norm_mix_g[0], wg_b, b_gate[0], wsb_b, wdf_b, wmem_b, wout_b,
        norm_ffn_g[0], w_router, b_router, "finish_prompt")
    y_prompt = _moe_and_final_norm(xmid, xn, route, counts, weg_b, weu_b, wed_b, norm_final_g)

    q_sb_s, k_sb_s, v_sb_s, q_df_s, k_df_s, v_df_s, q_mem_s, _ = _norm_proj(
        x_sample.reshape(1, bs * ts, d), norm_mix_g[0], w_in_b, jnp.ones((1, D_IN), F32),
        [(OFF_Q_SB, D_SB), (OFF_K_SB, D_SB), (OFF_V_SB, D_SB), (OFF_Q_DF, D_QK_DIFF),
         (OFF_K_DF, D_QK_DIFF), (OFF_V_DF, D_DIFF), (OFF_Q_MEM, D_MEM)], False, "proj_sample")
    o_sb_s = _sb_decode(
        q_sb_s.reshape(bs, ts, D_SB), k_sb_s.reshape(bs, ts, D_SB), v_sb_s.reshape(bs, ts, D_SB),
        keys_last(cache_sb_k, H_SB, DH_SB), keys_last(cache_sb_v, H_SB, DH_SB), page_table)
    o_df_s = _diff_decode(
        q_df_s.reshape(bs, ts, D_QK_DIFF), k_df_s.reshape(bs, ts, D_QK_DIFF),
        v_df_s.reshape(bs, ts, D_DIFF), keys_last(cache_diff_k, H_DIFF, 2 * DQ_DIFF),
        keys_last(cache_diff_v, H_DIFF, DV_DIFF), page_table, lam, dec_bias,
        jnp.tile(diff_norm_g[0], H_DIFF).reshape(1, D_DIFF), lam_init)
    o_mem_s = _mem_attn(
        q_mem_s.reshape(bs, ts, D_MEM), 0, cache_mem_k[0].reshape(bs, n_mem, D_MEM), 0,
        cache_mem_v[0].reshape(bs, n_mem, D_MEM), 0, ts, "mem_sample")
    xmid_s, xn_s, route_s, counts_s = _finish(
        x_sample.reshape(bs * ts, d), o_sb_s.reshape(bs * ts, D_SB), o_df_s.reshape(bs * ts, D_DIFF),
        o_mem_s.reshape(bs * ts, D_MEM), norm_mix_g[0], wg_b, b_gate[0], wsb_b, wdf_b, wmem_b, wout_b,
        norm_ffn_g[0], w_router, b_router, "finish_sample")
    y_sample = _moe_and_final_norm(xmid_s, xn_s, route_s, counts_s, weg_b, weu_b, wed_b, norm_final_g)

    return (y_prompt.reshape(b, t, d), y_sample.reshape(bs, ts, d),
            heads_last(k_sb, H_SB, DH_SB), heads_last(v_sb, H_SB, DH_SB),
            heads_last(k_df, H_DIFF, 2 * DQ_DIFF), heads_last(v_df, H_DIFF, DV_DIFF),
            heads_last(mk, H_MEM, DH_MEM), heads_last(mv, H_MEM, DH_MEM),
            k_sb_s.reshape(1, bs, ts, H_SB, DH_SB), v_sb_s.reshape(1, bs, ts, H_SB, DH_SB),
            k_df_s.reshape(1, bs, ts, H_DIFF, 2 * DQ_DIFF), v_df_s.reshape(1, bs, ts, H_DIFF, DV_DIFF))
```
